```python
import math
import jax, jax.numpy as jnp
from jax import lax
import numpy as np

D_MODEL = 1024
BATCH = 2
SEQ = 8192
DEPTH = 1
DEC_BATCH = 32
DEC_SEQ = 1
PAST_LEN = 8192
PAGE_SIZE = 128

HEAD_DIM = 64
N_HEADS = D_MODEL // HEAD_DIM
N_KV_HEADS = 4
ROT_DIM = HEAD_DIM // 4
ROPE_THETA = 500000.0
IDX_HEADS = 8
IDX_DIM = 64
TOPK_MAX = 256
Q_BLOCK = 128
SSM_HEAD_DIM = 64
SSM_HEADS = D_MODEL // SSM_HEAD_DIM
D_INNER = SSM_HEADS * SSM_HEAD_DIM
SSM_GROUPS = 4
D_STATE = 128
CONV_W = 4
CONV_DIM = D_INNER + 2 * SSM_GROUPS * D_STATE
SSD_CHUNK = 128
D_FF = 4 * D_MODEL
EPS = 1e-6

ATTN_W = N_HEADS * HEAD_DIM
KV_W = N_KV_HEADS * HEAD_DIM
IN_WIDTHS = (ATTN_W, KV_W, KV_W, IDX_HEADS * IDX_DIM, IDX_DIM, IDX_HEADS, D_INNER, CONV_DIM, SSM_HEADS, D_MODEL, D_MODEL)
D_IN_PROJ = ATTN_W + 2 * KV_W + IDX_HEADS * IDX_DIM + IDX_DIM + IDX_HEADS + D_INNER + CONV_DIM + SSM_HEADS + 2 * D_MODEL

kernel_name = "dsa_ssd_gated_hybrid_decode_step"


def _split(a, widths):
    idx = [int(i) for i in np.cumsum(np.array(widths))[:-1]]
    return jnp.split(a, idx, axis=-1)


def rmsnorm(x, w):
    xf = x.astype(jnp.float32)
    return xf * lax.rsqrt(jnp.mean(xf * xf, axis=-1, keepdims=True) + EPS) * w


def rope_partial(x, pos):
    half = ROT_DIM // 2
    inv = ROPE_THETA ** (-jnp.arange(half, dtype=jnp.float32) * 2.0 / ROT_DIM)
    ang = pos.astype(jnp.float32)[:, None] * inv[None, :]
    cos = jnp.cos(ang)[:, None, :]
    sin = jnp.sin(ang)[:, None, :]
    x1 = x[..., :half]
    x2 = x[..., half:ROT_DIM]
    return jnp.concatenate([x1 * cos - x2 * sin, x1 * sin + x2 * cos, x[..., ROT_DIM:]], axis=-1)


def sparse_attend(q, iq, iw, k, v, ik, q_pos, k_pos, topk):
    B, Tq = q.shape[0], q.shape[1]
    isc = jax.nn.relu(jnp.einsum('bthd,bsd->bths', iq, ik).astype(jnp.float32) * IDX_DIM ** -0.5)
    isc = jnp.einsum('bths,bth->bts', isc, iw.astype(jnp.float32))
    admissible = k_pos[None, :] <= q_pos[:, None]
    isc = jnp.where(admissible[None], isc, -jnp.inf)
    top_val, top_idx = lax.top_k(isc, topk)
    valid = jnp.isfinite(top_val)
    gather = jax.vmap(lambda a, i: a[i])
    kg = gather(k, top_idx)
    vg = gather(v, top_idx)
    qg = q.reshape(B, Tq, N_KV_HEADS, N_HEADS // N_KV_HEADS, HEAD_DIM)
    s = jnp.einsum('bthgd,btjhd->bthgj', qg, kg).astype(jnp.float32) * HEAD_DIM ** -0.5
    s = jnp.where(valid[:, :, None, None, :], s, -jnp.inf)
    p = jax.nn.softmax(s, axis=-1)
    o = jnp.einsum('bthgj,btjhd->bthgd', p.astype(vg.dtype), vg)
    return o.reshape(B, Tq, N_HEADS * HEAD_DIM)


def blocked_self_sparse_attention(q, iq, iw, k, v, ik, pos):
    B, T = q.shape[0], q.shape[1]
    nb = T // Q_BLOCK
    topk = min(TOPK_MAX, T // 4)

    def to_blocks(a):
        return jnp.swapaxes(a.reshape(B, nb, Q_BLOCK, *a.shape[2:]), 0, 1)

    def blk(args):
        qb, iqb, iwb, pb = args
        return sparse_attend(qb, iqb, iwb, k, v, ik, pb, pos, topk)

    out = lax.map(blk, (to_blocks(q), to_blocks(iq), to_blocks(iw), pos.reshape(nb, Q_BLOCK)))
    return jnp.swapaxes(out, 0, 1).reshape(B, T, N_HEADS * HEAD_DIM)


def causal_conv(u, prev, w, b):
    up = jnp.concatenate([prev.astype(u.dtype), u], axis=1)
    out = lax.conv_general_dilated(up, w[:, None, :].astype(u.dtype), (1,), 'VALID',
                                   dimension_numbers=('NWC', 'WIO', 'NWC'),
                                   feature_group_count=u.shape[-1])
    return jax.nn.silu(out + b), up[:, up.shape[1] - (CONV_W - 1):]


def ssd_scan(x, dt, a, bm, cm, s0, chunk):
    B, T, H, P = x.shape
    G, N = bm.shape[2], bm.shape[3]
    HG = H // G
    nc = T // chunk
    f32 = jnp.float32
    x = x.reshape(B, nc, chunk, G, HG, P).astype(f32)
    dt = dt.reshape(B, nc, chunk, G, HG).astype(f32)
    bm = bm.reshape(B, nc, chunk, G, N).astype(f32)
    cm = cm.reshape(B, nc, chunk, G, N).astype(f32)
    cs = jnp.cumsum(dt * a.reshape(G, HG).astype(f32), axis=2)
    seg = cs[:, :, :, None] - cs[:, :, None, :]
    mask = jnp.tril(jnp.ones((chunk, chunk), bool))[:, :, None, None]
    decay = jnp.exp(jnp.where(mask, seg, -jnp.inf))
    xdt = x * dt[..., None]
    cb = jnp.einsum('bcign,bcjgn->bcijg', cm, bm)
    y_diag = jnp.einsum('bcijg,bcijgh,bcjghp->bcighp', cb, decay, xdt)
    decay_end = jnp.exp(cs[:, :, -1:] - cs)
    st = jnp.einsum('bcjgn,bcjgh,bcjghp->bcghpn', bm, decay_end, xdt)
    chunk_decay = jnp.exp(cs[:, :, -1])

    def step(s, inp):
        dec, stc = inp
        return s * dec[..., None, None] + stc, s

    s_init = s0.astype(f32).reshape(B, G, HG, P, N)
    s_last, s_start = lax.scan(step, s_init, (jnp.moveaxis(chunk_decay, 1, 0), jnp.moveaxis(st, 1, 0)))
    s_start = jnp.moveaxis(s_start, 0, 1)
    y_off = jnp.einsum('bcign,bcghpn,bcigh->bcighp', cm, s_start, jnp.exp(cs))
    y = (y_diag + y_off).reshape(B, T, H, P)
    return y, s_last.reshape(B, H, P, N)


def gated_group_rmsnorm(y, z, w):
    g = (y * jax.nn.silu(z)).astype(jnp.float32)
    g = g.reshape(*y.shape[:-1], SSM_GROUPS, D_INNER // SSM_GROUPS)
    g = g * lax.rsqrt(jnp.mean(g * g, axis=-1, keepdims=True) + EPS)
    return g.reshape(y.shape) * w


def hybrid_layer(x, c, pos, past_kv, conv_prev, ssm_prev, w_ada, b_ada, norm1_w, w_in, conv_w, conv_b,
                 dt_bias, a_log, d_skip, ssm_norm_w, w_out, norm2_w, w_up, w_down):
    B, T, _ = x.shape
    mod = jax.nn.silu(c) @ w_ada + b_ada
    sh1, sc1, gt1, sh2, sc2, gt2 = [m[:, None, :] for m in jnp.split(mod, 6, axis=-1)]
    h = rmsnorm(x, norm1_w) * (1.0 + sc1) + sh1
    q, k, v, iq, ik, iw, z, xbc, dt, g_att, g_ssm = _split(h @ w_in, IN_WIDTHS)
    q = rope_partial(q.reshape(B, T, N_HEADS, HEAD_DIM), pos)
    k = rope_partial(k.reshape(B, T, N_KV_HEADS, HEAD_DIM), pos)
    v = v.reshape(B, T, N_KV_HEADS, HEAD_DIM)
    iq = rope_partial(iq.reshape(B, T, IDX_HEADS, IDX_DIM), pos)
    ik = rope_partial(ik[:, :, None, :], pos)[:, :, 0, :]
    iw = iw * IDX_HEADS ** -0.5
    if past_kv is None:
        att = blocked_self_sparse_attention(q, iq, iw, k, v, ik, pos)
    else:
        pk, pv, pik = past_kv
        kk = jnp.concatenate([pk.astype(k.dtype), k], axis=1)
        vv = jnp.concatenate([pv.astype(v.dtype), v], axis=1)
        ikk = jnp.concatenate([pik.astype(ik.dtype), ik], axis=1)
        L = kk.shape[1]
        att = sparse_attend(q, iq, iw, kk, vv, ikk, pos, jnp.arange(L), min(TOPK_MAX, L // 4))
    xbc, conv_new = causal_conv(xbc, conv_prev, conv_w, conv_b)
    xs, bm, cm = _split(xbc, (D_INNER, SSM_GROUPS * D_STATE, SSM_GROUPS * D_STATE))
    xs = xs.reshape(B, T, SSM_HEADS, SSM_HEAD_DIM)
    dt = jax.nn.softplus((dt + dt_bias).astype(jnp.float32))
    a = -jnp.exp(a_log.astype(jnp.float32))
    chunk = SSD_CHUNK if T % SSD_CHUNK == 0 else T
    y, ssm_new = ssd_scan(xs, dt, a, bm.reshape(B, T, SSM_GROUPS, D_STATE),
                          cm.reshape(B, T, SSM_GROUPS, D_STATE), ssm_prev, chunk)
    y = y + d_skip[:, None] * xs
    y = gated_group_rmsnorm(y.reshape(B, T, D_INNER), z, ssm_norm_w)
    merged = jax.nn.sigmoid(g_att) * att + jax.nn.sigmoid(g_ssm) * y
    x = x + gt1 * (merged @ w_out)
    h2 = rmsnorm(x, norm2_w) * (1.0 + sc2) + sh2
    x = x + gt2 * (jnp.square(jax.nn.relu(h2 @ w_up)) @ w_down)
    return x, (k, v, ik, conv_new, ssm_new)


def gather_pages(pool, page_table):
    g = pool[page_table]
    return g.reshape(g.shape[0], g.shape[1] * g.shape[2], *g.shape[3:])


def setup_inputs(seed: int = 0) -> dict:
    key = jax.random.key(seed)
    ks = jax.random.split(key, 32)
    n_pages = PAST_LEN // PAGE_SIZE
    n_phys = (5 * DEC_BATCH * n_pages) // 4

    def nrm(k, shape, s=1.0):
        return jax.random.normal(k, shape, jnp.float32) * s

    page_table = jax.random.permutation(ks[0], n_phys)[:DEC_BATCH * n_pages].reshape(DEC_BATCH, n_pages).astype(jnp.int32)
    dt0 = jnp.exp(jax.random.uniform(ks[1], (DEPTH, SSM_HEADS), jnp.float32, math.log(1e-3), math.log(1e-1)))
    return {
        "x_prompt": nrm(ks[2], (BATCH, SEQ, D_MODEL)),
        "x_sample": nrm(ks[3], (DEC_BATCH, DEC_SEQ, D_MODEL)),
        "cache_k": nrm(ks[4], (DEPTH, n_phys, PAGE_SIZE, N_KV_HEADS, HEAD_DIM)),
        "cache_v": nrm(ks[5], (DEPTH, n_phys, PAGE_SIZE, N_KV_HEADS, HEAD_DIM)),
        "cache_idx_k": nrm(ks[6], (DEPTH, n_phys, PAGE_SIZE, IDX_DIM)),
        "state_conv": nrm(ks[7], (DEPTH, DEC_BATCH, CONV_W - 1, CONV_DIM)),
        "state_ssm": nrm(ks[8], (DEPTH, DEC_BATCH, SSM_HEADS, SSM_HEAD_DIM, D_STATE), 0.5),
        "page_table": page_table,
        "c_prompt": nrm(ks[9], (BATCH, D_MODEL)),
        "c_sample": nrm(ks[10], (DEC_BATCH, D_MODEL)),
        "w_ada": nrm(ks[11], (DEPTH, D_MODEL, 6 * D_MODEL), D_MODEL ** -0.5),
        "b_ada": nrm(ks[12], (DEPTH, 6 * D_MODEL), 0.02),
        "norm1_w": 1.0 + nrm(ks[13], (DEPTH, D_MODEL), 0.02),
        "w_in": nrm(ks[14], (DEPTH, D_MODEL, D_IN_PROJ), D_MODEL ** -0.5),
        "conv_w": nrm(ks[15], (DEPTH, CONV_W, CONV_DIM), CONV_W ** -0.5),
        "conv_b": nrm(ks[16], (DEPTH, CONV_DIM), 0.02),
        "dt_bias": dt0 + jnp.log(-jnp.expm1(-dt0)),
        "a_log": jnp.log(jax.random.uniform(ks[17], (DEPTH, SSM_HEADS), jnp.float32, 1.0, 16.0)),
        "d_skip": 1.0 + nrm(ks[18], (DEPTH, SSM_HEADS), 0.02),
        "ssm_norm_w": 1.0 + nrm(ks[19], (DEPTH, D_INNER), 0.02),
        "w_out": nrm(ks[20], (DEPTH, D_MODEL, D_MODEL), D_MODEL ** -0.5),
        "norm2_w": 1.0 + nrm(ks[21], (DEPTH, D_MODEL), 0.02),
        "w_up": nrm(ks[22], (DEPTH, D_MODEL, D_FF), D_MODEL ** -0.5),
        "w_down": nrm(ks[23], (DEPTH, D_FF, D_MODEL), D_FF ** -0.5),
        "final_norm_w": 1.0 + nrm(ks[24], (D_MODEL,), 0.02),
    }


def reference(x_prompt, x_sample, cache_k, cache_v, cache_idx_k, state_conv, state_ssm, page_table,
              c_prompt, c_sample, w_ada, b_ada, norm1_w, w_in, conv_w, conv_b, dt_bias, a_log, d_skip,
              ssm_norm_w, w_out, norm2_w, w_up, w_down, final_norm_w):
    B, T = x_prompt.shape[0], x_prompt.shape[1]
    TS = x_sample.shape[1]
    past = page_table.shape[1] * cache_k.shape[2]
    pos_p = jnp.arange(T)
    pos_s = past + jnp.arange(TS)
    hp, hs = x_prompt, x_sample
    kp, vp, ikp, cvp, ssp = [], [], [], [], []
    ksm, vsm, iks, cvs, sss = [], [], [], [], []
    for l in range(DEPTH):
        lw = (w_ada[l], b_ada[l], norm1_w[l], w_in[l], conv_w[l], conv_b[l], dt_bias[l], a_log[l],
              d_skip[l], ssm_norm_w[l], w_out[l], norm2_w[l], w_up[l], w_down[l])
        conv0 = jnp.zeros((B, CONV_W - 1, CONV_DIM), x_prompt.dtype)
        ssm0 = jnp.zeros((B, SSM_HEADS, SSM_HEAD_DIM, D_STATE), jnp.float32)
        hp, (k1, v1, ik1, cv1, ss1) = hybrid_layer(hp, c_prompt, pos_p, None, conv0, ssm0, *lw)
        past_kv = (gather_pages(cache_k[l], page_table), gather_pages(cache_v[l], page_table),
                   gather_pages(cache_idx_k[l], page_table))
        hs, (k2, v2, ik2, cv2, ss2) = hybrid_layer(hs, c_sample, pos_s, past_kv, state_conv[l], state_ssm[l], *lw)
        kp.append(k1); vp.append(v1); ikp.append(ik1); cvp.append(cv1); ssp.append(ss1)
        ksm.append(k2); vsm.append(v2); iks.append(ik2); cvs.append(cv2); sss.append(ss2)
    y_prompt = rmsnorm(hp, final_norm_w)
    y_sample = rmsnorm(hs, final_norm_w)
    return (y_prompt, y_sample,
            jnp.stack(kp), jnp.stack(vp), jnp.stack(ikp), jnp.stack(cvp), jnp.stack(ssp),
            jnp.stack(ksm), jnp.stack(vsm), jnp.stack(iks), jnp.stack(cvs), jnp.stack(sss))
```

```python
import functools

import jax
import jax.numpy as jnp
import numpy as np
from jax import lax
from jax.experimental import pallas as pl
from jax.experimental.pallas import tpu as pltpu

F32 = jnp.float32
BF16 = jnp.bfloat16
I32 = jnp.int32

HEAD_DIM = 64
N_KV_HEADS = 4
ROT_HALF = 8
ROPE_THETA = 500000.0
IDX_HEADS = 8
IDX_DIM = 64
TOPK_MAX = 256
SSM_HEAD_DIM = 64
SSM_GROUPS = 4
D_STATE = 128
CONV_W = 4
SSD_CHUNK = 128
EPS = 1e-6

LANE = 128
SUBLANE = 8
VMEM_LIMIT = 56 * 1024 * 1024
IDX_PAGES_PER_STEP = 32
KV_PAGES_PER_STEP = 16
COUNT_STRIP = 64

IW_OFF = IDX_DIM
DT_OFF = IDX_DIM + IDX_HEADS
INT_MIN = -(2 ** 31)
NEG = -1e30
NEG_INF = float("-inf")
F32_LOWEST = float(np.finfo(np.float32).min)


def _sigmoid(x):
    return 1.0 / (1.0 + jnp.exp(-x))


def _silu(x):
    return x * _sigmoid(x)


def _softplus(x):
    return jnp.maximum(x, 0.0) + jnp.log1p(jnp.exp(-jnp.abs(x)))


def _bdot(a, b):
    return jnp.dot(a.astype(BF16), b.astype(BF16), preferred_element_type=F32)


def _bdot_nt(a, b):
    return lax.dot_general(a.astype(BF16), b.astype(BF16), (((1,), (1,)), ((), ())), preferred_element_type=F32)


def _split3_dot(v, e):
    hi = v.astype(BF16)
    r1 = v - hi.astype(F32)
    mid = r1.astype(BF16)
    lo = (r1 - mid.astype(F32)).astype(BF16)
    d = functools.partial(jnp.dot, preferred_element_type=F32)
    return d(hi, e) + d(mid, e) + d(lo, e)


def _f32_at_rank(u):
    key = u ^ jnp.int32(INT_MIN)
    bits = jnp.where(key < 0, jnp.int32(INT_MIN) - key, key)
    return lax.bitcast_convert_type(bits, F32)


def _kth_largest(count_ge, rows, topk, n_total):
    def cond(carry):
        t, _, _, unresolved = carry
        return jnp.logical_and(t < 32, unresolved > 0)

    def body(carry):
        t, tau_u, cnt_tau, _ = carry
        cand_u = tau_u | jnp.left_shift(jnp.int32(1), 31 - t)
        cnt = count_ge(_f32_at_rank(cand_u))
        take = cnt >= float(topk)
        cnt_tau = jnp.where(take, cnt, cnt_tau)
        unresolved = (jnp.max(cnt_tau) > float(topk)).astype(I32)
        return t + 1, jnp.where(take, cand_u, tau_u), cnt_tau, unresolved

    cnt0 = jnp.zeros((rows, 1), F32) + n_total
    init = (jnp.int32(0), jnp.zeros((rows, 1), I32), cnt0, (jnp.max(cnt0) > float(topk)).astype(I32))
    _, tau_u, cnt_tau, _ = lax.while_loop(cond, body, init)
    return jnp.where(tau_u == 0, NEG_INF, _f32_at_rank(tau_u)), cnt_tau


def _largest_divisor(n, cap):
    return max(k for k in range(1, cap + 1) if n % k == 0)


def _const_spec(shape):
    n = len(shape)
    return pl.BlockSpec(shape, lambda *a: (0,) * n, pipeline_mode=pl.Buffered(1))


def _params(n_axes):
    return pltpu.CompilerParams(dimension_semantics=("arbitrary",) * n_axes, vmem_limit_bytes=VMEM_LIMIT)


def _ada_kernel(c_ref, whi_ref, wlo_ref, b_ref, o_ref):
    s = _silu(c_ref[...])
    s_hi = s.astype(BF16)
    s_lo = (s - s_hi.astype(F32)).astype(BF16)
    d = functools.partial(jnp.dot, preferred_element_type=F32)
    o_ref[...] = d(s_hi, whi_ref[...]) + d(s_lo, whi_ref[...]) + d(s_hi, wlo_ref[...]) + b_ref[...]


def _ada(c_all, w_hi, w_lo, b_ada, tn=1024):
    r, d = c_all.shape
    n = w_hi.shape[1]
    return pl.pallas_call(
        _ada_kernel,
        grid=(n // tn,),
        in_specs=[
            pl.BlockSpec((r, d), lambda j: (0, 0)),
            pl.BlockSpec((d, tn), lambda j: (0, j)),
            pl.BlockSpec((d, tn), lambda j: (0, j)),
            pl.BlockSpec((1, tn), lambda j: (0, j)),
        ],
        out_specs=pl.BlockSpec((r, tn), lambda j: (0, j)),
        out_shape=jax.ShapeDtypeStruct((r, n), F32),
        compiler_params=_params(1),
        name="ada",
    )(c_all, w_hi, w_lo, b_ada)


def _rope_chunk(c, cos, s1, s2):
    return c * cos + pltpu.roll(c, LANE - ROT_HALF, 1) * s1 + pltpu.roll(c, ROT_HALF, 1) * s2


def _inproj_kernel(x_ref, sc_ref, sh_ref, nw_ref, w_ref, cos_ref, s1_ref, s2_ref, *outs, d_model, d_inner,
                   conv_dim, prompt):
    x = x_ref[...]
    h = x * lax.rsqrt(jnp.mean(x * x, axis=-1, keepdims=True) + EPS) * nw_ref[...]
    hb = (h * (1.0 + sc_ref[0]) + sh_ref[0]).astype(BF16)
    cos, s1, s2 = cos_ref[...], s1_ref[...], s2_ref[...]
    kv_w = N_KV_HEADS * HEAD_DIM
    iq_w = IDX_HEADS * IDX_DIM
    q0 = 0
    k0 = q0 + d_model
    v0 = k0 + kv_w
    iq0 = v0 + kv_w
    m0 = iq0 + iq_w
    z0 = m0 + LANE
    x0 = z0 + d_inner
    ga0 = x0 + conv_dim
    gs0 = ga0 + d_model

    def proj(lo, width):
        return jnp.dot(hb, w_ref[:, lo:lo + width], preferred_element_type=F32)

    if prompt:
        (q_ref, kt32_ref, kt_ref, vt32_ref, ve_ref, iq_ref, ikt32_ref, ikt_ref, misc_ref, z_ref, xbc_ref, ga_ref,
         gs_ref) = outs
    else:
        q_ref, k_ref, v_ref, iq_ref, misc_ref, z_ref, xbc_ref, ga_ref, gs_ref = outs

    q = proj(q0, d_model)
    for c in range(d_model // LANE):
        qc = _rope_chunk(q[:, c * LANE:(c + 1) * LANE], cos, s1, s2) * (HEAD_DIM ** -0.5)
        if prompt:
            q_ref[0, 2 * c] = qc[:, :HEAD_DIM].astype(BF16)
            q_ref[0, 2 * c + 1] = qc[:, HEAD_DIM:].astype(BF16)
        else:
            q_ref[:, c * LANE:(c + 1) * LANE] = qc

    kk = proj(k0, kv_w)
    for c in range(kv_w // LANE):
        kc = _rope_chunk(kk[:, c * LANE:(c + 1) * LANE], cos, s1, s2)
        if prompt:
            kc_t = kc.T
            kt32_ref[0, c * LANE:(c + 1) * LANE, :] = kc_t
            kt_ref[0, c * LANE:(c + 1) * LANE, :] = kc_t.astype(BF16)
        else:
            k_ref[:, c * LANE:(c + 1) * LANE] = kc

    vv = proj(v0, kv_w)
    if not prompt:
        v_ref[...] = vv
    else:
        for c in range(kv_w // LANE):
            vt32_ref[0, c * LANE:(c + 1) * LANE, :] = vv[:, c * LANE:(c + 1) * LANE].T
        lane = lax.broadcasted_iota(I32, (vv.shape[0], LANE), 1)
        for g in range(N_KV_HEADS):
            vc = vv[:, (g // 2) * LANE:(g // 2 + 1) * LANE]
            if g % 2 == 1:
                vc = pltpu.roll(vc, HEAD_DIM, 1)
            ve = jnp.where(lane < HEAD_DIM, vc, jnp.where(lane == HEAD_DIM, 1.0, 0.0))
            ve_ref[0, g] = ve.astype(BF16)

    iq = proj(iq0, iq_w)
    for c in range(iq_w // LANE):
        ic = _rope_chunk(iq[:, c * LANE:(c + 1) * LANE], cos, s1, s2)
        if prompt:
            iq_ref[0, 2 * c] = ic[:, :IDX_DIM].astype(BF16)
            iq_ref[0, 2 * c + 1] = ic[:, IDX_DIM:].astype(BF16)
        else:
            iq_ref[:, c * LANE:(c + 1) * LANE] = ic.astype(BF16)

    mm = proj(m0, LANE)
    lane = lax.broadcasted_iota(I32, mm.shape, 1)
    mm = jnp.where(lane < IDX_DIM, _rope_chunk(mm, cos, s1, s2), mm)
    misc_ref[...] = mm
    if prompt:
        ik_t = mm.T[:IDX_DIM, :]
        ikt32_ref[0] = ik_t
        ikt_ref[0] = ik_t.astype(BF16)

    z_ref[...] = proj(z0, d_inner)
    xbc_ref[...] = proj(x0, conv_dim)
    ga_ref[...] = proj(ga0, d_model)
    gs_ref[...] = proj(gs0, d_model)


def _inproj(x2d, sc, sh, nw, w_all, cos, s1, s2, *, n_batch, seq, tm, prompt, d_inner, conv_dim):
    rows, d = x2d.shape
    tb = seq // tm
    r_mod = sc.shape[1]
    kv_w = N_KV_HEADS * HEAD_DIM
    iq_w = IDX_HEADS * IDX_DIM
    n_heads = d // HEAD_DIM
    if cos.shape[0] == 1:
        tab = pl.BlockSpec((1, LANE), lambda i: (0, 0))
    else:
        tab = pl.BlockSpec((tm, LANE), lambda i: (i % tb, 0))
    mod = pl.BlockSpec((1, r_mod, d), lambda i: (i // tb, 0, 0))
    in_specs = [pl.BlockSpec((tm, d), lambda i: (i, 0)), mod, mod, _const_spec((1, d)), _const_spec(w_all.shape),
                tab, tab, tab]

    def rowspec(w):
        return pl.BlockSpec((tm, w), lambda i: (i, 0))

    def sds(shape, dt):
        return jax.ShapeDtypeStruct(shape, dt)

    if prompt:
        kv_t = pl.BlockSpec((1, kv_w, tm), lambda i: (i // tb, 0, i % tb))
        ik_t = pl.BlockSpec((1, IDX_DIM, tm), lambda i: (i // tb, 0, i % tb))
        out_specs = [
            pl.BlockSpec((1, n_heads, tm, HEAD_DIM), lambda i: (i // tb, 0, i % tb, 0)),
            kv_t, kv_t, kv_t,
            pl.BlockSpec((1, N_KV_HEADS, tm, LANE), lambda i: (i // tb, 0, i % tb, 0)),
            pl.BlockSpec((1, IDX_HEADS, tm, IDX_DIM), lambda i: (i // tb, 0, i % tb, 0)),
            ik_t, ik_t,
            rowspec(LANE), rowspec(d_inner), rowspec(conv_dim), rowspec(d), rowspec(d),
        ]
        out_shape = [
            sds((n_batch, n_heads, seq, HEAD_DIM), BF16), sds((n_batch, kv_w, seq), F32),
            sds((n_batch, kv_w, seq), BF16), sds((n_batch, kv_w, seq), F32),
            sds((n_batch, N_KV_HEADS, seq, LANE), BF16), sds((n_batch, IDX_HEADS, seq, IDX_DIM), BF16),
            sds((n_batch, IDX_DIM, seq), F32), sds((n_batch, IDX_DIM, seq), BF16),
            sds((rows, LANE), F32), sds((rows, d_inner), F32), sds((rows, conv_dim), F32), sds((rows, d), F32),
            sds((rows, d), F32),
        ]
    else:
        out_specs = [rowspec(d), rowspec(kv_w), rowspec(kv_w), rowspec(iq_w), rowspec(LANE), rowspec(d_inner),
                     rowspec(conv_dim), rowspec(d), rowspec(d)]
        out_shape = [sds((rows, d), F32), sds((rows, kv_w), F32), sds((rows, kv_w), F32), sds((rows, iq_w), BF16),
                     sds((rows, LANE), F32), sds((rows, d_inner), F32), sds((rows, conv_dim), F32),
                     sds((rows, d), F32), sds((rows, d), F32)]
    return pl.pallas_call(
        functools.partial(_inproj_kernel, d_model=d, d_inner=d_inner, conv_dim=conv_dim, prompt=prompt),
        grid=(rows // tm,),
        in_specs=in_specs,
        out_specs=out_specs,
        out_shape=out_shape,
        compiler_params=_params(1),
        name="inproj_prompt" if prompt else "inproj_sample",
    )(x2d, sc, sh, nw, w_all, cos, s1, s2)


def _attn_kernel(q_ref, iq_ref, misc_ref, kt_ref, ve_ref, ikt_ref, o_ref, sc_scr, cand_scr, m_scr, acc_scr, *, tq,
                 topk):
    i = pl.program_id(1)
    nblk = i + 1
    tk = tq
    hpg = q_ref.shape[1] // N_KV_HEADS
    cw = misc_ref[:, IW_OFF:IW_OFF + IDX_HEADS] * (IDX_HEADS ** -0.5) * (IDX_DIM ** -0.5)
    row = lax.broadcasted_iota(I32, (tq, tk), 0)
    col = lax.broadcasted_iota(I32, (tq, tk), 1)

    def phase_a(j, carry):
        off = pl.multiple_of(j * tk, tk)
        ikb = ikt_ref[0, :, pl.ds(off, tk)]
        acc = jnp.zeros((tq, tk), F32)
        for h in range(IDX_HEADS):
            x = jnp.dot(iq_ref[0, h], ikb, preferred_element_type=F32)
            acc = acc + jnp.maximum(x, 0.0) * cw[:, h:h + 1]
        ok = (off + col) <= (i * tq + row)
        sc_scr[:, pl.ds(off, tk)] = jnp.where(ok, acc, NEG_INF)
        return carry

    lax.fori_loop(0, nblk, phase_a, 0)

    def count(cmp, thr):
        cand_scr[...] = jnp.broadcast_to(thr, (tq, LANE))

        def body(j, accs):
            off = pl.multiple_of(j * tk, tk)
            out = []
            for k, acc in enumerate(accs):
                rows = pl.ds(k * COUNT_STRIP, COUNT_STRIP)
                cb = cand_scr[rows, :]
                for c in range(tk // LANE):
                    acc = acc + jnp.where(cmp(sc_scr[rows, pl.ds(off + c * LANE, LANE)], cb), 1.0, 0.0)
                out.append(acc)
            return tuple(out)

        zero = jnp.zeros((COUNT_STRIP, LANE), F32)
        accs = lax.fori_loop(0, nblk, body, (zero,) * (tq // COUNT_STRIP))
        return jnp.sum(jnp.concatenate(accs, axis=0), axis=1, keepdims=True)

    tau, cnt_tau = _kth_largest(lambda thr: count(lax.ge, thr), tq, topk, (nblk * tk).astype(F32))

    any_tie = jnp.max(jnp.where(cnt_tau > float(topk), 1.0, 0.0)) > 0.5

    @pl.when(any_tie)
    def _():
        need = float(topk) - count(lax.gt, tau)
        upper = jnp.where(row <= col, 1.0, 0.0).astype(BF16)

        def body(j, seen):
            off = pl.multiple_of(j * tk, tk)
            sb = sc_scr[:, pl.ds(off, tk)]
            eq = sb == tau
            prefix = jnp.dot(jnp.where(eq, 1.0, 0.0).astype(BF16), upper, preferred_element_type=F32)
            late = jnp.where(seen + prefix > need, NEG_INF, sb)
            sc_scr[:, pl.ds(off, tk)] = jnp.where(eq, late, sb)
            return seen + prefix[:, tk - 1:tk]

        lax.fori_loop(0, nblk, body, jnp.zeros((tq, 1), F32))

    tau_c = jnp.maximum(tau, F32_LOWEST)
    m_scr[...] = jnp.full(m_scr.shape, NEG, F32)
    acc_scr[...] = jnp.zeros(acc_scr.shape, F32)

    def phase_c(j, carry):
        off = pl.multiple_of(j * tk, tk)
        bias = jnp.where(sc_scr[:, pl.ds(off, tk)] >= tau_c, 0.0, NEG)
        for g in range(N_KV_HEADS):
            qg = q_ref[0, g * hpg:(g + 1) * hpg].reshape(hpg * tq, HEAD_DIM)
            s = jnp.dot(qg, kt_ref[0, g * HEAD_DIM:(g + 1) * HEAD_DIM, pl.ds(off, tk)],
                        preferred_element_type=F32)
            s = s.reshape(hpg, tq, tk) + bias[None]
            m_prev = m_scr[g]
            m_new = jnp.maximum(m_prev, jnp.max(s, axis=-1, keepdims=True))
            alpha = jnp.exp(m_prev - m_new)
            p = jnp.exp(s - jnp.concatenate([m_new] * (tk // LANE), axis=-1))
            m_scr[g] = m_new
            pv = jnp.dot(p.reshape(hpg * tq, tk).astype(BF16), ve_ref[0, g, pl.ds(off, tk), :],
                         preferred_element_type=F32)
            acc_scr[g] = alpha * acc_scr[g] + pv.reshape(hpg, tq, LANE)
        return carry

    lax.fori_loop(0, nblk, phase_c, 0)

    for g in range(N_KV_HEADS):
        for hh in range(hpg):
            a = acc_scr[g, hh]
            hq = g * hpg + hh
            o_ref[:, hq * HEAD_DIM:(hq + 1) * HEAD_DIM] = a[:, :HEAD_DIM] / a[:, HEAD_DIM:HEAD_DIM + 1]


def _prompt_attention(q_hm, iq_hm, misc, kt, ve, ikt, *, tq, topk):
    nb, nh, seq, _ = q_hm.shape
    hpg = nh // N_KV_HEADS
    tb = seq // tq
    kv_w = N_KV_HEADS * HEAD_DIM
    return pl.pallas_call(
        functools.partial(_attn_kernel, tq=tq, topk=topk),
        grid=(nb, tb),
        in_specs=[
            pl.BlockSpec((1, nh, tq, HEAD_DIM), lambda b, i: (b, 0, i, 0)),
            pl.BlockSpec((1, IDX_HEADS, tq, IDX_DIM), lambda b, i: (b, 0, i, 0)),
            pl.BlockSpec((tq, LANE), lambda b, i: (b * tb + i, 0)),
            pl.BlockSpec((1, kv_w, seq), lambda b, i: (b, 0, 0)),
            pl.BlockSpec((1, N_KV_HEADS, seq, LANE), lambda b, i: (b, 0, 0, 0)),
            pl.BlockSpec((1, IDX_DIM, seq), lambda b, i: (b, 0, 0)),
        ],
        out_specs=pl.BlockSpec((tq, nh * HEAD_DIM), lambda b, i: (b * tb + i, 0)),
        out_shape=jax.ShapeDtypeStruct((nb * seq, nh * HEAD_DIM), F32),
        scratch_shapes=[
            pltpu.VMEM((tq, seq), F32),
            pltpu.VMEM((tq, LANE), F32),
            pltpu.VMEM((N_KV_HEADS, hpg, tq, LANE), F32),
            pltpu.VMEM((N_KV_HEADS, hpg, tq, LANE), F32),
        ],
        compiler_params=_params(2),
        name="prompt_attention",
    )(q_hm, iq_hm, misc, kt, ve, ikt)


def _gated_group_norm(y, z, nw, d_inner):
    g = y * _silu(z)
    gw = d_inner // SSM_GROUPS
    outs = []
    for k in range(SSM_GROUPS):
        gg = g[:, k * gw:(k + 1) * gw]
        outs.append(gg * lax.rsqrt(jnp.mean(gg * gg, axis=-1, keepdims=True) + EPS))
    return jnp.concatenate(outs, axis=-1) * nw


def _ssd_kernel(xbc_ref, z_ref, misc_ref, cw_ref, cb_ref, dtb_ref, alog_ref, e_ref, dsk_ref, nw_ref, y_ref, st_ref,
                ext_scr, st_scr, y_scr, *, d_inner, n_heads):
    c = pl.program_id(1)
    q = xbc_ref.shape[0]
    gn = SSM_GROUPS * D_STATE
    hpg = n_heads // SSM_GROUPS
    gw = hpg * SSM_HEAD_DIM

    @pl.when(c == 0)
    def _():
        ext_scr[0:SUBLANE, :] = jnp.zeros((SUBLANE, ext_scr.shape[1]), F32)
        st_scr[...] = jnp.zeros(st_scr.shape, F32)

    u = xbc_ref[...]
    ext_scr[SUBLANE:, :] = u
    conv = cb_ref[...] + cw_ref[CONV_W - 1:CONV_W, :] * u
    for w in range(CONV_W - 1):
        lo = SUBLANE - (CONV_W - 1) + w
        conv = conv + cw_ref[w:w + 1, :] * ext_scr[lo:lo + q, :]
    ext_scr[0:SUBLANE, :] = u[q - SUBLANE:, :]
    act = _silu(conv)
    xs = act[:, :d_inner]
    bm = act[:, d_inner:d_inner + gn]
    cm = act[:, d_inner + gn:]

    dt_t = _softplus(misc_ref[...].T + dtb_ref[...])
    da_t = dt_t * (-jnp.exp(alog_ref[...]))
    ri = lax.broadcasted_iota(I32, (q, q), 0)
    ci = lax.broadcasted_iota(I32, (q, q), 1)
    cs_t = _split3_dot(da_t, jnp.where(ri <= ci, 1.0, 0.0).astype(BF16))
    cs = cs_t.T
    dt = dt_t.T
    e = e_ref[...]
    ecs_x = _split3_dot(jnp.exp(cs), e)
    wst_x = _split3_dot(dt * jnp.exp(cs[q - 1:q, :] - cs), e)
    tri = ri >= ci

    for g in range(SSM_GROUPS):
        bg = bm[:, g * D_STATE:(g + 1) * D_STATE]
        cg = cm[:, g * D_STATE:(g + 1) * D_STATE].astype(BF16)
        bg_t = bg.T.astype(BF16)
        cb = jnp.dot(cg, bg_t, preferred_element_type=F32)
        s_t = st_scr[g]
        y_off = jnp.dot(cg, s_t.astype(BF16), preferred_element_type=F32)
        xg = xs[:, g * gw:(g + 1) * gw]
        y_g = y_off * ecs_x[:, g * gw:(g + 1) * gw]
        pieces = []
        for hh in range(hpg):
            idx = DT_OFF + g * hpg + hh
            seg = cs[:, idx:idx + 1] - cs_t[idx:idx + 1, :]
            mm = cb * jnp.exp(jnp.where(tri, seg, NEG)) * dt_t[idx:idx + 1, :]
            pieces.append(_bdot(mm, xg[:, hh * SSM_HEAD_DIM:(hh + 1) * SSM_HEAD_DIM]))
        y_scr[:, g * gw:(g + 1) * gw] = y_g + jnp.concatenate(pieces, axis=-1)
        wg = (xg * wst_x[:, g * gw:(g + 1) * gw]).astype(BF16)
        st_scr[g] = s_t * ecs_x[q - 1:q, g * gw:(g + 1) * gw] + jnp.dot(bg_t, wg, preferred_element_type=F32)

    y = y_scr[...] + dsk_ref[...] * xs
    y_ref[...] = _gated_group_norm(y, z_ref[...], nw_ref[...], d_inner)

    @pl.when(c == pl.num_programs(1) - 1)
    def _():
        for g in range(SSM_GROUPS):
            st_ref[0, g * hpg:(g + 1) * hpg] = st_scr[g].T.reshape(hpg, SSM_HEAD_DIM, D_STATE)


def _prompt_ssd(xbc, z, misc, conv_w, conv_b, dtb_col, alog_col, e_mat, dsk_row, nw_row, *, n_batch, seq, d_inner,
                n_heads):
    q = SSD_CHUNK
    nc = seq // q
    cd = xbc.shape[1]
    hpg = n_heads // SSM_GROUPS
    return pl.pallas_call(
        functools.partial(_ssd_kernel, d_inner=d_inner, n_heads=n_heads),
        grid=(n_batch, nc),
        in_specs=[
            pl.BlockSpec((q, cd), lambda b, c: (b * nc + c, 0)),
            pl.BlockSpec((q, d_inner), lambda b, c: (b * nc + c, 0)),
            pl.BlockSpec((q, LANE), lambda b, c: (b * nc + c, 0)),
            _const_spec(conv_w.shape), _const_spec(conv_b.shape), _const_spec(dtb_col.shape),
            _const_spec(alog_col.shape), _const_spec(e_mat.shape), _const_spec(dsk_row.shape),
            _const_spec(nw_row.shape),
        ],
        out_specs=[
            pl.BlockSpec((q, d_inner), lambda b, c: (b * nc + c, 0)),
            pl.BlockSpec((1, n_heads, SSM_HEAD_DIM, D_STATE), lambda b, c: (b, 0, 0, 0)),
        ],
        out_shape=[
            jax.ShapeDtypeStruct((n_batch * seq, d_inner), F32),
            jax.ShapeDtypeStruct((n_batch, n_heads, SSM_HEAD_DIM, D_STATE), F32),
        ],
        scratch_shapes=[
            pltpu.VMEM((q + SUBLANE, cd), F32),
            pltpu.VMEM((SSM_GROUPS, D_STATE, hpg * SSM_HEAD_DIM), F32),
            pltpu.VMEM((q, d_inner), F32),
        ],
        compiler_params=_params(2),
        name="prompt_ssd",
    )(xbc, z, misc, conv_w, conv_b, dtb_col, alog_col, e_mat, dsk_row, nw_row)


def _mlp_kernel(x_ref, att_ref, y_ref, ga_ref, gs_ref, gt1_ref, sc2_ref, sh2_ref, gt2_ref, n2_ref, fn_ref, wo_ref,
                wu_ref, wd_ref, o_ref, *, ff_chunk):
    merged = _sigmoid(ga_ref[...]) * att_ref[...] + _sigmoid(gs_ref[...]) * y_ref[...]
    x1 = x_ref[...] + gt1_ref[0] * jnp.dot(merged.astype(BF16), wo_ref[...], preferred_element_type=F32)
    h2 = x1 * lax.rsqrt(jnp.mean(x1 * x1, axis=-1, keepdims=True) + EPS) * n2_ref[...]
    hb = (h2 * (1.0 + sc2_ref[0]) + sh2_ref[0]).astype(BF16)
    acc = jnp.zeros(x1.shape, F32)
    for c in range(wu_ref.shape[1] // ff_chunk):
        u = jnp.maximum(jnp.dot(hb, wu_ref[:, c * ff_chunk:(c + 1) * ff_chunk], preferred_element_type=F32), 0.0)
        acc = acc + jnp.dot((u * u).astype(BF16), wd_ref[c * ff_chunk:(c + 1) * ff_chunk, :],
                            preferred_element_type=F32)
    x2 = x1 + gt2_ref[0] * acc
    o_ref[...] = x2 * lax.rsqrt(jnp.mean(x2 * x2, axis=-1, keepdims=True) + EPS) * fn_ref[...]


def _mlp(x2d, att, y, ga, gs, gt1, sc2, sh2, gt2, n2, fn, wo, wu, wd, *, seq, tm, ff_chunk=1024):
    rows, d = x2d.shape
    tb = seq // tm
    r_mod = gt1.shape[1]
    row = pl.BlockSpec((tm, d), lambda i: (i, 0))
    mod = pl.BlockSpec((1, r_mod, d), lambda i: (i // tb, 0, 0))
    return pl.pallas_call(
        functools.partial(_mlp_kernel, ff_chunk=ff_chunk),
        grid=(rows // tm,),
        in_specs=[row, row, row, row, row, mod, mod, mod, mod, _const_spec((1, d)), _const_spec((1, d)),
                  _const_spec(wo.shape), _const_spec(wu.shape), _const_spec(wd.shape)],
        out_specs=row,
        out_shape=jax.ShapeDtypeStruct((rows, d), F32),
        compiler_params=_params(1),
        name="merge_mlp",
    )(x2d, att, y, ga, gs, gt1, sc2, sh2, gt2, n2, fn, wo, wu, wd)


def _sidx_kernel(pt_ref, iq_ref, cw_ref, iknew_ref, *rest, pages_per_step, n_steps):
    del pt_ref
    page_refs, o_ref = rest[:pages_per_step], rest[pages_per_step]
    r = pl.program_id(1)
    iq = iq_ref[0]
    cw = cw_ref[0]

    def score(ik_t):
        x = _bdot(iq, ik_t)
        return jnp.sum(jnp.maximum(x, 0.0) * cw, axis=0, keepdims=True)

    @pl.when(r < n_steps)
    def _():
        for m in range(pages_per_step):
            o_ref[0, :, m * LANE:(m + 1) * LANE] = score(page_refs[m][...])

    @pl.when(r == n_steps)
    def _():
        s_new = score(iknew_ref[0])[:, 0:1]
        lane = lax.broadcasted_iota(I32, (1, o_ref.shape[2]), 1)
        o_ref[0] = jnp.where(lane == 0, jnp.broadcast_to(s_new, lane.shape), NEG_INF)


def _sample_index_scores(page_table, iq3, cw3, iknew_t, idx_cache_t, *, pages_per_step):
    db, n_pages = page_table.shape
    page = idx_cache_t.shape[3]
    n_steps = n_pages // pages_per_step
    width = pages_per_step * page

    def page_spec(m):
        return pl.BlockSpec(
            (None, None, IDX_DIM, page),
            lambda b, r, pt: (0, pt[b, jnp.minimum(r, n_steps - 1) * pages_per_step + m], 0, 0))

    grid_spec = pltpu.PrefetchScalarGridSpec(
        num_scalar_prefetch=1,
        grid=(db, n_steps + 1),
        in_specs=[
            pl.BlockSpec((1, IDX_HEADS, IDX_DIM), lambda b, r, pt: (b, 0, 0)),
            pl.BlockSpec((1, IDX_HEADS, 1), lambda b, r, pt: (b, 0, 0)),
            pl.BlockSpec((1, IDX_DIM, page), lambda b, r, pt: (b, 0, 0)),
        ] + [page_spec(m) for m in range(pages_per_step)],
        out_specs=pl.BlockSpec((1, 1, width), lambda b, r, pt: (b, 0, r)),
    )
    return pl.pallas_call(
        functools.partial(_sidx_kernel, pages_per_step=pages_per_step, n_steps=n_steps),
        grid_spec=grid_spec,
        out_shape=jax.ShapeDtypeStruct((db, 1, (n_steps + 1) * width), F32),
        compiler_params=_params(2),
        name="sample_index_scores",
    )(page_table, iq3, cw3, iknew_t, *([idx_cache_t] * pages_per_step))


def _ssel_kernel(sc_ref, bias_ref, *, topk):
    sc = sc_ref[...]
    db, lk = sc.shape

    def count_ge(thr):
        return jnp.sum(jnp.where(sc >= thr, 1.0, 0.0), axis=1, keepdims=True)

    tau, _ = _kth_largest(count_ge, db, topk, jnp.float32(lk))
    need = float(topk) - jnp.sum(jnp.where(sc > tau, 1.0, 0.0), axis=1, keepdims=True)
    ri = lax.broadcasted_iota(I32, (LANE, LANE), 0)
    ci = lax.broadcasted_iota(I32, (LANE, LANE), 1)
    upper = jnp.where(ri <= ci, 1.0, 0.0).astype(BF16)

    def body(j, seen):
        off = pl.multiple_of(j * LANE, LANE)
        sb = sc_ref[:, pl.ds(off, LANE)]
        eq = sb == tau
        prefix = jnp.dot(jnp.where(eq, 1.0, 0.0).astype(BF16), upper, preferred_element_type=F32)
        keep_eq = jnp.where(seen + prefix <= need, 0.0, NEG)
        sel = jnp.where(sb > tau, 0.0, jnp.where(eq, keep_eq, NEG))
        bias_ref[:, pl.ds(off, LANE)] = jnp.where(sb == NEG_INF, NEG, sel)
        return seen + prefix[:, LANE - 1:LANE]

    lax.fori_loop(0, lk // LANE, body, jnp.zeros((db, 1), F32))


def _sample_select(keys2d, *, topk):
    return pl.pallas_call(
        functools.partial(_ssel_kernel, topk=topk),
        out_shape=jax.ShapeDtypeStruct(keys2d.shape, F32),
        compiler_params=pltpu.CompilerParams(vmem_limit_bytes=VMEM_LIMIT),
        name="sample_select",
    )(keys2d)


def _sattn_kernel(pt_ref, q_ref, bias_ref, knew_ref, vnew_ref, *rest, pages_per_step, n_steps):
    del pt_ref
    k_refs = rest[:pages_per_step]
    v_refs = rest[pages_per_step:2 * pages_per_step]
    o_ref, m_scr, l_scr, acc_scr = rest[2 * pages_per_step:]
    r = pl.program_id(1)
    q = q_ref[0]
    bias = bias_ref[0]

    @pl.when(r == 0)
    def _():
        m_scr[...] = jnp.full(m_scr.shape, NEG, F32)
        l_scr[...] = jnp.zeros(l_scr.shape, F32)
        acc_scr[...] = jnp.zeros(acc_scr.shape, F32)

    def update(s, pv_fn):
        m_prev = m_scr[...]
        m_new = jnp.maximum(m_prev, jnp.max(s, axis=1, keepdims=True))
        alpha = jnp.exp(m_prev - m_new)
        p = jnp.exp(s - m_new[:, 0:1])
        m_scr[...] = m_new
        l_scr[...] = alpha * l_scr[...] + jnp.sum(p, axis=1, keepdims=True)
        acc_scr[...] = alpha[:, 0:1] * acc_scr[...] + pv_fn(p)

    @pl.when(r < n_steps)
    def _():
        s = jnp.concatenate([_bdot(q, k_refs[m][...]) for m in range(pages_per_step)], axis=1) + bias

        def pv(p):
            out = _bdot_nt(p[:, 0:LANE], v_refs[0][...])
            for m in range(1, pages_per_step):
                out = out + _bdot_nt(p[:, m * LANE:(m + 1) * LANE], v_refs[m][...])
            return out

        update(s, pv)

    @pl.when(r == n_steps)
    def _():
        s = _bdot(q, knew_ref[0]) + bias[:, 0:LANE]
        update(s, lambda p: _bdot_nt(p, vnew_ref[0]))
        o_ref[0] = acc_scr[...] / l_scr[:, 0:1]


def _sample_attention(page_table, qmat, bias3, knew_t, vnew_t, cache_kt, cache_vt, *, pages_per_step):
    db, n_pages = page_table.shape
    kv_w, page = cache_kt.shape[2], cache_kt.shape[3]
    nh = qmat.shape[1]
    n_steps = n_pages // pages_per_step
    width = pages_per_step * page

    def page_spec(m):
        return pl.BlockSpec(
            (None, None, kv_w, page),
            lambda b, r, pt: (0, pt[b, jnp.minimum(r, n_steps - 1) * pages_per_step + m], 0, 0))

    pages = [page_spec(m) for m in range(pages_per_step)]
    grid_spec = pltpu.PrefetchScalarGridSpec(
        num_scalar_prefetch=1,
        grid=(db, n_steps + 1),
        in_specs=[
            pl.BlockSpec((1, nh, kv_w), lambda b, r, pt: (b, 0, 0)),
            pl.BlockSpec((1, 1, width), lambda b, r, pt: (b, 0, r)),
            pl.BlockSpec((1, kv_w, page), lambda b, r, pt: (b, 0, 0)),
            pl.BlockSpec((1, kv_w, page), lambda b, r, pt: (b, 0, 0)),
        ] + pages + pages,
        out_specs=pl.BlockSpec((1, nh, kv_w), lambda b, r, pt: (b, 0, 0)),
        scratch_shapes=[pltpu.VMEM((nh, LANE), F32), pltpu.VMEM((nh, LANE), F32), pltpu.VMEM((nh, kv_w), F32)],
    )
    return pl.pallas_call(
        functools.partial(_sattn_kernel, pages_per_step=pages_per_step, n_steps=n_steps),
        grid_spec=grid_spec,
        out_shape=jax.ShapeDtypeStruct((db, nh, kv_w), F32),
        compiler_params=_params(2),
        name="sample_attention",
    )(page_table, qmat, bias3, knew_t, vnew_t, *([cache_kt] * pages_per_step), *([cache_vt] * pages_per_step))


def _sssd_kernel(xbc_ref, prev_ref, misc_ref, z_ref, st_ref, cw_ref, cb_ref, dtb_ref, alog_ref, e_ref, dsk_ref,
                 nw_ref, y_ref, sto_ref, *, d_inner, n_heads):
    gn = SSM_GROUPS * D_STATE
    hpg = n_heads // SSM_GROUPS
    gw = hpg * SSM_HEAD_DIM
    u = xbc_ref[0]
    prev = prev_ref[0]
    conv = cb_ref[...] + cw_ref[CONV_W - 1:CONV_W, :] * u
    for w in range(CONV_W - 1):
        conv = conv + cw_ref[w:w + 1, :] * prev[w:w + 1, :]
    act = _silu(conv)
    xs = act[:, :d_inner]
    bm = act[:, d_inner:d_inner + gn]
    cm = act[:, d_inner + gn:]
    dt = _softplus(misc_ref[0] + dtb_ref[...])
    dec = jnp.exp(dt * (-jnp.exp(alog_ref[...])))
    pad6 = jnp.zeros((SUBLANE - 2, LANE), F32)
    ex = _split3_dot(jnp.concatenate([dt, dec, pad6], axis=0), e_ref[...])
    rows = jnp.concatenate([xs * ex[0:1, :], ex[1:2, :], jnp.zeros((LANE - 2, d_inner), F32)], axis=0)
    cols = rows.T
    pad15 = jnp.zeros((2 * SUBLANE - 1, D_STATE), F32)
    ys = []
    for g in range(SSM_GROUPS):
        s0 = st_ref[0, g * hpg:(g + 1) * hpg].reshape(gw, D_STATE)
        cg = cols[g * gw:(g + 1) * gw, :]
        s1 = s0 * cg[:, 1:2] + cg[:, 0:1] * bm[:, g * D_STATE:(g + 1) * D_STATE]
        sto_ref[0, g * hpg:(g + 1) * hpg] = s1.reshape(hpg, SSM_HEAD_DIM, D_STATE)
        c16 = jnp.concatenate([cm[:, g * D_STATE:(g + 1) * D_STATE], pad15], axis=0)
        ys.append(_bdot_nt(c16, s1)[0:1, :])
    y = jnp.concatenate(ys, axis=-1) + dsk_ref[...] * xs
    y_ref[0] = _gated_group_norm(y, z_ref[0], nw_ref[...], d_inner)


def _sample_ssd(xbc3, prev, misc3, z3, state, conv_w, conv_b, dtb_row, alog_row, e_mat, dsk_row, nw_row, *,
                d_inner, n_heads):
    db = xbc3.shape[0]
    cd = xbc3.shape[2]

    def per(shape):
        n = len(shape)
        return pl.BlockSpec((1,) + tuple(shape[1:]), lambda b: (b,) + (0,) * (n - 1))

    return pl.pallas_call(
        functools.partial(_sssd_kernel, d_inner=d_inner, n_heads=n_heads),
        grid=(db,),
        in_specs=[per(xbc3.shape), per(prev.shape), per(misc3.shape), per(z3.shape), per(state.shape),
                  _const_spec(conv_w.shape), _const_spec(conv_b.shape), _const_spec(dtb_row.shape),
                  _const_spec(alog_row.shape), _const_spec(e_mat.shape), _const_spec(dsk_row.shape),
                  _const_spec(nw_row.shape)],
        out_specs=[per((db, 1, d_inner)), per(state.shape)],
        out_shape=[jax.ShapeDtypeStruct((db, 1, d_inner), F32), jax.ShapeDtypeStruct(state.shape, F32)],
        compiler_params=_params(1),
        name="sample_ssd",
    )(xbc3, prev, misc3, z3, state, conv_w, conv_b, dtb_row, alog_row, e_mat, dsk_row, nw_row)


def _rope_tables(pos):
    inv = ROPE_THETA ** (-jnp.arange(ROT_HALF, dtype=F32) * 2.0 / (2 * ROT_HALF))
    ang = pos.astype(F32)[:, None] * inv[None, :]
    cos, sin = jnp.cos(ang), jnp.sin(ang)
    n = pos.shape[0]
    one = jnp.ones((n, HEAD_DIM - 2 * ROT_HALF), F32)
    zero = jnp.zeros((n, HEAD_DIM - 2 * ROT_HALF), F32)
    z8 = jnp.zeros((n, ROT_HALF), F32)
    c_head = jnp.concatenate([cos, cos, one], axis=1)
    s1_head = jnp.concatenate([-sin, z8, zero], axis=1)
    s2_head = jnp.concatenate([z8, sin, zero], axis=1)
    rep = LANE // HEAD_DIM
    return jnp.tile(c_head, (1, rep)), jnp.tile(s1_head, (1, rep)), jnp.tile(s2_head, (1, rep))


def kernel(x_prompt, x_sample, cache_k, cache_v, cache_idx_k, state_conv, state_ssm, page_table, c_prompt, c_sample,
           w_ada, b_ada, norm1_w, w_in, conv_w, conv_b, dt_bias, a_log, d_skip, ssm_norm_w, w_out, norm2_w, w_up,
           w_down, final_norm_w):
    nb, seq, d = x_prompt.shape
    db, dec_seq, _ = x_sample.shape
    depth = w_in.shape[0]
    assert depth == 1 and dec_seq == 1 and seq % SSD_CHUNK == 0
    n_heads = d // HEAD_DIM
    n_ssm_heads = a_log.shape[1]
    d_inner = n_ssm_heads * SSM_HEAD_DIM
    conv_dim = conv_w.shape[2]
    kv_w = N_KV_HEADS * HEAD_DIM
    iq_w = IDX_HEADS * IDX_DIM
    n_pages, page = page_table.shape[1], cache_k.shape[2]
    past = n_pages * page

    wi = w_in[0]
    offs = np.cumsum([0, d, kv_w, kv_w, iq_w, IDX_DIM, IDX_HEADS, d_inner, conv_dim, n_ssm_heads, d, d])
    seg = [wi[:, offs[k]:offs[k + 1]] for k in range(11)]
    misc_w = jnp.concatenate(
        [seg[4], seg[5], seg[8], jnp.zeros((d, LANE - IDX_DIM - IDX_HEADS - n_ssm_heads), F32)], axis=1)
    w_all = jnp.concatenate([seg[0], seg[1], seg[2], seg[3], misc_w, seg[6], seg[7], seg[9], seg[10]],
                            axis=1).astype(BF16)
    wo, wu, wd = w_out[0].astype(BF16), w_up[0].astype(BF16), w_down[0].astype(BF16)
    nw1, nw2, fnw = norm1_w[0][None, :], norm2_w[0][None, :], final_norm_w[None, :]
    head_lane = jnp.zeros((LANE,), F32)
    dtb_row = head_lane.at[DT_OFF:DT_OFF + n_ssm_heads].set(dt_bias[0])[None, :]
    alog_row = head_lane.at[DT_OFF:DT_OFF + n_ssm_heads].set(a_log[0])[None, :]
    e_mat = jnp.zeros((LANE, d_inner), F32).at[DT_OFF:DT_OFF + n_ssm_heads].set(
        jnp.repeat(jnp.eye(n_ssm_heads, dtype=F32), SSM_HEAD_DIM, axis=1)).astype(BF16)
    dsk_row = jnp.repeat(d_skip[0], SSM_HEAD_DIM)[None, :]
    snw_row = ssm_norm_w[0][None, :]
    cw2, cb2 = conv_w[0], conv_b[0][None, :]

    n_mod = nb + db
    r_mod = -(-n_mod // SUBLANE) * SUBLANE
    c_all = jnp.concatenate([c_prompt, c_sample, jnp.zeros((r_mod - n_mod, d), F32)], axis=0)
    wa_hi = w_ada[0].astype(BF16)
    wa_lo = (w_ada[0] - wa_hi.astype(F32)).astype(BF16)
    mod = _ada(c_all, wa_hi, wa_lo, b_ada[0][None, :])
    sh1, sc1, gt1, sh2, sc2, gt2 = [mod[:, k * d:(k + 1) * d] for k in range(6)]

    def pmod(a):
        return a[:nb].reshape(nb, 1, d)

    def smod(a):
        return a[nb:nb + db].reshape(1, db, d)

    xp = x_prompt.reshape(nb * seq, d)
    cos_p, s1_p, s2_p = _rope_tables(jnp.arange(seq))
    (q_hm, kt32, kt, vt32, ve, iq_hm, ikt32, ikt, misc_p, z_p, xbc_p, ga_p, gs_p) = _inproj(
        xp, pmod(sc1), pmod(sh1), nw1, w_all, cos_p, s1_p, s2_p, n_batch=nb, seq=seq, tm=256, prompt=True,
        d_inner=d_inner, conv_dim=conv_dim)
    topk_p = min(TOPK_MAX, seq // 4)
    att_p = _prompt_attention(q_hm, iq_hm, misc_p, kt, ve, ikt, tq=256, topk=topk_p)
    y_p, ssm_p = _prompt_ssd(xbc_p, z_p, misc_p, cw2, cb2, dtb_row.T, alog_row.T, e_mat, dsk_row, snw_row,
                             n_batch=nb, seq=seq, d_inner=d_inner, n_heads=n_ssm_heads)
    out_p = _mlp(xp, att_p, y_p, ga_p, gs_p, pmod(gt1), pmod(sc2), pmod(sh2), pmod(gt2), nw2, fnw, wo, wu, wd,
                 seq=seq, tm=256)

    xs2 = x_sample.reshape(db, d)
    cos_s, s1_s, s2_s = _rope_tables(jnp.full((1,), past, jnp.int32))
    (q_s, k_s, v_s, iq_s, misc_s, z_s, xbc_s, ga_s, gs_s) = _inproj(
        xs2, smod(sc1), smod(sh1), nw1, w_all, cos_s, s1_s, s2_s, n_batch=1, seq=db, tm=db, prompt=False,
        d_inner=d_inner, conv_dim=conv_dim)
    pps_idx = _largest_divisor(n_pages, IDX_PAGES_PER_STEP)
    pps_kv = _largest_divisor(n_pages, KV_PAGES_PER_STEP)
    cw_s = (misc_s[:, IW_OFF:IW_OFF + IDX_HEADS] * (IDX_HEADS ** -0.5) * (IDX_DIM ** -0.5)).reshape(db, IDX_HEADS, 1)

    def as_page_t(a):
        return jnp.pad(a[:, :, None], ((0, 0), (0, 0), (0, page - 1)))

    idx_cache_t = jnp.transpose(cache_idx_k, (0, 1, 3, 2))
    kv_shape_t = (depth, cache_k.shape[1], kv_w, page)
    cache_kt = jnp.transpose(cache_k, (0, 1, 3, 4, 2)).reshape(kv_shape_t)
    cache_vt = jnp.transpose(cache_v, (0, 1, 3, 4, 2)).reshape(kv_shape_t)

    sc_s = _sample_index_scores(page_table, iq_s.reshape(db, IDX_HEADS, IDX_DIM), cw_s,
                                as_page_t(misc_s[:, :IDX_DIM]), idx_cache_t, pages_per_step=pps_idx)
    topk_s = min(TOPK_MAX, (past + 1) // 4)
    lk = past + pps_kv * page
    assert pps_idx >= pps_kv
    bias_s = _sample_select(sc_s[:, 0, :lk], topk=topk_s).reshape(db, 1, lk)
    hpg = n_heads // N_KV_HEADS
    q5 = q_s.reshape(db, N_KV_HEADS, hpg, 1, HEAD_DIM)
    eye = jnp.eye(N_KV_HEADS, dtype=F32)[None, :, None, :, None]
    qmat = (q5 * eye).reshape(db, n_heads, kv_w).astype(BF16)
    acc_s = _sample_attention(page_table, qmat, bias_s, as_page_t(k_s), as_page_t(v_s), cache_kt, cache_vt,
                              pages_per_step=pps_kv)
    a5 = acc_s.reshape(db, N_KV_HEADS, hpg, N_KV_HEADS, HEAD_DIM)
    att_s = jnp.einsum('bghgd->bghd', a5).reshape(db, d)
    y_s, ssm_s = _sample_ssd(xbc_s.reshape(db, 1, conv_dim), state_conv[0], misc_s.reshape(db, 1, LANE),
                             z_s.reshape(db, 1, d_inner), state_ssm[0], cw2, cb2, dtb_row, alog_row, e_mat, dsk_row,
                             snw_row, d_inner=d_inner, n_heads=n_ssm_heads)
    out_s = _mlp(xs2, att_s, y_s.reshape(db, d_inner), ga_s, gs_s, smod(gt1), smod(sc2), smod(sh2), smod(gt2), nw2,
                 fnw, wo, wu, wd, seq=db, tm=db)

    conv_p = xbc_p.reshape(nb, seq, conv_dim)[:, seq - (CONV_W - 1):, :]
    conv_s = jnp.concatenate([state_conv[0][:, 1:, :], xbc_s[:, None, :]], axis=1)
    def kv_out(a_t):
        return jnp.transpose(a_t.reshape(nb, N_KV_HEADS, HEAD_DIM, seq), (0, 3, 1, 2))[None]

    return (
        out_p.reshape(nb, seq, d),
        out_s.reshape(db, 1, d),
        kv_out(kt32),
        kv_out(vt32),
        jnp.transpose(ikt32, (0, 2, 1))[None],
        conv_p[None],
        ssm_p[None],
        k_s.reshape(1, db, 1, N_KV_HEADS, HEAD_DIM),
        v_s.reshape(1, db, 1, N_KV_HEADS, HEAD_DIM),
        misc_s[:, :IDX_DIM].reshape(1, db, 1, IDX_DIM),
        conv_s[None],
        ssm_s[None],
    )
```

```python
import functools

import jax
import jax.numpy as jnp
import numpy as np
from jax import lax
from jax.experimental import pallas as pl
from jax.experimental.pallas import tpu as pltpu

F32 = jnp.float32
BF16 = jnp.bfloat16
I32 = jnp.int32

HEAD_DIM = 64
N_KV_HEADS = 4
ROT_HALF = 8
ROPE_THETA = 500000.0
IDX_HEADS = 8
IDX_DIM = 64
TOPK_MAX = 256
SSM_HEAD_DIM = 64
SSM_GROUPS = 4
D_STATE = 128
CONV_W = 4
SSD_CHUNK = 128
EPS = 1e-6

LANE = 128
SUBLANE = 8
VMEM_LIMIT = 56 * 1024 * 1024
IDX_PAGES_PER_STEP = 32
KV_PAGES_PER_STEP = 16
COUNT_STRIP = 64
SOFTMAX_KEY_BLOCKS = 2

IW_OFF = IDX_DIM
DT_OFF = IDX_DIM + IDX_HEADS
INT_MIN = -(2 ** 31)
NEG = -1e30
NEG_INF = float("-inf")
F32_LOWEST = float(np.finfo(np.float32).min)
LOG2E = 1.4426950408889634


def _sigmoid(x):
    return 1.0 / (1.0 + jnp.exp(-x))


def _silu(x):
    return x * _sigmoid(x)


def _softplus(x):
    return jnp.maximum(x, 0.0) + jnp.log1p(jnp.exp(-jnp.abs(x)))


def _bdot(a, b):
    return jnp.dot(a.astype(BF16), b.astype(BF16), preferred_element_type=F32)


def _bdot_nt(a, b):
    return lax.dot_general(a.astype(BF16), b.astype(BF16), (((1,), (1,)), ((), ())), preferred_element_type=F32)


def _split3_dot(v, e):
    hi = v.astype(BF16)
    r1 = v - hi.astype(F32)
    mid = r1.astype(BF16)
    lo = (r1 - mid.astype(F32)).astype(BF16)
    d = functools.partial(jnp.dot, preferred_element_type=F32)
    return d(hi, e) + d(mid, e) + d(lo, e)


def _f32_at_rank(u):
    key = u ^ jnp.int32(INT_MIN)
    bits = jnp.where(key < 0, jnp.int32(INT_MIN) - key, key)
    return lax.bitcast_convert_type(bits, F32)


def _kth_largest(count_ge, shape, topk, n_total):
    def cond(carry):
        t, _, _, unresolved = carry
        return jnp.logical_and(t < 32, unresolved > 0)

    def body(carry):
        t, tau_u, cnt_tau, _ = carry
        cand_u = tau_u | jnp.left_shift(jnp.int32(1), 31 - t)
        cnt = count_ge(_f32_at_rank(cand_u))
        take = cnt >= float(topk)
        cnt_tau = jnp.where(take, cnt, cnt_tau)
        unresolved = (jnp.max(cnt_tau) > float(topk)).astype(I32)
        return t + 1, jnp.where(take, cand_u, tau_u), cnt_tau, unresolved

    cnt0 = jnp.zeros(shape, F32) + n_total
    init = (jnp.int32(0), jnp.zeros(shape, I32), cnt0, (jnp.max(cnt0) > float(topk)).astype(I32))
    _, tau_u, cnt_tau, _ = lax.while_loop(cond, body, init)
    return jnp.where(tau_u == 0, NEG_INF, _f32_at_rank(tau_u)), cnt_tau


def _largest_divisor(n, cap):
    return max(k for k in range(1, cap + 1) if n % k == 0)


def _const_spec(shape):
    n = len(shape)
    return pl.BlockSpec(shape, lambda *a: (0,) * n, pipeline_mode=pl.Buffered(1))


def _params(n_axes):
    return pltpu.CompilerParams(dimension_semantics=("arbitrary",) * n_axes, vmem_limit_bytes=VMEM_LIMIT)


def _ada_kernel(c_ref, whi_ref, wlo_ref, b_ref, o_ref):
    s = _silu(c_ref[...])
    s_hi = s.astype(BF16)
    s_lo = (s - s_hi.astype(F32)).astype(BF16)
    d = functools.partial(jnp.dot, preferred_element_type=F32)
    o_ref[...] = d(s_hi, whi_ref[...]) + d(s_lo, whi_ref[...]) + d(s_hi, wlo_ref[...]) + b_ref[...]


def _ada(c_all, w_hi, w_lo, b_ada, tn=1024):
    r, d = c_all.shape
    n = w_hi.shape[1]
    return pl.pallas_call(
        _ada_kernel,
        grid=(n // tn,),
        in_specs=[
            pl.BlockSpec((r, d), lambda j: (0, 0)),
            pl.BlockSpec((d, tn), lambda j: (0, j)),
            pl.BlockSpec((d, tn), lambda j: (0, j)),
            pl.BlockSpec((1, tn), lambda j: (0, j)),
        ],
        out_specs=pl.BlockSpec((r, tn), lambda j: (0, j)),
        out_shape=jax.ShapeDtypeStruct((r, n), F32),
        compiler_params=_params(1),
        name="ada",
    )(c_all, w_hi, w_lo, b_ada)


def _rope_chunk(c, cos, s1, s2):
    return c * cos + pltpu.roll(c, LANE - ROT_HALF, 1) * s1 + pltpu.roll(c, ROT_HALF, 1) * s2


def _inproj_kernel(x_ref, sc_ref, sh_ref, nw_ref, w_ref, cos_ref, s1_ref, s2_ref, *outs, d_model, d_inner,
                   conv_dim, prompt):
    x = x_ref[...]
    h = x * lax.rsqrt(jnp.mean(x * x, axis=-1, keepdims=True) + EPS) * nw_ref[...]
    hb = (h * (1.0 + sc_ref[0]) + sh_ref[0]).astype(BF16)
    cos, s1, s2 = cos_ref[...], s1_ref[...], s2_ref[...]
    kv_w = N_KV_HEADS * HEAD_DIM
    iq_w = IDX_HEADS * IDX_DIM
    q0 = 0
    k0 = q0 + d_model
    v0 = k0 + kv_w
    iq0 = v0 + kv_w
    m0 = iq0 + iq_w
    z0 = m0 + LANE
    x0 = z0 + d_inner
    ga0 = x0 + conv_dim
    gs0 = ga0 + d_model

    def proj(lo, width):
        return jnp.dot(hb, w_ref[:, lo:lo + width], preferred_element_type=F32)

    if prompt:
        (q_ref, kt32_ref, kt_ref, vt32_ref, ve_ref, iqt_ref, ikt32_ref, ikr_ref, cwt_ref, misc_ref, z_ref, xbc_ref,
         ga_ref, gs_ref) = outs
    else:
        q_ref, k_ref, v_ref, iq_ref, misc_ref, z_ref, xbc_ref, ga_ref, gs_ref = outs

    q = proj(q0, d_model)
    q_scale = HEAD_DIM ** -0.5 * (LOG2E if prompt else 1.0)
    for c in range(d_model // LANE):
        qc = _rope_chunk(q[:, c * LANE:(c + 1) * LANE], cos, s1, s2) * q_scale
        if prompt:
            q_ref[0, 2 * c] = qc[:, :HEAD_DIM].astype(BF16)
            q_ref[0, 2 * c + 1] = qc[:, HEAD_DIM:].astype(BF16)
        else:
            q_ref[:, c * LANE:(c + 1) * LANE] = qc

    kk = proj(k0, kv_w)
    for c in range(kv_w // LANE):
        kc = _rope_chunk(kk[:, c * LANE:(c + 1) * LANE], cos, s1, s2)
        if prompt:
            kc_t = kc.T
            kt32_ref[0, c * LANE:(c + 1) * LANE, :] = kc_t
            kt_ref[0, c * LANE:(c + 1) * LANE, :] = kc_t.astype(BF16)
        else:
            k_ref[:, c * LANE:(c + 1) * LANE] = kc

    vv = proj(v0, kv_w)
    if not prompt:
        v_ref[...] = vv
    else:
        for c in range(kv_w // LANE):
            vt32_ref[0, c * LANE:(c + 1) * LANE, :] = vv[:, c * LANE:(c + 1) * LANE].T
        lane = lax.broadcasted_iota(I32, (vv.shape[0], LANE), 1)
        for g in range(N_KV_HEADS):
            vc = vv[:, (g // 2) * LANE:(g // 2 + 1) * LANE]
            if g % 2 == 1:
                vc = pltpu.roll(vc, HEAD_DIM, 1)
            ve = jnp.where(lane < HEAD_DIM, vc, jnp.where(lane == HEAD_DIM, 1.0, 0.0))
            ve_ref[0, g] = ve.astype(BF16)

    iq = proj(iq0, iq_w)
    for c in range(iq_w // LANE):
        ic = _rope_chunk(iq[:, c * LANE:(c + 1) * LANE], cos, s1, s2)
        if prompt:
            iqt_ref[0, c * LANE:(c + 1) * LANE, :] = ic.T.astype(BF16)
        else:
            iq_ref[:, c * LANE:(c + 1) * LANE] = ic.astype(BF16)

    mm = proj(m0, LANE)
    lane = lax.broadcasted_iota(I32, mm.shape, 1)
    mm = jnp.where(lane < IDX_DIM, _rope_chunk(mm, cos, s1, s2), mm)
    misc_ref[...] = mm
    if prompt:
        mm_t = mm.T
        ikt32_ref[0] = mm_t[:IDX_DIM, :]
        ikr_ref[...] = mm[:, :IDX_DIM].astype(BF16)
        cwt_ref[0] = mm_t[IW_OFF:IW_OFF + IDX_HEADS, :] * (IDX_HEADS ** -0.5) * (IDX_DIM ** -0.5)

    z_ref[...] = proj(z0, d_inner)
    xbc_ref[...] = proj(x0, conv_dim)
    ga_ref[...] = proj(ga0, d_model)
    gs_ref[...] = proj(gs0, d_model)


def _inproj(x2d, sc, sh, nw, w_all, cos, s1, s2, *, n_batch, seq, tm, prompt, d_inner, conv_dim):
    rows, d = x2d.shape
    tb = seq // tm
    r_mod = sc.shape[1]
    kv_w = N_KV_HEADS * HEAD_DIM
    iq_w = IDX_HEADS * IDX_DIM
    n_heads = d // HEAD_DIM
    if cos.shape[0] == 1:
        tab = pl.BlockSpec((1, LANE), lambda i: (0, 0))
    else:
        tab = pl.BlockSpec((tm, LANE), lambda i: (i % tb, 0))
    mod = pl.BlockSpec((1, r_mod, d), lambda i: (i // tb, 0, 0))
    in_specs = [pl.BlockSpec((tm, d), lambda i: (i, 0)), mod, mod, _const_spec((1, d)), _const_spec(w_all.shape),
                tab, tab, tab]

    def rowspec(w):
        return pl.BlockSpec((tm, w), lambda i: (i, 0))

    def sds(shape, dt):
        return jax.ShapeDtypeStruct(shape, dt)

    if prompt:
        def tspec(w):
            return pl.BlockSpec((1, w, tm), lambda i: (i // tb, 0, i % tb))

        out_specs = [
            pl.BlockSpec((1, n_heads, tm, HEAD_DIM), lambda i: (i // tb, 0, i % tb, 0)),
            tspec(kv_w), tspec(kv_w), tspec(kv_w),
            pl.BlockSpec((1, N_KV_HEADS, tm, LANE), lambda i: (i // tb, 0, i % tb, 0)),
            tspec(iq_w), tspec(IDX_DIM), rowspec(IDX_DIM), tspec(IDX_HEADS),
            rowspec(LANE), rowspec(d_inner), rowspec(conv_dim), rowspec(d), rowspec(d),
        ]
        out_shape = [
            sds((n_batch, n_heads, seq, HEAD_DIM), BF16), sds((n_batch, kv_w, seq), F32),
            sds((n_batch, kv_w, seq), BF16), sds((n_batch, kv_w, seq), F32),
            sds((n_batch, N_KV_HEADS, seq, LANE), BF16), sds((n_batch, iq_w, seq), BF16),
            sds((n_batch, IDX_DIM, seq), F32), sds((rows, IDX_DIM), BF16), sds((n_batch, IDX_HEADS, seq), F32),
            sds((rows, LANE), F32), sds((rows, d_inner), F32), sds((rows, conv_dim), F32), sds((rows, d), F32),
            sds((rows, d), F32),
        ]
    else:
        out_specs = [rowspec(d), rowspec(kv_w), rowspec(kv_w), rowspec(iq_w), rowspec(LANE), rowspec(d_inner),
                     rowspec(conv_dim), rowspec(d), rowspec(d)]
        out_shape = [sds((rows, d), F32), sds((rows, kv_w), F32), sds((rows, kv_w), F32), sds((rows, iq_w), BF16),
                     sds((rows, LANE), F32), sds((rows, d_inner), F32), sds((rows, conv_dim), F32),
                     sds((rows, d), F32), sds((rows, d), F32)]
    return pl.pallas_call(
        functools.partial(_inproj_kernel, d_model=d, d_inner=d_inner, conv_dim=conv_dim, prompt=prompt),
        grid=(rows // tm,),
        in_specs=in_specs,
        out_specs=out_specs,
        out_shape=out_shape,
        compiler_params=_params(1),
        name="inproj_prompt" if prompt else "inproj_sample",
    )(x2d, sc, sh, nw, w_all, cos, s1, s2)


def _attn_kernel(q_ref, iqt_ref, cwt_ref, kt_ref, ve_ref, ikr_ref, o_ref, sc_scr, m_scr, acc_scr, *, tq, topk):
    i = pl.program_id(1)
    nblk = i + 1
    tk = tq
    tkc = SOFTMAX_KEY_BLOCKS * tk
    nblk_c = (nblk + SOFTMAX_KEY_BLOCKS - 1) // SOFTMAX_KEY_BLOCKS
    hpg = q_ref.shape[1] // N_KV_HEADS
    cw = cwt_ref[0]
    kidx = lax.broadcasted_iota(I32, (tk, tq), 0)
    qidx = lax.broadcasted_iota(I32, (tk, tq), 1)

    def phase_a(j, carry):
        off = pl.multiple_of(j * tk, tk)
        ikb = ikr_ref[pl.ds(off, tk), :]
        acc = jnp.zeros((tk, tq), F32)
        for h in range(IDX_HEADS):
            x = jnp.dot(ikb, iqt_ref[0, h * IDX_DIM:(h + 1) * IDX_DIM, :], preferred_element_type=F32)
            acc = acc + jnp.maximum(x, 0.0) * cw[h:h + 1, :]
        ok = (off + kidx) <= (i * tq + qidx)
        sc_scr[pl.ds(off, tk), :] = jnp.where(ok, acc, NEG_INF)
        return carry

    lax.fori_loop(0, nblk_c * SOFTMAX_KEY_BLOCKS, phase_a, 0)

    def count(cmp, thr):
        def body(j, acc):
            off = pl.multiple_of(j * tk, tk)
            for r in range(tk // COUNT_STRIP):
                sb = sc_scr[pl.ds(off + r * COUNT_STRIP, COUNT_STRIP), :]
                acc = acc + jnp.where(cmp(sb, thr), 1.0, 0.0)
            return acc

        acc = lax.fori_loop(0, nblk, body, jnp.zeros((COUNT_STRIP, tq), F32))
        return jnp.sum(acc, axis=0, keepdims=True)

    tau, cnt_tau = _kth_largest(lambda thr: count(lax.ge, thr), (1, tq), topk, (nblk * tk).astype(F32))

    any_tie = jnp.max(jnp.where(cnt_tau > float(topk), 1.0, 0.0)) > 0.5

    @pl.when(any_tie)
    def _():
        need = float(topk) - count(lax.gt, tau)
        lower = jnp.where(qidx <= kidx, 1.0, 0.0).astype(BF16)

        def body(j, seen):
            off = pl.multiple_of(j * tk, tk)
            sb = sc_scr[pl.ds(off, tk), :]
            eq = sb == tau
            prefix = jnp.dot(lower, jnp.where(eq, 1.0, 0.0).astype(BF16), preferred_element_type=F32)
            late = jnp.where(seen + prefix > need, NEG_INF, sb)
            sc_scr[pl.ds(off, tk), :] = jnp.where(eq, late, sb)
            return seen + prefix[tk - 1:tk, :]

        lax.fori_loop(0, nblk, body, jnp.zeros((1, tq), F32))

    tau_c = jnp.maximum(tau, F32_LOWEST)
    m_scr[...] = jnp.full(m_scr.shape, NEG, F32)
    acc_scr[...] = jnp.zeros(acc_scr.shape, F32)

    def phase_c(jc, carry):
        off = pl.multiple_of(jc * tkc, tkc)
        bias = jnp.concatenate(
            [jnp.where(sc_scr[pl.ds(off + b * tk, tk), :] >= tau_c, 0.0, NEG).T for b in range(SOFTMAX_KEY_BLOCKS)],
            axis=1)
        for g in range(N_KV_HEADS):
            qg = q_ref[0, g * hpg:(g + 1) * hpg].reshape(hpg * tq, HEAD_DIM)
            s = jnp.dot(qg, kt_ref[0, g * HEAD_DIM:(g + 1) * HEAD_DIM, pl.ds(off, tkc)],
                        preferred_element_type=F32)
            s = s.reshape(hpg, tq, tkc) + bias[None]
            m_prev = m_scr[g]
            m_new = jnp.maximum(m_prev, jnp.max(s, axis=-1, keepdims=True))
            alpha = jnp.exp2(m_prev - m_new)
            p = jnp.exp2(s - jnp.concatenate([m_new] * (tkc // LANE), axis=-1))
            m_scr[g] = m_new
            pv = jnp.dot(p.reshape(hpg * tq, tkc).astype(BF16), ve_ref[0, g, pl.ds(off, tkc), :],
                         preferred_element_type=F32)
            acc_scr[g] = alpha * acc_scr[g] + pv.reshape(hpg, tq, LANE)
        return carry

    lax.fori_loop(0, nblk_c, phase_c, 0)

    for g in range(N_KV_HEADS):
        for hh in range(hpg):
            a = acc_scr[g, hh]
            hq = g * hpg + hh
            o_ref[:, hq * HEAD_DIM:(hq + 1) * HEAD_DIM] = a[:, :HEAD_DIM] / a[:, HEAD_DIM:HEAD_DIM + 1]


def _prompt_attention(q_hm, iqt, cwt, kt, ve, ikr, *, tq, topk):
    nb, nh, seq, _ = q_hm.shape
    hpg = nh // N_KV_HEADS
    tb = seq // tq
    kv_w = N_KV_HEADS * HEAD_DIM
    assert (seq // tq) % SOFTMAX_KEY_BLOCKS == 0

    def resident(shape, imap):
        return pl.BlockSpec(shape, imap, pipeline_mode=pl.Buffered(1))

    return pl.pallas_call(
        functools.partial(_attn_kernel, tq=tq, topk=topk),
        grid=(nb, tb),
        in_specs=[
            pl.BlockSpec((1, nh, tq, HEAD_DIM), lambda b, i: (b, 0, i, 0)),
            pl.BlockSpec((1, IDX_HEADS * IDX_DIM, tq), lambda b, i: (b, 0, i)),
            pl.BlockSpec((1, IDX_HEADS, tq), lambda b, i: (b, 0, i)),
            resident((1, kv_w, seq), lambda b, i: (b, 0, 0)),
            resident((1, N_KV_HEADS, seq, LANE), lambda b, i: (b, 0, 0, 0)),
            resident((seq, IDX_DIM), lambda b, i: (b, 0)),
        ],
        out_specs=pl.BlockSpec((tq, nh * HEAD_DIM), lambda b, i: (b * tb + i, 0)),
        out_shape=jax.ShapeDtypeStruct((nb * seq, nh * HEAD_DIM), F32),
        scratch_shapes=[
            pltpu.VMEM((seq, tq), F32),
            pltpu.VMEM((N_KV_HEADS, hpg, tq, LANE), F32),
            pltpu.VMEM((N_KV_HEADS, hpg, tq, LANE), F32),
        ],
        compiler_params=_params(2),
        name="prompt_attention",
    )(q_hm, iqt, cwt, kt, ve, ikr)


def _gated_group_norm(y, z, nw, d_inner):
    g = y * _silu(z)
    gw = d_inner // SSM_GROUPS
    outs = []
    for k in range(SSM_GROUPS):
        gg = g[:, k * gw:(k + 1) * gw]
        outs.append(gg * lax.rsqrt(jnp.mean(gg * gg, axis=-1, keepdims=True) + EPS))
    return jnp.concatenate(outs, axis=-1) * nw


def _ssd_kernel(xbc_ref, z_ref, misc_ref, cw_ref, cb_ref, dtb_ref, alog_ref, e_ref, dsk_ref, nw_ref, y_ref, st_ref,
                ext_scr, st_scr, y_scr, *, d_inner, n_heads):
    c = pl.program_id(1)
    q = xbc_ref.shape[0]
    gn = SSM_GROUPS * D_STATE
    hpg = n_heads // SSM_GROUPS
    gw = hpg * SSM_HEAD_DIM

    @pl.when(c == 0)
    def _():
        ext_scr[0:SUBLANE, :] = jnp.zeros((SUBLANE, ext_scr.shape[1]), F32)
        st_scr[...] = jnp.zeros(st_scr.shape, F32)

    u = xbc_ref[...]
    ext_scr[SUBLANE:, :] = u
    conv = cb_ref[...] + cw_ref[CONV_W - 1:CONV_W, :] * u
    for w in range(CONV_W - 1):
        lo = SUBLANE - (CONV_W - 1) + w
        conv = conv + cw_ref[w:w + 1, :] * ext_scr[lo:lo + q, :]
    ext_scr[0:SUBLANE, :] = u[q - SUBLANE:, :]
    act = _silu(conv)
    xs = act[:, :d_inner]
    bm = act[:, d_inner:d_inner + gn]
    cm = act[:, d_inner + gn:]

    dt_t = _softplus(misc_ref[...].T + dtb_ref[...])
    da_t = dt_t * (-jnp.exp(alog_ref[...]))
    ri = lax.broadcasted_iota(I32, (q, q), 0)
    ci = lax.broadcasted_iota(I32, (q, q), 1)
    cs_t = _split3_dot(da_t, jnp.where(ri <= ci, 1.0, 0.0).astype(BF16))
    cs = cs_t.T
    dt = dt_t.T
    e = e_ref[...]
    ecs_x = _split3_dot(jnp.exp(cs), e)
    wst_x = _split3_dot(dt * jnp.exp(cs[q - 1:q, :] - cs), e)
    tri = ri >= ci

    for g in range(SSM_GROUPS):
        bg = bm[:, g * D_STATE:(g + 1) * D_STATE]
        cg = cm[:, g * D_STATE:(g + 1) * D_STATE].astype(BF16)
        bg_t = bg.T.astype(BF16)
        cb = jnp.dot(cg, bg_t, preferred_element_type=F32)
        s_t = st_scr[g]
        y_off = jnp.dot(cg, s_t.astype(BF16), preferred_element_type=F32)
        xg = xs[:, g * gw:(g + 1) * gw]
        y_g = y_off * ecs_x[:, g * gw:(g + 1) * gw]
        pieces = []
        for hh in range(hpg):
            idx = DT_OFF + g * hpg + hh
            seg = cs[:, idx:idx + 1] - cs_t[idx:idx + 1, :]
            mm = cb * jnp.exp(jnp.where(tri, seg, NEG)) * dt_t[idx:idx + 1, :]
            pieces.append(_bdot(mm, xg[:, hh * SSM_HEAD_DIM:(hh + 1) * SSM_HEAD_DIM]))
        y_scr[:, g * gw:(g + 1) * gw] = y_g + jnp.concatenate(pieces, axis=-1)
        wg = (xg * wst_x[:, g * gw:(g + 1) * gw]).astype(BF16)
        st_scr[g] = s_t * ecs_x[q - 1:q, g * gw:(g + 1) * gw] + jnp.dot(bg_t, wg, preferred_element_type=F32)

    y = y_scr[...] + dsk_ref[...] * xs
    y_ref[...] = _gated_group_norm(y, z_ref[...], nw_ref[...], d_inner)

    @pl.when(c == pl.num_programs(1) - 1)
    def _():
        for g in range(SSM_GROUPS):
            st_ref[0, g * hpg:(g + 1) * hpg] = st_scr[g].T.reshape(hpg, SSM_HEAD_DIM, D_STATE)


def _prompt_ssd(xbc, z, misc, conv_w, conv_b, dtb_col, alog_col, e_mat, dsk_row, nw_row, *, n_batch, seq, d_inner,
                n_heads):
    q = SSD_CHUNK
    nc = seq // q
    cd = xbc.shape[1]
    hpg = n_heads // SSM_GROUPS
    return pl.pallas_call(
        functools.partial(_ssd_kernel, d_inner=d_inner, n_heads=n_heads),
        grid=(n_batch, nc),
        in_specs=[
            pl.BlockSpec((q, cd), lambda b, c: (b * nc + c, 0)),
            pl.BlockSpec((q, d_inner), lambda b, c: (b * nc + c, 0)),
            pl.BlockSpec((q, LANE), lambda b, c: (b * nc + c, 0)),
            _const_spec(conv_w.shape), _const_spec(conv_b.shape), _const_spec(dtb_col.shape),
            _const_spec(alog_col.shape), _const_spec(e_mat.shape), _const_spec(dsk_row.shape),
            _const_spec(nw_row.shape),
        ],
        out_specs=[
            pl.BlockSpec((q, d_inner), lambda b, c: (b * nc + c, 0)),
            pl.BlockSpec((1, n_heads, SSM_HEAD_DIM, D_STATE), lambda b, c: (b, 0, 0, 0)),
        ],
        out_shape=[
            jax.ShapeDtypeStruct((n_batch * seq, d_inner), F32),
            jax.ShapeDtypeStruct((n_batch, n_heads, SSM_HEAD_DIM, D_STATE), F32),
        ],
        scratch_shapes=[
            pltpu.VMEM((q + SUBLANE, cd), F32),
            pltpu.VMEM((SSM_GROUPS, D_STATE, hpg * SSM_HEAD_DIM), F32),
            pltpu.VMEM((q, d_inner), F32),
        ],
        compiler_params=_params(2),
        name="prompt_ssd",
    )(xbc, z, misc, conv_w, conv_b, dtb_col, alog_col, e_mat, dsk_row, nw_row)


def _mlp_kernel(x_ref, att_ref, y_ref, ga_ref, gs_ref, gt1_ref, sc2_ref, sh2_ref, gt2_ref, n2_ref, fn_ref, wo_ref,
                wu_ref, wd_ref, o_ref, *, ff_chunk):
    merged = _sigmoid(ga_ref[...]) * att_ref[...] + _sigmoid(gs_ref[...]) * y_ref[...]
    x1 = x_ref[...] + gt1_ref[0] * jnp.dot(merged.astype(BF16), wo_ref[...], preferred_element_type=F32)
    h2 = x1 * lax.rsqrt(jnp.mean(x1 * x1, axis=-1, keepdims=True) + EPS) * n2_ref[...]
    hb = (h2 * (1.0 + sc2_ref[0]) + sh2_ref[0]).astype(BF16)
    acc = jnp.zeros(x1.shape, F32)
    for c in range(wu_ref.shape[1] // ff_chunk):
        u = jnp.maximum(jnp.dot(hb, wu_ref[:, c * ff_chunk:(c + 1) * ff_chunk], preferred_element_type=F32), 0.0)
        acc = acc + jnp.dot((u * u).astype(BF16), wd_ref[c * ff_chunk:(c + 1) * ff_chunk, :],
                            preferred_element_type=F32)
    x2 = x1 + gt2_ref[0] * acc
    o_ref[...] = x2 * lax.rsqrt(jnp.mean(x2 * x2, axis=-1, keepdims=True) + EPS) * fn_ref[...]


def _mlp(x2d, att, y, ga, gs, gt1, sc2, sh2, gt2, n2, fn, wo, wu, wd, *, seq, tm, ff_chunk=1024):
    rows, d = x2d.shape
    tb = seq // tm
    r_mod = gt1.shape[1]
    row = pl.BlockSpec((tm, d), lambda i: (i, 0))
    mod = pl.BlockSpec((1, r_mod, d), lambda i: (i // tb, 0, 0))
    return pl.pallas_call(
        functools.partial(_mlp_kernel, ff_chunk=ff_chunk),
        grid=(rows // tm,),
        in_specs=[row, row, row, row, row, mod, mod, mod, mod, _const_spec((1, d)), _const_spec((1, d)),
                  _const_spec(wo.shape), _const_spec(wu.shape), _const_spec(wd.shape)],
        out_specs=row,
        out_shape=jax.ShapeDtypeStruct((rows, d), F32),
        compiler_params=_params(1),
        name="merge_mlp",
    )(x2d, att, y, ga, gs, gt1, sc2, sh2, gt2, n2, fn, wo, wu, wd)


def _sidx_kernel(pt_ref, iq_ref, cw_ref, iknew_ref, *rest, pages_per_step, n_steps):
    del pt_ref
    page_refs, o_ref = rest[:pages_per_step], rest[pages_per_step]
    r = pl.program_id(1)
    iq = iq_ref[0]
    cw = cw_ref[0]

    def score(ik_t):
        x = _bdot(iq, ik_t)
        return jnp.sum(jnp.maximum(x, 0.0) * cw, axis=0, keepdims=True)

    @pl.when(r < n_steps)
    def _():
        for m in range(pages_per_step):
            o_ref[0, :, m * LANE:(m + 1) * LANE] = score(page_refs[m][...])

    @pl.when(r == n_steps)
    def _():
        s_new = score(iknew_ref[0])[:, 0:1]
        lane = lax.broadcasted_iota(I32, (1, o_ref.shape[2]), 1)
        o_ref[0] = jnp.where(lane == 0, jnp.broadcast_to(s_new, lane.shape), NEG_INF)


def _sample_index_scores(page_table, iq3, cw3, iknew_t, idx_cache_t, *, pages_per_step):
    db, n_pages = page_table.shape
    page = idx_cache_t.shape[3]
    n_steps = n_pages // pages_per_step
    width = pages_per_step * page

    def page_spec(m):
        return pl.BlockSpec(
            (None, None, IDX_DIM, page),
            lambda b, r, pt: (0, pt[b, jnp.minimum(r, n_steps - 1) * pages_per_step + m], 0, 0))

    grid_spec = pltpu.PrefetchScalarGridSpec(
        num_scalar_prefetch=1,
        grid=(db, n_steps + 1),
        in_specs=[
            pl.BlockSpec((1, IDX_HEADS, IDX_DIM), lambda b, r, pt: (b, 0, 0)),
            pl.BlockSpec((1, IDX_HEADS, 1), lambda b, r, pt: (b, 0, 0)),
            pl.BlockSpec((1, IDX_DIM, page), lambda b, r, pt: (b, 0, 0)),
        ] + [page_spec(m) for m in range(pages_per_step)],
        out_specs=pl.BlockSpec((1, 1, width), lambda b, r, pt: (b, 0, r)),
    )
    return pl.pallas_call(
        functools.partial(_sidx_kernel, pages_per_step=pages_per_step, n_steps=n_steps),
        grid_spec=grid_spec,
        out_shape=jax.ShapeDtypeStruct((db, 1, (n_steps + 1) * width), F32),
        compiler_params=_params(2),
        name="sample_index_scores",
    )(page_table, iq3, cw3, iknew_t, *([idx_cache_t] * pages_per_step))


def _ssel_kernel(sc_ref, bias_ref, *, topk):
    sc = sc_ref[...]
    db, lk = sc.shape

    def count_ge(thr):
        return jnp.sum(jnp.where(sc >= thr, 1.0, 0.0), axis=1, keepdims=True)

    tau, _ = _kth_largest(count_ge, (db, 1), topk, jnp.float32(lk))
    need = float(topk) - jnp.sum(jnp.where(sc > tau, 1.0, 0.0), axis=1, keepdims=True)
    ri = lax.broadcasted_iota(I32, (LANE, LANE), 0)
    ci = lax.broadcasted_iota(I32, (LANE, LANE), 1)
    upper = jnp.where(ri <= ci, 1.0, 0.0).astype(BF16)

    def body(j, seen):
        off = pl.multiple_of(j * LANE, LANE)
        sb = sc_ref[:, pl.ds(off, LANE)]
        eq = sb == tau
        prefix = jnp.dot(jnp.where(eq, 1.0, 0.0).astype(BF16), upper, preferred_element_type=F32)
        keep_eq = jnp.where(seen + prefix <= need, 0.0, NEG)
        sel = jnp.where(sb > tau, 0.0, jnp.where(eq, keep_eq, NEG))
        bias_ref[:, pl.ds(off, LANE)] = jnp.where(sb == NEG_INF, NEG, sel)
        return seen + prefix[:, LANE - 1:LANE]

    lax.fori_loop(0, lk // LANE, body, jnp.zeros((db, 1), F32))


def _sample_select(keys2d, *, topk):
    return pl.pallas_call(
        functools.partial(_ssel_kernel, topk=topk),
        out_shape=jax.ShapeDtypeStruct(keys2d.shape, F32),
        compiler_params=pltpu.CompilerParams(vmem_limit_bytes=VMEM_LIMIT),
        name="sample_select",
    )(keys2d)


def _sattn_kernel(pt_ref, q_ref, bias_ref, knew_ref, vnew_ref, *rest, pages_per_step, n_steps):
    del pt_ref
    k_refs = rest[:pages_per_step]
    v_refs = rest[pages_per_step:2 * pages_per_step]
    o_ref, m_scr, l_scr, acc_scr = rest[2 * pages_per_step:]
    r = pl.program_id(1)
    q = q_ref[0]
    bias = bias_ref[0]

    @pl.when(r == 0)
    def _():
        m_scr[...] = jnp.full(m_scr.shape, NEG, F32)
        l_scr[...] = jnp.zeros(l_scr.shape, F32)
        acc_scr[...] = jnp.zeros(acc_scr.shape, F32)

    def update(s, pv_fn):
        m_prev = m_scr[...]
        m_new = jnp.maximum(m_prev, jnp.max(s, axis=1, keepdims=True))
        alpha = jnp.exp(m_prev - m_new)
        p = jnp.exp(s - m_new[:, 0:1])
        m_scr[...] = m_new
        l_scr[...] = alpha * l_scr[...] + jnp.sum(p, axis=1, keepdims=True)
        acc_scr[...] = alpha[:, 0:1] * acc_scr[...] + pv_fn(p)

    @pl.when(r < n_steps)
    def _():
        s = jnp.concatenate([_bdot(q, k_refs[m][...]) for m in range(pages_per_step)], axis=1) + bias

        def pv(p):
            out = _bdot_nt(p[:, 0:LANE], v_refs[0][...])
            for m in range(1, pages_per_step):
                out = out + _bdot_nt(p[:, m * LANE:(m + 1) * LANE], v_refs[m][...])
            return out

        update(s, pv)

    @pl.when(r == n_steps)
    def _():
        s = _bdot(q, knew_ref[0]) + bias[:, 0:LANE]
        update(s, lambda p: _bdot_nt(p, vnew_ref[0]))
        o_ref[0] = acc_scr[...] / l_scr[:, 0:1]


def _sample_attention(page_table, qmat, bias3, knew_t, vnew_t, cache_kt, cache_vt, *, pages_per_step):
    db, n_pages = page_table.shape
    kv_w, page = cache_kt.shape[2], cache_kt.shape[3]
    nh = qmat.shape[1]
    n_steps = n_pages // pages_per_step
    width = pages_per_step * page

    def page_spec(m):
        return pl.BlockSpec(
            (None, None, kv_w, page),
            lambda b, r, pt: (0, pt[b, jnp.minimum(r, n_steps - 1) * pages_per_step + m], 0, 0))

    pages = [page_spec(m) for m in range(pages_per_step)]
    grid_spec = pltpu.PrefetchScalarGridSpec(
        num_scalar_prefetch=1,
        grid=(db, n_steps + 1),
        in_specs=[
            pl.BlockSpec((1, nh, kv_w), lambda b, r, pt: (b, 0, 0)),
            pl.BlockSpec((1, 1, width), lambda b, r, pt: (b, 0, r)),
            pl.BlockSpec((1, kv_w, page), lambda b, r, pt: (b, 0, 0)),
            pl.BlockSpec((1, kv_w, page), lambda b, r, pt: (b, 0, 0)),
        ] + pages + pages,
        out_specs=pl.BlockSpec((1, nh, kv_w), lambda b, r, pt: (b, 0, 0)),
        scratch_shapes=[pltpu.VMEM((nh, LANE), F32), pltpu.VMEM((nh, LANE), F32), pltpu.VMEM((nh, kv_w), F32)],
    )
    return pl.pallas_call(
        functools.partial(_sattn_kernel, pages_per_step=pages_per_step, n_steps=n_steps),
        grid_spec=grid_spec,
        out_shape=jax.ShapeDtypeStruct((db, nh, kv_w), F32),
        compiler_params=_params(2),
        name="sample_attention",
    )(page_table, qmat, bias3, knew_t, vnew_t, *([cache_kt] * pages_per_step), *([cache_vt] * pages_per_step))


def _sssd_kernel(xbc_ref, prev_ref, misc_ref, z_ref, st_ref, cw_ref, cb_ref, dtb_ref, alog_ref, e_ref, dsk_ref,
                 nw_ref, y_ref, sto_ref, *, d_inner, n_heads):
    gn = SSM_GROUPS * D_STATE
    hpg = n_heads // SSM_GROUPS
    gw = hpg * SSM_HEAD_DIM
    u = xbc_ref[0]
    prev = prev_ref[0]
    conv = cb_ref[...] + cw_ref[CONV_W - 1:CONV_W, :] * u
    for w in range(CONV_W - 1):
        conv = conv + cw_ref[w:w + 1, :] * prev[w:w + 1, :]
    act = _silu(conv)
    xs = act[:, :d_inner]
    bm = act[:, d_inner:d_inner + gn]
    cm = act[:, d_inner + gn:]
    dt = _softplus(misc_ref[0] + dtb_ref[...])
    dec = jnp.exp(dt * (-jnp.exp(alog_ref[...])))
    pad6 = jnp.zeros((SUBLANE - 2, LANE), F32)
    ex = _split3_dot(jnp.concatenate([dt, dec, pad6], axis=0), e_ref[...])
    rows = jnp.concatenate([xs * ex[0:1, :], ex[1:2, :], jnp.zeros((LANE - 2, d_inner), F32)], axis=0)
    cols = rows.T
    pad15 = jnp.zeros((2 * SUBLANE - 1, D_STATE), F32)
    ys = []
    for g in range(SSM_GROUPS):
        s0 = st_ref[0, g * hpg:(g + 1) * hpg].reshape(gw, D_STATE)
        cg = cols[g * gw:(g + 1) * gw, :]
        s1 = s0 * cg[:, 1:2] + cg[:, 0:1] * bm[:, g * D_STATE:(g + 1) * D_STATE]
        sto_ref[0, g * hpg:(g + 1) * hpg] = s1.reshape(hpg, SSM_HEAD_DIM, D_STATE)
        c16 = jnp.concatenate([cm[:, g * D_STATE:(g + 1) * D_STATE], pad15], axis=0)
        ys.append(_bdot_nt(c16, s1)[0:1, :])
    y = jnp.concatenate(ys, axis=-1) + dsk_ref[...] * xs
    y_ref[0] = _gated_group_norm(y, z_ref[0], nw_ref[...], d_inner)


def _sample_ssd(xbc3, prev, misc3, z3, state, conv_w, conv_b, dtb_row, alog_row, e_mat, dsk_row, nw_row, *,
                d_inner, n_heads):
    db = xbc3.shape[0]
    cd = xbc3.shape[2]

    def per(shape):
        n = len(shape)
        return pl.BlockSpec((1,) + tuple(shape[1:]), lambda b: (b,) + (0,) * (n - 1))

    return pl.pallas_call(
        functools.partial(_sssd_kernel, d_inner=d_inner, n_heads=n_heads),
        grid=(db,),
        in_specs=[per(xbc3.shape), per(prev.shape), per(misc3.shape), per(z3.shape), per(state.shape),
                  _const_spec(conv_w.shape), _const_spec(conv_b.shape), _const_spec(dtb_row.shape),
                  _const_spec(alog_row.shape), _const_spec(e_mat.shape), _const_spec(dsk_row.shape),
                  _const_spec(nw_row.shape)],
        out_specs=[per((db, 1, d_inner)), per(state.shape)],
        out_shape=[jax.ShapeDtypeStruct((db, 1, d_inner), F32), jax.ShapeDtypeStruct(state.shape, F32)],
        compiler_params=_params(1),
        name="sample_ssd",
    )(xbc3, prev, misc3, z3, state, conv_w, conv_b, dtb_row, alog_row, e_mat, dsk_row, nw_row)


def _rope_tables(pos):
    inv = ROPE_THETA ** (-jnp.arange(ROT_HALF, dtype=F32) * 2.0 / (2 * ROT_HALF))
    ang = pos.astype(F32)[:, None] * inv[None, :]
    cos, sin = jnp.cos(ang), jnp.sin(ang)
    n = pos.shape[0]
    one = jnp.ones((n, HEAD_DIM - 2 * ROT_HALF), F32)
    zero = jnp.zeros((n, HEAD_DIM - 2 * ROT_HALF), F32)
    z8 = jnp.zeros((n, ROT_HALF), F32)
    c_head = jnp.concatenate([cos, cos, one], axis=1)
    s1_head = jnp.concatenate([-sin, z8, zero], axis=1)
    s2_head = jnp.concatenate([z8, sin, zero], axis=1)
    rep = LANE // HEAD_DIM
    return jnp.tile(c_head, (1, rep)), jnp.tile(s1_head, (1, rep)), jnp.tile(s2_head, (1, rep))


def kernel(x_prompt, x_sample, cache_k, cache_v, cache_idx_k, state_conv, state_ssm, page_table, c_prompt, c_sample,
           w_ada, b_ada, norm1_w, w_in, conv_w, conv_b, dt_bias, a_log, d_skip, ssm_norm_w, w_out, norm2_w, w_up,
           w_down, final_norm_w):
    nb, seq, d = x_prompt.shape
    db, dec_seq, _ = x_sample.shape
    depth = w_in.shape[0]
    assert depth == 1 and dec_seq == 1 and seq % SSD_CHUNK == 0
    n_heads = d // HEAD_DIM
    n_ssm_heads = a_log.shape[1]
    d_inner = n_ssm_heads * SSM_HEAD_DIM
    conv_dim = conv_w.shape[2]
    kv_w = N_KV_HEADS * HEAD_DIM
    iq_w = IDX_HEADS * IDX_DIM
    n_pages, page = page_table.shape[1], cache_k.shape[2]
    past = n_pages * page

    wi = w_in[0]
    offs = np.cumsum([0, d, kv_w, kv_w, iq_w, IDX_DIM, IDX_HEADS, d_inner, conv_dim, n_ssm_heads, d, d])
    seg = [wi[:, offs[k]:offs[k + 1]] for k in range(11)]
    misc_w = jnp.concatenate(
        [seg[4], seg[5], seg[8], jnp.zeros((d, LANE - IDX_DIM - IDX_HEADS - n_ssm_heads), F32)], axis=1)
    w_all = jnp.concatenate([seg[0], seg[1], seg[2], seg[3], misc_w, seg[6], seg[7], seg[9], seg[10]],
                            axis=1).astype(BF16)
    wo, wu, wd = w_out[0].astype(BF16), w_up[0].astype(BF16), w_down[0].astype(BF16)
    nw1, nw2, fnw = norm1_w[0][None, :], norm2_w[0][None, :], final_norm_w[None, :]
    head_lane = jnp.zeros((LANE,), F32)
    dtb_row = head_lane.at[DT_OFF:DT_OFF + n_ssm_heads].set(dt_bias[0])[None, :]
    alog_row = head_lane.at[DT_OFF:DT_OFF + n_ssm_heads].set(a_log[0])[None, :]
    e_mat = jnp.zeros((LANE, d_inner), F32).at[DT_OFF:DT_OFF + n_ssm_heads].set(
        jnp.repeat(jnp.eye(n_ssm_heads, dtype=F32), SSM_HEAD_DIM, axis=1)).astype(BF16)
    dsk_row = jnp.repeat(d_skip[0], SSM_HEAD_DIM)[None, :]
    snw_row = ssm_norm_w[0][None, :]
    cw2, cb2 = conv_w[0], conv_b[0][None, :]

    n_mod = nb + db
    r_mod = -(-n_mod // SUBLANE) * SUBLANE
    c_all = jnp.concatenate([c_prompt, c_sample, jnp.zeros((r_mod - n_mod, d), F32)], axis=0)
    wa_hi = w_ada[0].astype(BF16)
    wa_lo = (w_ada[0] - wa_hi.astype(F32)).astype(BF16)
    mod = _ada(c_all, wa_hi, wa_lo, b_ada[0][None, :])
    sh1, sc1, gt1, sh2, sc2, gt2 = [mod[:, k * d:(k + 1) * d] for k in range(6)]

    def pmod(a):
        return a[:nb].reshape(nb, 1, d)

    def smod(a):
        return a[nb:nb + db].reshape(1, db, d)

    xp = x_prompt.reshape(nb * seq, d)
    cos_p, s1_p, s2_p = _rope_tables(jnp.arange(seq))
    (q_hm, kt32, kt, vt32, ve, iqt, ikt32, ikr, cwt, misc_p, z_p, xbc_p, ga_p, gs_p) = _inproj(
        xp, pmod(sc1), pmod(sh1), nw1, w_all, cos_p, s1_p, s2_p, n_batch=nb, seq=seq, tm=256, prompt=True,
        d_inner=d_inner, conv_dim=conv_dim)
    topk_p = min(TOPK_MAX, seq // 4)
    att_p = _prompt_attention(q_hm, iqt, cwt, kt, ve, ikr, tq=256, topk=topk_p)
    y_p, ssm_p = _prompt_ssd(xbc_p, z_p, misc_p, cw2, cb2, dtb_row.T, alog_row.T, e_mat, dsk_row, snw_row,
                             n_batch=nb, seq=seq, d_inner=d_inner, n_heads=n_ssm_heads)
    out_p = _mlp(xp, att_p, y_p, ga_p, gs_p, pmod(gt1), pmod(sc2), pmod(sh2), pmod(gt2), nw2, fnw, wo, wu, wd,
                 seq=seq, tm=256)

    xs2 = x_sample.reshape(db, d)
    cos_s, s1_s, s2_s = _rope_tables(jnp.full((1,), past, jnp.int32))
    (q_s, k_s, v_s, iq_s, misc_s, z_s, xbc_s, ga_s, gs_s) = _inproj(
        xs2, smod(sc1), smod(sh1), nw1, w_all, cos_s, s1_s, s2_s, n_batch=1, seq=db, tm=db, prompt=False,
        d_inner=d_inner, conv_dim=conv_dim)
    pps_idx = _largest_divisor(n_pages, IDX_PAGES_PER_STEP)
    pps_kv = _largest_divisor(n_pages, KV_PAGES_PER_STEP)
    cw_s = (misc_s[:, IW_OFF:IW_OFF + IDX_HEADS] * (IDX_HEADS ** -0.5) * (IDX_DIM ** -0.5)).reshape(db, IDX_HEADS, 1)

    def as_page_t(a):
        return jnp.pad(a[:, :, None], ((0, 0), (0, 0), (0, page - 1)))

    idx_cache_t = jnp.transpose(cache_idx_k, (0, 1, 3, 2))
    kv_shape_t = (depth, cache_k.shape[1], kv_w, page)
    cache_kt = jnp.transpose(cache_k, (0, 1, 3, 4, 2)).reshape(kv_shape_t)
    cache_vt = jnp.transpose(cache_v, (0, 1, 3, 4, 2)).reshape(kv_shape_t)

    sc_s = _sample_index_scores(page_table, iq_s.reshape(db, IDX_HEADS, IDX_DIM), cw_s,
                                as_page_t(misc_s[:, :IDX_DIM]), idx_cache_t, pages_per_step=pps_idx)
    topk_s = min(TOPK_MAX, (past + 1) // 4)
    lk = past + pps_kv * page
    assert pps_idx >= pps_kv
    bias_s = _sample_select(sc_s[:, 0, :lk], topk=topk_s).reshape(db, 1, lk)
    hpg = n_heads // N_KV_HEADS
    q5 = q_s.reshape(db, N_KV_HEADS, hpg, 1, HEAD_DIM)
    eye = jnp.eye(N_KV_HEADS, dtype=F32)[None, :, None, :, None]
    qmat = (q5 * eye).reshape(db, n_heads, kv_w).astype(BF16)
    acc_s = _sample_attention(page_table, qmat, bias_s, as_page_t(k_s), as_page_t(v_s), cache_kt, cache_vt,
                              pages_per_step=pps_kv)
    a5 = acc_s.reshape(db, N_KV_HEADS, hpg, N_KV_HEADS, HEAD_DIM)
    att_s = jnp.einsum('bghgd->bghd', a5).reshape(db, d)
    y_s, ssm_s = _sample_ssd(xbc_s.reshape(db, 1, conv_dim), state_conv[0], misc_s.reshape(db, 1, LANE),
                             z_s.reshape(db, 1, d_inner), state_ssm[0], cw2, cb2, dtb_row, alog_row, e_mat, dsk_row,
                             snw_row, d_inner=d_inner, n_heads=n_ssm_heads)
    out_s = _mlp(xs2, att_s, y_s.reshape(db, d_inner), ga_s, gs_s, smod(gt1), smod(sc2), smod(sh2), smod(gt2), nw2,
                 fnw, wo, wu, wd, seq=db, tm=db)

    conv_p = xbc_p.reshape(nb, seq, conv_dim)[:, seq - (CONV_W - 1):, :]
    conv_s = jnp.concatenate([state_conv[0][:, 1:, :], xbc_s[:, None, :]], axis=1)
    def kv_out(a_t):
        return jnp.transpose(a_t.reshape(nb, N_KV_HEADS, HEAD_DIM, seq), (0, 3, 1, 2))[None]

    return (
        out_p.reshape(nb, seq, d),
        out_s.reshape(db, 1, d),
        kv_out(kt32),
        kv_out(vt32),
        jnp.transpose(ikt32, (0, 2, 1))[None],
        conv_p[None],
        ssm_p[None],
        k_s.reshape(1, db, 1, N_KV_HEADS, HEAD_DIM),
        v_s.reshape(1, db, 1, N_KV_HEADS, HEAD_DIM),
        misc_s[:, :IDX_DIM].reshape(1, db, 1, IDX_DIM),
        conv_s[None],
        ssm_s[None],
    )
```

```python
import functools

import jax
import jax.numpy as jnp
import numpy as np
from jax import lax
from jax.experimental import pallas as pl
from jax.experimental.pallas import tpu as pltpu

F32 = jnp.float32
BF16 = jnp.bfloat16
I32 = jnp.int32

HEAD_DIM = 64
N_KV_HEADS = 4
ROT_HALF = 8
ROPE_THETA = 500000.0
IDX_HEADS = 8
IDX_DIM = 64
TOPK_MAX = 256
SSM_HEAD_DIM = 64
SSM_GROUPS = 4
D_STATE = 128
CONV_W = 4
SSD_CHUNK = 128
EPS = 1e-6

LANE = 128
SUBLANE = 8
VMEM_LIMIT = 56 * 1024 * 1024
IDX_PAGES_PER_STEP = 32
KV_PAGES_PER_STEP = 16
COUNT_STRIP = 64
SOFTMAX_KEY_BLOCKS = 2

IW_OFF = IDX_DIM
DT_OFF = IDX_DIM + IDX_HEADS
INT_MIN = -(2 ** 31)
NEG = -1e30
NEG_INF = float("-inf")
F32_LOWEST = float(np.finfo(np.float32).min)
LOG2E = 1.4426950408889634
BF16_STEP_DOWN = 1.25 * 2.0 ** -8


def _sigmoid(x):
    return 1.0 / (1.0 + jnp.exp(-x))


def _silu(x):
    return x * _sigmoid(x)


def _softplus(x):
    return jnp.maximum(x, 0.0) + jnp.log1p(jnp.exp(-jnp.abs(x)))


def _bdot(a, b):
    return jnp.dot(a.astype(BF16), b.astype(BF16), preferred_element_type=F32)


def _bdot_nt(a, b):
    return lax.dot_general(a.astype(BF16), b.astype(BF16), (((1,), (1,)), ((), ())), preferred_element_type=F32)


def _split3_dot(v, e):
    hi = v.astype(BF16)
    r1 = v - hi.astype(F32)
    mid = r1.astype(BF16)
    lo = (r1 - mid.astype(F32)).astype(BF16)
    d = functools.partial(jnp.dot, preferred_element_type=F32)
    return d(hi, e) + d(mid, e) + d(lo, e)


def _f32_at_rank(u):
    key = u ^ jnp.int32(INT_MIN)
    bits = jnp.where(key < 0, jnp.int32(INT_MIN) - key, key)
    return lax.bitcast_convert_type(bits, F32)


def _floor_to_bf16(x):
    r = x.astype(BF16).astype(F32)
    below = r - jnp.abs(r) * BF16_STEP_DOWN
    return jnp.where(r > x, below, r).astype(BF16)


def _kth_largest(stages, shape, topk, n_total):
    cnt0 = jnp.zeros(shape, F32) + n_total
    carry = (jnp.int32(0), jnp.zeros(shape, I32), cnt0, (jnp.max(cnt0) > float(topk)).astype(I32))
    for end_bit, count_ge in stages:
        def cond(c, end_bit=end_bit):
            t, _, _, unresolved = c
            return jnp.logical_and(t < end_bit, unresolved > 0)

        def body(c, count_ge=count_ge):
            t, tau_u, cnt_tau, _ = c
            cand_u = tau_u | jnp.left_shift(jnp.int32(1), 31 - t)
            cnt = count_ge(_f32_at_rank(cand_u))
            take = cnt >= float(topk)
            cnt_tau = jnp.where(take, cnt, cnt_tau)
            unresolved = (jnp.max(cnt_tau) > float(topk)).astype(I32)
            return t + 1, jnp.where(take, cand_u, tau_u), cnt_tau, unresolved

        carry = lax.while_loop(cond, body, carry)
    _, tau_u, cnt_tau, _ = carry
    return jnp.where(tau_u == 0, NEG_INF, _f32_at_rank(tau_u)), cnt_tau


def _largest_divisor(n, cap):
    return max(k for k in range(1, cap + 1) if n % k == 0)


def _const_spec(shape):
    n = len(shape)
    return pl.BlockSpec(shape, lambda *a: (0,) * n, pipeline_mode=pl.Buffered(1))


def _params(n_axes):
    return pltpu.CompilerParams(dimension_semantics=("arbitrary",) * n_axes, vmem_limit_bytes=VMEM_LIMIT)


def _ada_kernel(c_ref, whi_ref, wlo_ref, b_ref, o_ref):
    s = _silu(c_ref[...])
    s_hi = s.astype(BF16)
    s_lo = (s - s_hi.astype(F32)).astype(BF16)
    d = functools.partial(jnp.dot, preferred_element_type=F32)
    o_ref[...] = d(s_hi, whi_ref[...]) + d(s_lo, whi_ref[...]) + d(s_hi, wlo_ref[...]) + b_ref[...]


def _ada(c_all, w_hi, w_lo, b_ada, tn=1024):
    r, d = c_all.shape
    n = w_hi.shape[1]
    return pl.pallas_call(
        _ada_kernel,
        grid=(n // tn,),
        in_specs=[
            pl.BlockSpec((r, d), lambda j: (0, 0)),
            pl.BlockSpec((d, tn), lambda j: (0, j)),
            pl.BlockSpec((d, tn), lambda j: (0, j)),
            pl.BlockSpec((1, tn), lambda j: (0, j)),
        ],
        out_specs=pl.BlockSpec((r, tn), lambda j: (0, j)),
        out_shape=jax.ShapeDtypeStruct((r, n), F32),
        compiler_params=_params(1),
        name="ada",
    )(c_all, w_hi, w_lo, b_ada)


def _rope_chunk(c, cos, s1, s2):
    return c * cos + pltpu.roll(c, LANE - ROT_HALF, 1) * s1 + pltpu.roll(c, ROT_HALF, 1) * s2


def _inproj_kernel(x_ref, sc_ref, sh_ref, nw_ref, w_ref, cos_ref, s1_ref, s2_ref, *outs, d_model, d_inner,
                   conv_dim, prompt):
    x = x_ref[...]
    h = x * lax.rsqrt(jnp.mean(x * x, axis=-1, keepdims=True) + EPS) * nw_ref[...]
    hb = (h * (1.0 + sc_ref[0]) + sh_ref[0]).astype(BF16)
    cos, s1, s2 = cos_ref[...], s1_ref[...], s2_ref[...]
    kv_w = N_KV_HEADS * HEAD_DIM
    iq_w = IDX_HEADS * IDX_DIM
    q0 = 0
    k0 = q0 + d_model
    v0 = k0 + kv_w
    iq0 = v0 + kv_w
    m0 = iq0 + iq_w
    z0 = m0 + LANE
    x0 = z0 + d_inner
    ga0 = x0 + conv_dim
    gs0 = ga0 + d_model

    def proj(lo, width):
        return jnp.dot(hb, w_ref[:, lo:lo + width], preferred_element_type=F32)

    if prompt:
        (q_ref, kt32_ref, kt_ref, vt32_ref, ve_ref, iqt_ref, ikt32_ref, ikr_ref, cwt_ref, misc_ref, z_ref, xbc_ref,
         ga_ref, gs_ref) = outs
    else:
        q_ref, k_ref, v_ref, iq_ref, misc_ref, z_ref, xbc_ref, ga_ref, gs_ref = outs

    q = proj(q0, d_model)
    q_scale = HEAD_DIM ** -0.5 * (LOG2E if prompt else 1.0)
    for c in range(d_model // LANE):
        qc = _rope_chunk(q[:, c * LANE:(c + 1) * LANE], cos, s1, s2) * q_scale
        if prompt:
            q_ref[0, 2 * c] = qc[:, :HEAD_DIM].astype(BF16)
            q_ref[0, 2 * c + 1] = qc[:, HEAD_DIM:].astype(BF16)
        else:
            q_ref[:, c * LANE:(c + 1) * LANE] = qc

    kk = proj(k0, kv_w)
    for c in range(kv_w // LANE):
        kc = _rope_chunk(kk[:, c * LANE:(c + 1) * LANE], cos, s1, s2)
        if prompt:
            kc_t = kc.T
            kt32_ref[0, c * LANE:(c + 1) * LANE, :] = kc_t
            kt_ref[0, c * LANE:(c + 1) * LANE, :] = kc_t.astype(BF16)
        else:
            k_ref[:, c * LANE:(c + 1) * LANE] = kc

    vv = proj(v0, kv_w)
    if not prompt:
        v_ref[...] = vv
    else:
        for c in range(kv_w // LANE):
            vt32_ref[0, c * LANE:(c + 1) * LANE, :] = vv[:, c * LANE:(c + 1) * LANE].T
        lane = lax.broadcasted_iota(I32, (vv.shape[0], LANE), 1)
        for g in range(N_KV_HEADS):
            vc = vv[:, (g // 2) * LANE:(g // 2 + 1) * LANE]
            if g % 2 == 1:
                vc = pltpu.roll(vc, HEAD_DIM, 1)
            ve = jnp.where(lane < HEAD_DIM, vc, jnp.where(lane == HEAD_DIM, 1.0, 0.0))
            ve_ref[0, g] = ve.astype(BF16)

    iq = proj(iq0, iq_w)
    for c in range(iq_w // LANE):
        ic = _rope_chunk(iq[:, c * LANE:(c + 1) * LANE], cos, s1, s2)
        if prompt:
            iqt_ref[0, c * LANE:(c + 1) * LANE, :] = ic.T.astype(BF16)
        else:
            iq_ref[:, c * LANE:(c + 1) * LANE] = ic.astype(BF16)

    mm = proj(m0, LANE)
    lane = lax.broadcasted_iota(I32, mm.shape, 1)
    mm = jnp.where(lane < IDX_DIM, _rope_chunk(mm, cos, s1, s2), mm)
    misc_ref[...] = mm
    if prompt:
        mm_t = mm.T
        ikt32_ref[0] = mm_t[:IDX_DIM, :]
        ikr_ref[...] = mm[:, :IDX_DIM].astype(BF16)
        cwt_ref[0] = mm_t[IW_OFF:IW_OFF + IDX_HEADS, :] * (IDX_HEADS ** -0.5) * (IDX_DIM ** -0.5)

    z_ref[...] = proj(z0, d_inner)
    xbc_ref[...] = proj(x0, conv_dim)
    ga_ref[...] = proj(ga0, d_model)
    gs_ref[...] = proj(gs0, d_model)


def _inproj(x2d, sc, sh, nw, w_all, cos, s1, s2, *, n_batch, seq, tm, prompt, d_inner, conv_dim):
    rows, d = x2d.shape
    tb = seq // tm
    r_mod = sc.shape[1]
    kv_w = N_KV_HEADS * HEAD_DIM
    iq_w = IDX_HEADS * IDX_DIM
    n_heads = d // HEAD_DIM
    if cos.shape[0] == 1:
        tab = pl.BlockSpec((1, LANE), lambda i: (0, 0))
    else:
        tab = pl.BlockSpec((tm, LANE), lambda i: (i % tb, 0))
    mod = pl.BlockSpec((1, r_mod, d), lambda i: (i // tb, 0, 0))
    in_specs = [pl.BlockSpec((tm, d), lambda i: (i, 0)), mod, mod, _const_spec((1, d)), _const_spec(w_all.shape),
                tab, tab, tab]

    def rowspec(w):
        return pl.BlockSpec((tm, w), lambda i: (i, 0))

    def sds(shape, dt):
        return jax.ShapeDtypeStruct(shape, dt)

    if prompt:
        def tspec(w):
            return pl.BlockSpec((1, w, tm), lambda i: (i // tb, 0, i % tb))

        out_specs = [
            pl.BlockSpec((1, n_heads, tm, HEAD_DIM), lambda i: (i // tb, 0, i % tb, 0)),
            tspec(kv_w), tspec(kv_w), tspec(kv_w),
            pl.BlockSpec((1, N_KV_HEADS, tm, LANE), lambda i: (i // tb, 0, i % tb, 0)),
            tspec(iq_w), tspec(IDX_DIM), rowspec(IDX_DIM), tspec(IDX_HEADS),
            rowspec(LANE), rowspec(d_inner), rowspec(conv_dim), rowspec(d), rowspec(d),
        ]
        out_shape = [
            sds((n_batch, n_heads, seq, HEAD_DIM), BF16), sds((n_batch, kv_w, seq), F32),
            sds((n_batch, kv_w, seq), BF16), sds((n_batch, kv_w, seq), F32),
            sds((n_batch, N_KV_HEADS, seq, LANE), BF16), sds((n_batch, iq_w, seq), BF16),
            sds((n_batch, IDX_DIM, seq), F32), sds((rows, IDX_DIM), BF16), sds((n_batch, IDX_HEADS, seq), F32),
            sds((rows, LANE), F32), sds((rows, d_inner), F32), sds((rows, conv_dim), F32), sds((rows, d), F32),
            sds((rows, d), F32),
        ]
    else:
        out_specs = [rowspec(d), rowspec(kv_w), rowspec(kv_w), rowspec(iq_w), rowspec(LANE), rowspec(d_inner),
                     rowspec(conv_dim), rowspec(d), rowspec(d)]
        out_shape = [sds((rows, d), F32), sds((rows, kv_w), F32), sds((rows, kv_w), F32), sds((rows, iq_w), BF16),
                     sds((rows, LANE), F32), sds((rows, d_inner), F32), sds((rows, conv_dim), F32),
                     sds((rows, d), F32), sds((rows, d), F32)]
    return pl.pallas_call(
        functools.partial(_inproj_kernel, d_model=d, d_inner=d_inner, conv_dim=conv_dim, prompt=prompt),
        grid=(rows // tm,),
        in_specs=in_specs,
        out_specs=out_specs,
        out_shape=out_shape,
        compiler_params=_params(1),
        name="inproj_prompt" if prompt else "inproj_sample",
    )(x2d, sc, sh, nw, w_all, cos, s1, s2)


def _attn_kernel(q_ref, iqt_ref, cwt_ref, kt_ref, ve_ref, ikr_ref, o_ref, sc_scr, sc16_scr, m_scr, acc_scr, *, tq,
                 topk):
    i = pl.program_id(1)
    nblk = i + 1
    tk = tq
    tkc = SOFTMAX_KEY_BLOCKS * tk
    nblk_c = (nblk + SOFTMAX_KEY_BLOCKS - 1) // SOFTMAX_KEY_BLOCKS
    hpg = q_ref.shape[1] // N_KV_HEADS
    cw = cwt_ref[0]
    kidx = lax.broadcasted_iota(I32, (tk, tq), 0)
    qidx = lax.broadcasted_iota(I32, (tk, tq), 1)

    def phase_a(j, carry):
        off = pl.multiple_of(j * tk, tk)
        ikb = ikr_ref[pl.ds(off, tk), :]
        acc = jnp.zeros((tk, tq), F32)
        for h in range(IDX_HEADS):
            x = jnp.dot(ikb, iqt_ref[0, h * IDX_DIM:(h + 1) * IDX_DIM, :], preferred_element_type=F32)
            acc = acc + jnp.maximum(x, 0.0) * cw[h:h + 1, :]
        ok = (off + kidx) <= (i * tq + qidx)
        sc = jnp.where(ok, acc, NEG_INF)
        sc_scr[pl.ds(off, tk), :] = sc
        sc16_scr[pl.ds(off, tk), :] = _floor_to_bf16(sc)
        return carry

    lax.fori_loop(0, nblk_c * SOFTMAX_KEY_BLOCKS, phase_a, 0)

    def count(cmp, thr):
        def body(j, acc):
            off = pl.multiple_of(j * tk, tk)
            for r in range(tk // COUNT_STRIP):
                sb = sc_scr[pl.ds(off + r * COUNT_STRIP, COUNT_STRIP), :]
                acc = acc + jnp.where(cmp(sb, thr), 1.0, 0.0)
            return acc

        acc = lax.fori_loop(0, nblk, body, jnp.zeros((COUNT_STRIP, tq), F32))
        return jnp.sum(acc, axis=0, keepdims=True)

    def count16(thr):
        thr16 = thr.astype(BF16)
        one, zero = jnp.ones((), BF16), jnp.zeros((), BF16)

        def body(j, acc):
            off = pl.multiple_of(j * tk, tk)
            for r in range(tk // COUNT_STRIP):
                sb = sc16_scr[pl.ds(off + r * COUNT_STRIP, COUNT_STRIP), :]
                acc = acc + jnp.where(sb >= thr16, one, zero)
            return acc

        acc = lax.fori_loop(0, nblk, body, jnp.zeros((COUNT_STRIP, tq), BF16))
        return jnp.sum(acc.astype(F32), axis=0, keepdims=True)

    tau, cnt_tau = _kth_largest([(16, count16), (32, lambda thr: count(lax.ge, thr))], (1, tq), topk,
                                (nblk * tk).astype(F32))

    any_tie = jnp.max(jnp.where(cnt_tau > float(topk), 1.0, 0.0)) > 0.5

    @pl.when(any_tie)
    def _():
        need = float(topk) - count(lax.gt, tau)
        lower = jnp.where(qidx <= kidx, 1.0, 0.0).astype(BF16)

        def body(j, seen):
            off = pl.multiple_of(j * tk, tk)
            sb = sc_scr[pl.ds(off, tk), :]
            eq = sb == tau
            prefix = jnp.dot(lower, jnp.where(eq, 1.0, 0.0).astype(BF16), preferred_element_type=F32)
            late = jnp.where(seen + prefix > need, NEG_INF, sb)
            sc_scr[pl.ds(off, tk), :] = jnp.where(eq, late, sb)
            return seen + prefix[tk - 1:tk, :]

        lax.fori_loop(0, nblk, body, jnp.zeros((1, tq), F32))

    tau_c = jnp.maximum(tau, F32_LOWEST)
    m_scr[...] = jnp.full(m_scr.shape, NEG, F32)
    acc_scr[...] = jnp.zeros(acc_scr.shape, F32)

    def phase_c(jc, carry):
        off = pl.multiple_of(jc * tkc, tkc)
        bias = jnp.concatenate(
            [jnp.where(sc_scr[pl.ds(off + b * tk, tk), :] >= tau_c, 0.0, NEG).T for b in range(SOFTMAX_KEY_BLOCKS)],
            axis=1)
        for g in range(N_KV_HEADS):
            qg = q_ref[0, g * hpg:(g + 1) * hpg].reshape(hpg * tq, HEAD_DIM)
            s = jnp.dot(qg, kt_ref[0, g * HEAD_DIM:(g + 1) * HEAD_DIM, pl.ds(off, tkc)],
                        preferred_element_type=F32)
            s = s.reshape(hpg, tq, tkc) + bias[None]
            m_prev = m_scr[g]
            m_new = jnp.maximum(m_prev, jnp.max(s, axis=-1, keepdims=True))
            alpha = jnp.exp2(m_prev - m_new)
            p = jnp.exp2(s - jnp.concatenate([m_new] * (tkc // LANE), axis=-1))
            m_scr[g] = m_new
            pv = jnp.dot(p.reshape(hpg * tq, tkc).astype(BF16), ve_ref[0, g, pl.ds(off, tkc), :],
                         preferred_element_type=F32)
            acc_scr[g] = alpha * acc_scr[g] + pv.reshape(hpg, tq, LANE)
        return carry

    lax.fori_loop(0, nblk_c, phase_c, 0)

    for g in range(N_KV_HEADS):
        for hh in range(hpg):
            a = acc_scr[g, hh]
            hq = g * hpg + hh
            o_ref[:, hq * HEAD_DIM:(hq + 1) * HEAD_DIM] = a[:, :HEAD_DIM] / a[:, HEAD_DIM:HEAD_DIM + 1]


def _prompt_attention(q_hm, iqt, cwt, kt, ve, ikr, *, tq, topk):
    nb, nh, seq, _ = q_hm.shape
    hpg = nh // N_KV_HEADS
    tb = seq // tq
    kv_w = N_KV_HEADS * HEAD_DIM
    assert (seq // tq) % SOFTMAX_KEY_BLOCKS == 0

    def resident(shape, imap):
        return pl.BlockSpec(shape, imap, pipeline_mode=pl.Buffered(1))

    return pl.pallas_call(
        functools.partial(_attn_kernel, tq=tq, topk=topk),
        grid=(nb, tb),
        in_specs=[
            pl.BlockSpec((1, nh, tq, HEAD_DIM), lambda b, i: (b, 0, i, 0)),
            pl.BlockSpec((1, IDX_HEADS * IDX_DIM, tq), lambda b, i: (b, 0, i)),
            pl.BlockSpec((1, IDX_HEADS, tq), lambda b, i: (b, 0, i)),
            resident((1, kv_w, seq), lambda b, i: (b, 0, 0)),
            resident((1, N_KV_HEADS, seq, LANE), lambda b, i: (b, 0, 0, 0)),
            resident((seq, IDX_DIM), lambda b, i: (b, 0)),
        ],
        out_specs=pl.BlockSpec((tq, nh * HEAD_DIM), lambda b, i: (b * tb + i, 0)),
        out_shape=jax.ShapeDtypeStruct((nb * seq, nh * HEAD_DIM), F32),
        scratch_shapes=[
            pltpu.VMEM((seq, tq), F32),
            pltpu.VMEM((seq, tq), BF16),
            pltpu.VMEM((N_KV_HEADS, hpg, tq, LANE), F32),
            pltpu.VMEM((N_KV_HEADS, hpg, tq, LANE), F32),
        ],
        compiler_params=_params(2),
        name="prompt_attention",
    )(q_hm, iqt, cwt, kt, ve, ikr)


def _gated_group_norm(y, z, nw, d_inner):
    g = y * _silu(z)
    gw = d_inner // SSM_GROUPS
    outs = []
    for k in range(SSM_GROUPS):
        gg = g[:, k * gw:(k + 1) * gw]
        outs.append(gg * lax.rsqrt(jnp.mean(gg * gg, axis=-1, keepdims=True) + EPS))
    return jnp.concatenate(outs, axis=-1) * nw


def _ssd_kernel(xbc_ref, z_ref, misc_ref, cw_ref, cb_ref, dtb_ref, alog_ref, e_ref, dsk_ref, nw_ref, y_ref, st_ref,
                ext_scr, st_scr, y_scr, *, d_inner, n_heads):
    c = pl.program_id(1)
    q = xbc_ref.shape[0]
    gn = SSM_GROUPS * D_STATE
    hpg = n_heads // SSM_GROUPS
    gw = hpg * SSM_HEAD_DIM

    @pl.when(c == 0)
    def _():
        ext_scr[0:SUBLANE, :] = jnp.zeros((SUBLANE, ext_scr.shape[1]), F32)
        st_scr[...] = jnp.zeros(st_scr.shape, F32)

    u = xbc_ref[...]
    ext_scr[SUBLANE:, :] = u
    conv = cb_ref[...] + cw_ref[CONV_W - 1:CONV_W, :] * u
    for w in range(CONV_W - 1):
        lo = SUBLANE - (CONV_W - 1) + w
        conv = conv + cw_ref[w:w + 1, :] * ext_scr[lo:lo + q, :]
    ext_scr[0:SUBLANE, :] = u[q - SUBLANE:, :]
    act = _silu(conv)
    xs = act[:, :d_inner]
    bm = act[:, d_inner:d_inner + gn]
    cm = act[:, d_inner + gn:]

    dt_t = _softplus(misc_ref[...].T + dtb_ref[...])
    da_t = dt_t * (-jnp.exp(alog_ref[...]))
    ri = lax.broadcasted_iota(I32, (q, q), 0)
    ci = lax.broadcasted_iota(I32, (q, q), 1)
    cs_t = _split3_dot(da_t, jnp.where(ri <= ci, 1.0, 0.0).astype(BF16))
    cs = cs_t.T
    dt = dt_t.T
    e = e_ref[...]
    ecs_x = _split3_dot(jnp.exp(cs), e)
    wst_x = _split3_dot(dt * jnp.exp(cs[q - 1:q, :] - cs), e)
    tri = ri >= ci

    for g in range(SSM_GROUPS):
        bg = bm[:, g * D_STATE:(g + 1) * D_STATE]
        cg = cm[:, g * D_STATE:(g + 1) * D_STATE].astype(BF16)
        bg_t = bg.T.astype(BF16)
        cb = jnp.dot(cg, bg_t, preferred_element_type=F32)
        s_t = st_scr[g]
        y_off = jnp.dot(cg, s_t.astype(BF16), preferred_element_type=F32)
        xg = xs[:, g * gw:(g + 1) * gw]
        y_g = y_off * ecs_x[:, g * gw:(g + 1) * gw]
        pieces = []
        for hh in range(hpg):
            idx = DT_OFF + g * hpg + hh
            seg = cs[:, idx:idx + 1] - cs_t[idx:idx + 1, :]
            mm = cb * jnp.exp(jnp.where(tri, seg, NEG)) * dt_t[idx:idx + 1, :]
            pieces.append(_bdot(mm, xg[:, hh * SSM_HEAD_DIM:(hh + 1) * SSM_HEAD_DIM]))
        y_scr[:, g * gw:(g + 1) * gw] = y_g + jnp.concatenate(pieces, axis=-1)
        wg = (xg * wst_x[:, g * gw:(g + 1) * gw]).astype(BF16)
        st_scr[g] = s_t * ecs_x[q - 1:q, g * gw:(g + 1) * gw] + jnp.dot(bg_t, wg, preferred_element_type=F32)

    y = y_scr[...] + dsk_ref[...] * xs
    y_ref[...] = _gated_group_norm(y, z_ref[...], nw_ref[...], d_inner)

    @pl.when(c == pl.num_programs(1) - 1)
    def _():
        for g in range(SSM_GROUPS):
            st_ref[0, g * hpg:(g + 1) * hpg] = st_scr[g].T.reshape(hpg, SSM_HEAD_DIM, D_STATE)


def _prompt_ssd(xbc, z, misc, conv_w, conv_b, dtb_col, alog_col, e_mat, dsk_row, nw_row, *, n_batch, seq, d_inner,
                n_heads):
    q = SSD_CHUNK
    nc = seq // q
    cd = xbc.shape[1]
    hpg = n_heads // SSM_GROUPS
    return pl.pallas_call(
        functools.partial(_ssd_kernel, d_inner=d_inner, n_heads=n_heads),
        grid=(n_batch, nc),
        in_specs=[
            pl.BlockSpec((q, cd), lambda b, c: (b * nc + c, 0)),
            pl.BlockSpec((q, d_inner), lambda b, c: (b * nc + c, 0)),
            pl.BlockSpec((q, LANE), lambda b, c: (b * nc + c, 0)),
            _const_spec(conv_w.shape), _const_spec(conv_b.shape), _const_spec(dtb_col.shape),
            _const_spec(alog_col.shape), _const_spec(e_mat.shape), _const_spec(dsk_row.shape),
            _const_spec(nw_row.shape),
        ],
        out_specs=[
            pl.BlockSpec((q, d_inner), lambda b, c: (b * nc + c, 0)),
            pl.BlockSpec((1, n_heads, SSM_HEAD_DIM, D_STATE), lambda b, c: (b, 0, 0, 0)),
        ],
        out_shape=[
            jax.ShapeDtypeStruct((n_batch * seq, d_inner), F32),
            jax.ShapeDtypeStruct((n_batch, n_heads, SSM_HEAD_DIM, D_STATE), F32),
        ],
        scratch_shapes=[
            pltpu.VMEM((q + SUBLANE, cd), F32),
            pltpu.VMEM((SSM_GROUPS, D_STATE, hpg * SSM_HEAD_DIM), F32),
            pltpu.VMEM((q, d_inner), F32),
        ],
        compiler_params=_params(2),
        name="prompt_ssd",
    )(xbc, z, misc, conv_w, conv_b, dtb_col, alog_col, e_mat, dsk_row, nw_row)


def _mlp_kernel(x_ref, att_ref, y_ref, ga_ref, gs_ref, gt1_ref, sc2_ref, sh2_ref, gt2_ref, n2_ref, fn_ref, wo_ref,
                wu_ref, wd_ref, o_ref, *, ff_chunk):
    merged = _sigmoid(ga_ref[...]) * att_ref[...] + _sigmoid(gs_ref[...]) * y_ref[...]
    x1 = x_ref[...] + gt1_ref[0] * jnp.dot(merged.astype(BF16), wo_ref[...], preferred_element_type=F32)
    h2 = x1 * lax.rsqrt(jnp.mean(x1 * x1, axis=-1, keepdims=True) + EPS) * n2_ref[...]
    hb = (h2 * (1.0 + sc2_ref[0]) + sh2_ref[0]).astype(BF16)
    acc = jnp.zeros(x1.shape, F32)
    for c in range(wu_ref.shape[1] // ff_chunk):
        u = jnp.maximum(jnp.dot(hb, wu_ref[:, c * ff_chunk:(c + 1) * ff_chunk], preferred_element_type=F32), 0.0)
        acc = acc + jnp.dot((u * u).astype(BF16), wd_ref[c * ff_chunk:(c + 1) * ff_chunk, :],
                            preferred_element_type=F32)
    x2 = x1 + gt2_ref[0] * acc
    o_ref[...] = x2 * lax.rsqrt(jnp.mean(x2 * x2, axis=-1, keepdims=True) + EPS) * fn_ref[...]


def _mlp(x2d, att, y, ga, gs, gt1, sc2, sh2, gt2, n2, fn, wo, wu, wd, *, seq, tm, ff_chunk=1024):
    rows, d = x2d.shape
    tb = seq // tm
    r_mod = gt1.shape[1]
    row = pl.BlockSpec((tm, d), lambda i: (i, 0))
    mod = pl.BlockSpec((1, r_mod, d), lambda i: (i // tb, 0, 0))
    return pl.pallas_call(
        functools.partial(_mlp_kernel, ff_chunk=ff_chunk),
        grid=(rows // tm,),
        in_specs=[row, row, row, row, row, mod, mod, mod, mod, _const_spec((1, d)), _const_spec((1, d)),
                  _const_spec(wo.shape), _const_spec(wu.shape), _const_spec(wd.shape)],
        out_specs=row,
        out_shape=jax.ShapeDtypeStruct((rows, d), F32),
        compiler_params=_params(1),
        name="merge_mlp",
    )(x2d, att, y, ga, gs, gt1, sc2, sh2, gt2, n2, fn, wo, wu, wd)


def _sidx_kernel(pt_ref, iq_ref, cw_ref, iknew_ref, *rest, pages_per_step, n_steps):
    del pt_ref
    page_refs, o_ref, ik_scr = rest[:pages_per_step], rest[pages_per_step], rest[pages_per_step + 1]
    r = pl.program_id(1)
    iq = iq_ref[0]
    cw = cw_ref[0]

    def score(ik_t):
        x = _bdot(iq, ik_t)
        return jnp.sum(jnp.maximum(x, 0.0) * cw, axis=0, keepdims=True)

    @pl.when(r < n_steps)
    def _():
        for m in range(pages_per_step):
            ik_scr[:, m * LANE:(m + 1) * LANE] = page_refs[m][...].astype(BF16)
        o_ref[0] = score(ik_scr[...])

    @pl.when(r == n_steps)
    def _():
        s_new = score(iknew_ref[0])[:, 0:1]
        lane = lax.broadcasted_iota(I32, (1, o_ref.shape[2]), 1)
        o_ref[0] = jnp.where(lane == 0, jnp.broadcast_to(s_new, lane.shape), NEG_INF)


def _sample_index_scores(page_table, iq3, cw3, iknew_t, idx_cache_t, *, pages_per_step):
    db, n_pages = page_table.shape
    page = idx_cache_t.shape[3]
    n_steps = n_pages // pages_per_step
    width = pages_per_step * page

    def page_spec(m):
        return pl.BlockSpec(
            (None, None, IDX_DIM, page),
            lambda b, r, pt: (0, pt[b, jnp.minimum(r, n_steps - 1) * pages_per_step + m], 0, 0))

    grid_spec = pltpu.PrefetchScalarGridSpec(
        num_scalar_prefetch=1,
        grid=(db, n_steps + 1),
        in_specs=[
            pl.BlockSpec((1, IDX_HEADS, IDX_DIM), lambda b, r, pt: (b, 0, 0)),
            pl.BlockSpec((1, IDX_HEADS, 1), lambda b, r, pt: (b, 0, 0)),
            pl.BlockSpec((1, IDX_DIM, page), lambda b, r, pt: (b, 0, 0)),
        ] + [page_spec(m) for m in range(pages_per_step)],
        out_specs=pl.BlockSpec((1, 1, width), lambda b, r, pt: (b, 0, r)),
        scratch_shapes=[pltpu.VMEM((IDX_DIM, width), BF16)],
    )
    return pl.pallas_call(
        functools.partial(_sidx_kernel, pages_per_step=pages_per_step, n_steps=n_steps),
        grid_spec=grid_spec,
        out_shape=jax.ShapeDtypeStruct((db, 1, (n_steps + 1) * width), F32),
        compiler_params=_params(2),
        name="sample_index_scores",
    )(page_table, iq3, cw3, iknew_t, *([idx_cache_t] * pages_per_step))


def _ssel_kernel(sc_ref, bias_ref, *, topk):
    sc = sc_ref[...]
    db, lk = sc.shape

    def count_ge(thr):
        return jnp.sum(jnp.where(sc >= thr, 1.0, 0.0), axis=1, keepdims=True)

    tau, _ = _kth_largest([(32, count_ge)], (db, 1), topk, jnp.float32(lk))
    need = float(topk) - jnp.sum(jnp.where(sc > tau, 1.0, 0.0), axis=1, keepdims=True)
    ri = lax.broadcasted_iota(I32, (LANE, LANE), 0)
    ci = lax.broadcasted_iota(I32, (LANE, LANE), 1)
    upper = jnp.where(ri <= ci, 1.0, 0.0).astype(BF16)

    def body(j, seen):
        off = pl.multiple_of(j * LANE, LANE)
        sb = sc_ref[:, pl.ds(off, LANE)]
        eq = sb == tau
        prefix = jnp.dot(jnp.where(eq, 1.0, 0.0).astype(BF16), upper, preferred_element_type=F32)
        keep_eq = jnp.where(seen + prefix <= need, 0.0, NEG)
        sel = jnp.where(sb > tau, 0.0, jnp.where(eq, keep_eq, NEG))
        bias_ref[:, pl.ds(off, LANE)] = jnp.where(sb == NEG_INF, NEG, sel)
        return seen + prefix[:, LANE - 1:LANE]

    lax.fori_loop(0, lk // LANE, body, jnp.zeros((db, 1), F32))


def _sample_select(keys2d, *, topk):
    return pl.pallas_call(
        functools.partial(_ssel_kernel, topk=topk),
        out_shape=jax.ShapeDtypeStruct(keys2d.shape, F32),
        compiler_params=pltpu.CompilerParams(vmem_limit_bytes=VMEM_LIMIT),
        name="sample_select",
    )(keys2d)


def _sattn_kernel(pt_ref, q_ref, bias_ref, knew_ref, vnew_ref, *rest, pages_per_step, n_steps):
    del pt_ref
    k_refs = rest[:pages_per_step]
    v_refs = rest[pages_per_step:2 * pages_per_step]
    o_ref, m_scr, l_scr, acc_scr, k_scr, v_scr = rest[2 * pages_per_step:]
    r = pl.program_id(1)
    q = q_ref[0]
    bias = bias_ref[0]

    @pl.when(r == 0)
    def _():
        m_scr[...] = jnp.full(m_scr.shape, NEG, F32)
        l_scr[...] = jnp.zeros(l_scr.shape, F32)
        acc_scr[...] = jnp.zeros(acc_scr.shape, F32)

    def update(s, pv_fn):
        m_prev = m_scr[...]
        m_new = jnp.maximum(m_prev, jnp.max(s, axis=1, keepdims=True))
        alpha = jnp.exp(m_prev - m_new)
        p = jnp.exp(s - m_new[:, 0:1])
        m_scr[...] = m_new
        l_scr[...] = alpha * l_scr[...] + jnp.sum(p, axis=1, keepdims=True)
        acc_scr[...] = alpha[:, 0:1] * acc_scr[...] + pv_fn(p)

    @pl.when(r < n_steps)
    def _():
        for m in range(pages_per_step):
            k_scr[:, m * LANE:(m + 1) * LANE] = k_refs[m][...].astype(BF16)
            v_scr[:, m * LANE:(m + 1) * LANE] = v_refs[m][...].astype(BF16)
        s = jnp.dot(q, k_scr[...], preferred_element_type=F32) + bias
        update(s, lambda p: _bdot_nt(p, v_scr[...]))

    @pl.when(r == n_steps)
    def _():
        s = _bdot(q, knew_ref[0]) + bias[:, 0:LANE]
        update(s, lambda p: _bdot_nt(p, vnew_ref[0]))
        o_ref[0] = acc_scr[...] / l_scr[:, 0:1]


def _sample_attention(page_table, qmat, bias3, knew_t, vnew_t, cache_kt, cache_vt, *, pages_per_step):
    db, n_pages = page_table.shape
    kv_w, page = cache_kt.shape[2], cache_kt.shape[3]
    nh = qmat.shape[1]
    n_steps = n_pages // pages_per_step
    width = pages_per_step * page

    def page_spec(m):
        return pl.BlockSpec(
            (None, None, kv_w, page),
            lambda b, r, pt: (0, pt[b, jnp.minimum(r, n_steps - 1) * pages_per_step + m], 0, 0))

    pages = [page_spec(m) for m in range(pages_per_step)]
    grid_spec = pltpu.PrefetchScalarGridSpec(
        num_scalar_prefetch=1,
        grid=(db, n_steps + 1),
        in_specs=[
            pl.BlockSpec((1, nh, kv_w), lambda b, r, pt: (b, 0, 0)),
            pl.BlockSpec((1, 1, width), lambda b, r, pt: (b, 0, r)),
            pl.BlockSpec((1, kv_w, page), lambda b, r, pt: (b, 0, 0)),
            pl.BlockSpec((1, kv_w, page), lambda b, r, pt: (b, 0, 0)),
        ] + pages + pages,
        out_specs=pl.BlockSpec((1, nh, kv_w), lambda b, r, pt: (b, 0, 0)),
        scratch_shapes=[pltpu.VMEM((nh, LANE), F32), pltpu.VMEM((nh, LANE), F32), pltpu.VMEM((nh, kv_w), F32),
                        pltpu.VMEM((kv_w, width), BF16), pltpu.VMEM((kv_w, width), BF16)],
    )
    return pl.pallas_call(
        functools.partial(_sattn_kernel, pages_per_step=pages_per_step, n_steps=n_steps),
        grid_spec=grid_spec,
        out_shape=jax.ShapeDtypeStruct((db, nh, kv_w), F32),
        compiler_params=_params(2),
        name="sample_attention",
    )(page_table, qmat, bias3, knew_t, vnew_t, *([cache_kt] * pages_per_step), *([cache_vt] * pages_per_step))


def _sssd_kernel(xbc_ref, prev_ref, misc_ref, z_ref, st_ref, cw_ref, cb_ref, dtb_ref, alog_ref, e_ref, dsk_ref,
                 nw_ref, y_ref, sto_ref, *, d_inner, n_heads):
    gn = SSM_GROUPS * D_STATE
    hpg = n_heads // SSM_GROUPS
    gw = hpg * SSM_HEAD_DIM
    u = xbc_ref[0]
    prev = prev_ref[0]
    conv = cb_ref[...] + cw_ref[CONV_W - 1:CONV_W, :] * u
    for w in range(CONV_W - 1):
        conv = conv + cw_ref[w:w + 1, :] * prev[w:w + 1, :]
    act = _silu(conv)
    xs = act[:, :d_inner]
    bm = act[:, d_inner:d_inner + gn]
    cm = act[:, d_inner + gn:]
    dt = _softplus(misc_ref[0] + dtb_ref[...])
    dec = jnp.exp(dt * (-jnp.exp(alog_ref[...])))
    pad6 = jnp.zeros((SUBLANE - 2, LANE), F32)
    ex = _split3_dot(jnp.concatenate([dt, dec, pad6], axis=0), e_ref[...])
    rows = jnp.concatenate([xs * ex[0:1, :], ex[1:2, :], jnp.zeros((LANE - 2, d_inner), F32)], axis=0)
    cols = rows.T
    pad15 = jnp.zeros((2 * SUBLANE - 1, D_STATE), F32)
    ys = []
    for g in range(SSM_GROUPS):
        s0 = st_ref[0, g * hpg:(g + 1) * hpg].reshape(gw, D_STATE)
        cg = cols[g * gw:(g + 1) * gw, :]
        s1 = s0 * cg[:, 1:2] + cg[:, 0:1] * bm[:, g * D_STATE:(g + 1) * D_STATE]
        sto_ref[0, g * hpg:(g + 1) * hpg] = s1.reshape(hpg, SSM_HEAD_DIM, D_STATE)
        c16 = jnp.concatenate([cm[:, g * D_STATE:(g + 1) * D_STATE], pad15], axis=0)
        ys.append(_bdot_nt(c16, s1)[0:1, :])
    y = jnp.concatenate(ys, axis=-1) + dsk_ref[...] * xs
    y_ref[0] = _gated_group_norm(y, z_ref[0], nw_ref[...], d_inner)


def _sample_ssd(xbc3, prev, misc3, z3, state, conv_w, conv_b, dtb_row, alog_row, e_mat, dsk_row, nw_row, *,
                d_inner, n_heads):
    db = xbc3.shape[0]
    cd = xbc3.shape[2]

    def per(shape):
        n = len(shape)
        return pl.BlockSpec((1,) + tuple(shape[1:]), lambda b: (b,) + (0,) * (n - 1))

    return pl.pallas_call(
        functools.partial(_sssd_kernel, d_inner=d_inner, n_heads=n_heads),
        grid=(db,),
        in_specs=[per(xbc3.shape), per(prev.shape), per(misc3.shape), per(z3.shape), per(state.shape),
                  _const_spec(conv_w.shape), _const_spec(conv_b.shape), _const_spec(dtb_row.shape),
                  _const_spec(alog_row.shape), _const_spec(e_mat.shape), _const_spec(dsk_row.shape),
                  _const_spec(nw_row.shape)],
        out_specs=[per((db, 1, d_inner)), per(state.shape)],
        out_shape=[jax.ShapeDtypeStruct((db, 1, d_inner), F32), jax.ShapeDtypeStruct(state.shape, F32)],
        compiler_params=_params(1),
        name="sample_ssd",
    )(xbc3, prev, misc3, z3, state, conv_w, conv_b, dtb_row, alog_row, e_mat, dsk_row, nw_row)


def _rope_tables(pos):
    inv = ROPE_THETA ** (-jnp.arange(ROT_HALF, dtype=F32) * 2.0 / (2 * ROT_HALF))
    ang = pos.astype(F32)[:, None] * inv[None, :]
    cos, sin = jnp.cos(ang), jnp.sin(ang)
    n = pos.shape[0]
    one = jnp.ones((n, HEAD_DIM - 2 * ROT_HALF), F32)
    zero = jnp.zeros((n, HEAD_DIM - 2 * ROT_HALF), F32)
    z8 = jnp.zeros((n, ROT_HALF), F32)
    c_head = jnp.concatenate([cos, cos, one], axis=1)
    s1_head = jnp.concatenate([-sin, z8, zero], axis=1)
    s2_head = jnp.concatenate([z8, sin, zero], axis=1)
    rep = LANE // HEAD_DIM
    return jnp.tile(c_head, (1, rep)), jnp.tile(s1_head, (1, rep)), jnp.tile(s2_head, (1, rep))


def kernel(x_prompt, x_sample, cache_k, cache_v, cache_idx_k, state_conv, state_ssm, page_table, c_prompt, c_sample,
           w_ada, b_ada, norm1_w, w_in, conv_w, conv_b, dt_bias, a_log, d_skip, ssm_norm_w, w_out, norm2_w, w_up,
           w_down, final_norm_w):
    nb, seq, d = x_prompt.shape
    db, dec_seq, _ = x_sample.shape
    depth = w_in.shape[0]
    assert depth == 1 and dec_seq == 1 and seq % SSD_CHUNK == 0
    n_heads = d // HEAD_DIM
    n_ssm_heads = a_log.shape[1]
    d_inner = n_ssm_heads * SSM_HEAD_DIM
    conv_dim = conv_w.shape[2]
    kv_w = N_KV_HEADS * HEAD_DIM
    iq_w = IDX_HEADS * IDX_DIM
    n_pages, page = page_table.shape[1], cache_k.shape[2]
    past = n_pages * page

    wi = w_in[0]
    offs = np.cumsum([0, d, kv_w, kv_w, iq_w, IDX_DIM, IDX_HEADS, d_inner, conv_dim, n_ssm_heads, d, d])
    seg = [wi[:, offs[k]:offs[k + 1]] for k in range(11)]
    misc_w = jnp.concatenate(
        [seg[4], seg[5], seg[8], jnp.zeros((d, LANE - IDX_DIM - IDX_HEADS - n_ssm_heads), F32)], axis=1)
    w_all = jnp.concatenate([seg[0], seg[1], seg[2], seg[3], misc_w, seg[6], seg[7], seg[9], seg[10]],
                            axis=1).astype(BF16)
    wo, wu, wd = w_out[0].astype(BF16), w_up[0].astype(BF16), w_down[0].astype(BF16)
    nw1, nw2, fnw = norm1_w[0][None, :], norm2_w[0][None, :], final_norm_w[None, :]
    head_lane = jnp.zeros((LANE,), F32)
    dtb_row = head_lane.at[DT_OFF:DT_OFF + n_ssm_heads].set(dt_bias[0])[None, :]
    alog_row = head_lane.at[DT_OFF:DT_OFF + n_ssm_heads].set(a_log[0])[None, :]
    e_mat = jnp.zeros((LANE, d_inner), F32).at[DT_OFF:DT_OFF + n_ssm_heads].set(
        jnp.repeat(jnp.eye(n_ssm_heads, dtype=F32), SSM_HEAD_DIM, axis=1)).astype(BF16)
    dsk_row = jnp.repeat(d_skip[0], SSM_HEAD_DIM)[None, :]
    snw_row = ssm_norm_w[0][None, :]
    cw2, cb2 = conv_w[0], conv_b[0][None, :]

    n_mod = nb + db
    r_mod = -(-n_mod // SUBLANE) * SUBLANE
    c_all = jnp.concatenate([c_prompt, c_sample, jnp.zeros((r_mod - n_mod, d), F32)], axis=0)
    wa_hi = w_ada[0].astype(BF16)
    wa_lo = (w_ada[0] - wa_hi.astype(F32)).astype(BF16)
    mod = _ada(c_all, wa_hi, wa_lo, b_ada[0][None, :])
    sh1, sc1, gt1, sh2, sc2, gt2 = [mod[:, k * d:(k + 1) * d] for k in range(6)]

    def pmod(a):
        return a[:nb].reshape(nb, 1, d)

    def smod(a):
        return a[nb:nb + db].reshape(1, db, d)

    xp = x_prompt.reshape(nb * seq, d)
    cos_p, s1_p, s2_p = _rope_tables(jnp.arange(seq))
    (q_hm, kt32, kt, vt32, ve, iqt, ikt32, ikr, cwt, misc_p, z_p, xbc_p, ga_p, gs_p) = _inproj(
        xp, pmod(sc1), pmod(sh1), nw1, w_all, cos_p, s1_p, s2_p, n_batch=nb, seq=seq, tm=256, prompt=True,
        d_inner=d_inner, conv_dim=conv_dim)
    topk_p = min(TOPK_MAX, seq // 4)
    att_p = _prompt_attention(q_hm, iqt, cwt, kt, ve, ikr, tq=256, topk=topk_p)
    y_p, ssm_p = _prompt_ssd(xbc_p, z_p, misc_p, cw2, cb2, dtb_row.T, alog_row.T, e_mat, dsk_row, snw_row,
                             n_batch=nb, seq=seq, d_inner=d_inner, n_heads=n_ssm_heads)
    out_p = _mlp(xp, att_p, y_p, ga_p, gs_p, pmod(gt1), pmod(sc2), pmod(sh2), pmod(gt2), nw2, fnw, wo, wu, wd,
                 seq=seq, tm=256)

    xs2 = x_sample.reshape(db, d)
    cos_s, s1_s, s2_s = _rope_tables(jnp.full((1,), past, jnp.int32))
    (q_s, k_s, v_s, iq_s, misc_s, z_s, xbc_s, ga_s, gs_s) = _inproj(
        xs2, smod(sc1), smod(sh1), nw1, w_all, cos_s, s1_s, s2_s, n_batch=1, seq=db, tm=db, prompt=False,
        d_inner=d_inner, conv_dim=conv_dim)
    pps_idx = _largest_divisor(n_pages, IDX_PAGES_PER_STEP)
    pps_kv = _largest_divisor(n_pages, KV_PAGES_PER_STEP)
    cw_s = (misc_s[:, IW_OFF:IW_OFF + IDX_HEADS] * (IDX_HEADS ** -0.5) * (IDX_DIM ** -0.5)).reshape(db, IDX_HEADS, 1)

    def as_page_t(a):
        return jnp.pad(a[:, :, None], ((0, 0), (0, 0), (0, page - 1)))

    idx_cache_t = jnp.transpose(cache_idx_k, (0, 1, 3, 2))
    kv_shape_t = (depth, cache_k.shape[1], kv_w, page)
    cache_kt = jnp.transpose(cache_k, (0, 1, 3, 4, 2)).reshape(kv_shape_t)
    cache_vt = jnp.transpose(cache_v, (0, 1, 3, 4, 2)).reshape(kv_shape_t)

    sc_s = _sample_index_scores(page_table, iq_s.reshape(db, IDX_HEADS, IDX_DIM), cw_s,
                                as_page_t(misc_s[:, :IDX_DIM]), idx_cache_t, pages_per_step=pps_idx)
    topk_s = min(TOPK_MAX, (past + 1) // 4)
    lk = past + pps_kv * page
    assert pps_idx >= pps_kv
    bias_s = _sample_select(sc_s[:, 0, :lk], topk=topk_s).reshape(db, 1, lk)
    hpg = n_heads // N_KV_HEADS
    q5 = q_s.reshape(db, N_KV_HEADS, hpg, 1, HEAD_DIM)
    eye = jnp.eye(N_KV_HEADS, dtype=F32)[None, :, None, :, None]
    qmat = (q5 * eye).reshape(db, n_heads, kv_w).astype(BF16)
    acc_s = _sample_attention(page_table, qmat, bias_s, as_page_t(k_s), as_page_t(v_s), cache_kt, cache_vt,
                              pages_per_step=pps_kv)
    a5 = acc_s.reshape(db, N_KV_HEADS, hpg, N_KV_HEADS, HEAD_DIM)
    att_s = jnp.einsum('bghgd->bghd', a5).reshape(db, d)
    y_s, ssm_s = _sample_ssd(xbc_s.reshape(db, 1, conv_dim), state_conv[0], misc_s.reshape(db, 1, LANE),
                             z_s.reshape(db, 1, d_inner), state_ssm[0], cw2, cb2, dtb_row, alog_row, e_mat, dsk_row,
                             snw_row, d_inner=d_inner, n_heads=n_ssm_heads)
    out_s = _mlp(xs2, att_s, y_s.reshape(db, d_inner), ga_s, gs_s, smod(gt1), smod(sc2), smod(sh2), smod(gt2), nw2,
                 fnw, wo, wu, wd, seq=db, tm=db)

    conv_p = xbc_p.reshape(nb, seq, conv_dim)[:, seq - (CONV_W - 1):, :]
    conv_s = jnp.concatenate([state_conv[0][:, 1:, :], xbc_s[:, None, :]], axis=1)
    def kv_out(a_t):
        return jnp.transpose(a_t.reshape(nb, N_KV_HEADS, HEAD_DIM, seq), (0, 3, 1, 2))[None]

    return (
        out_p.reshape(nb, seq, d),
        out_s.reshape(db, 1, d),
        kv_out(kt32),
        kv_out(vt32),
        jnp.transpose(ikt32, (0, 2, 1))[None],
        conv_p[None],
        ssm_p[None],
        k_s.reshape(1, db, 1, N_KV_HEADS, HEAD_DIM),
        v_s.reshape(1, db, 1, N_KV_HEADS, HEAD_DIM),
        misc_s[:, :IDX_DIM].reshape(1, db, 1, IDX_DIM),
        conv_s[None],
        ssm_s[None],
    )
```

```python
import functools

import jax
import jax.numpy as jnp
import numpy as np
from jax import lax
from jax.experimental import pallas as pl
from jax.experimental.pallas import tpu as pltpu

F32 = jnp.float32
BF16 = jnp.bfloat16
I32 = jnp.int32

HEAD_DIM = 64
N_KV_HEADS = 4
ROT_HALF = 8
ROPE_THETA = 500000.0
IDX_HEADS = 8
IDX_DIM = 64
TOPK_MAX = 256
SSM_HEAD_DIM = 64
SSM_GROUPS = 4
D_STATE = 128
CONV_W = 4
SSD_CHUNK = 128
EPS = 1e-6

LANE = 128
SUBLANE = 8
VMEM_LIMIT = 56 * 1024 * 1024
IDX_PAGES_PER_STEP = 32
KV_PAGES_PER_STEP = 16
COUNT_STRIP = 64
SOFTMAX_KEY_BLOCKS = 2
EARLY_EXIT_FROM_BIT = 24

IW_OFF = IDX_DIM
DT_OFF = IDX_DIM + IDX_HEADS
INT_MIN = -(2 ** 31)
NEG = -1e30
NEG_INF = float("-inf")
F32_LOWEST = float(np.finfo(np.float32).min)
LOG2E = 1.4426950408889634
BF16_STEP_DOWN = 1.25 * 2.0 ** -8


def _sigmoid(x):
    return 1.0 / (1.0 + jnp.exp(-x))


def _silu(x):
    return x * _sigmoid(x)


def _softplus(x):
    return jnp.maximum(x, 0.0) + jnp.log1p(jnp.exp(-jnp.abs(x)))


def _bdot(a, b):
    return jnp.dot(a.astype(BF16), b.astype(BF16), preferred_element_type=F32)


def _bdot_nt(a, b):
    return lax.dot_general(a.astype(BF16), b.astype(BF16), (((1,), (1,)), ((), ())), preferred_element_type=F32)


def _split3_dot(v, e):
    hi = v.astype(BF16)
    r1 = v - hi.astype(F32)
    mid = r1.astype(BF16)
    lo = (r1 - mid.astype(F32)).astype(BF16)
    d = functools.partial(jnp.dot, preferred_element_type=F32)
    return d(hi, e) + d(mid, e) + d(lo, e)


def _f32_at_rank(u):
    key = u ^ jnp.int32(INT_MIN)
    bits = jnp.where(key < 0, jnp.int32(INT_MIN) - key, key)
    return lax.bitcast_convert_type(bits, F32)


def _floor_to_bf16(x):
    r = x.astype(BF16).astype(F32)
    below = r - jnp.abs(r) * BF16_STEP_DOWN
    return jnp.where(r > x, below, r).astype(BF16)


def _kth_largest(stages, shape, topk, n_total):
    def step(t, tau_u, cnt_tau, count_ge):
        cand_u = tau_u | jnp.left_shift(jnp.int32(1), 31 - t)
        cnt = count_ge(_f32_at_rank(cand_u))
        take = cnt >= float(topk)
        return jnp.where(take, cand_u, tau_u), jnp.where(take, cnt, cnt_tau)

    t0 = 0
    tau_u, cnt_tau = jnp.zeros(shape, I32), jnp.zeros(shape, F32) + n_total
    for end_bit, count_ge, early_exit in stages:
        if early_exit:
            def cond(c, end_bit=end_bit):
                return jnp.logical_and(c[0] < end_bit, c[3] > 0)

            def body(c, count_ge=count_ge):
                tau_n, cnt_n = step(c[0], c[1], c[2], count_ge)
                return c[0] + 1, tau_n, cnt_n, (jnp.max(cnt_n) > float(topk)).astype(I32)

            init = (jnp.int32(t0), tau_u, cnt_tau, (jnp.max(cnt_tau) > float(topk)).astype(I32))
            _, tau_u, cnt_tau, _ = lax.while_loop(cond, body, init)
        else:
            tau_u, cnt_tau = lax.fori_loop(
                t0, end_bit, lambda t, c, count_ge=count_ge: step(t, c[0], c[1], count_ge), (tau_u, cnt_tau))
        t0 = end_bit
    return jnp.where(tau_u == 0, NEG_INF, _f32_at_rank(tau_u)), cnt_tau


def _largest_divisor(n, cap):
    return max(k for k in range(1, cap + 1) if n % k == 0)


def _const_spec(shape):
    n = len(shape)
    return pl.BlockSpec(shape, lambda *a: (0,) * n, pipeline_mode=pl.Buffered(1))


def _params(n_axes):
    return pltpu.CompilerParams(dimension_semantics=("arbitrary",) * n_axes, vmem_limit_bytes=VMEM_LIMIT)


def _ada_kernel(c_ref, w_ref, b_ref, o_ref):
    s = _silu(c_ref[...])
    s_hi = s.astype(BF16)
    s_lo = (s - s_hi.astype(F32)).astype(BF16)
    w = w_ref[...]
    w_hi = w.astype(BF16)
    w_lo = (w - w_hi.astype(F32)).astype(BF16)
    d = functools.partial(jnp.dot, preferred_element_type=F32)
    o_ref[...] = d(s_hi, w_hi) + d(s_lo, w_hi) + d(s_hi, w_lo) + b_ref[...]


def _ada(c_all, w_ada, b_ada, tn=1024):
    r, d = c_all.shape
    n = w_ada.shape[1]
    return pl.pallas_call(
        _ada_kernel,
        grid=(n // tn,),
        in_specs=[
            pl.BlockSpec((r, d), lambda j: (0, 0)),
            pl.BlockSpec((d, tn), lambda j: (0, j)),
            pl.BlockSpec((1, tn), lambda j: (0, j)),
        ],
        out_specs=pl.BlockSpec((r, tn), lambda j: (0, j)),
        out_shape=jax.ShapeDtypeStruct((r, n), F32),
        compiler_params=_params(1),
        name="ada",
    )(c_all, w_ada, b_ada)


def _rope_chunk(c, cos, s1, s2):
    return c * cos + pltpu.roll(c, LANE - ROT_HALF, 1) * s1 + pltpu.roll(c, ROT_HALF, 1) * s2


def _inproj_kernel(x_ref, sc_ref, sh_ref, nw_ref, wa_ref, wm_ref, wb_ref, wc_ref, cos_ref, s1_ref, s2_ref, *outs,
                   d_model, d_inner, conv_dim, prompt):
    x = x_ref[...]
    h = x * lax.rsqrt(jnp.mean(x * x, axis=-1, keepdims=True) + EPS) * nw_ref[...]
    hb = (h * (1.0 + sc_ref[0]) + sh_ref[0]).astype(BF16)
    cos, s1, s2 = cos_ref[...], s1_ref[...], s2_ref[...]
    kv_w = N_KV_HEADS * HEAD_DIM
    iq_w = IDX_HEADS * IDX_DIM
    q0 = (wa_ref, 0)
    k0 = (wa_ref, d_model)
    v0 = (wa_ref, d_model + kv_w)
    iq0 = (wa_ref, d_model + 2 * kv_w)
    m0 = (wm_ref, 0)
    z0 = (wb_ref, 0)
    x0 = (wb_ref, d_inner)
    ga0 = (wc_ref, 0)
    gs0 = (wc_ref, d_model)

    def proj(group, width):
        w_ref, lo = group
        return jnp.dot(hb, w_ref[:, lo:lo + width], preferred_element_type=F32)

    if prompt:
        (q_ref, kt32_ref, kt_ref, vt32_ref, ve_ref, iqt_ref, ikt32_ref, ikr_ref, cwt_ref, misc_ref, z_ref, xbc_ref,
         ga_ref, gs_ref) = outs
    else:
        q_ref, k_ref, v_ref, iq_ref, misc_ref, z_ref, xbc_ref, ga_ref, gs_ref = outs

    q = proj(q0, d_model)
    q_scale = HEAD_DIM ** -0.5 * (LOG2E if prompt else 1.0)
    for c in range(d_model // LANE):
        qc = _rope_chunk(q[:, c * LANE:(c + 1) * LANE], cos, s1, s2) * q_scale
        if prompt:
            q_ref[0, 2 * c] = qc[:, :HEAD_DIM].astype(BF16)
            q_ref[0, 2 * c + 1] = qc[:, HEAD_DIM:].astype(BF16)
        else:
            q_ref[:, c * LANE:(c + 1) * LANE] = qc

    kk = proj(k0, kv_w)
    for c in range(kv_w // LANE):
        kc = _rope_chunk(kk[:, c * LANE:(c + 1) * LANE], cos, s1, s2)
        if prompt:
            kc_t = kc.T
            kt32_ref[0, c * LANE:(c + 1) * LANE, :] = kc_t
            kt_ref[0, c * LANE:(c + 1) * LANE, :] = kc_t.astype(BF16)
        else:
            k_ref[:, c * LANE:(c + 1) * LANE] = kc

    vv = proj(v0, kv_w)
    if not prompt:
        v_ref[...] = vv
    else:
        for c in range(kv_w // LANE):
            vt32_ref[0, c * LANE:(c + 1) * LANE, :] = vv[:, c * LANE:(c + 1) * LANE].T
        lane = lax.broadcasted_iota(I32, (vv.shape[0], LANE), 1)
        for g in range(N_KV_HEADS):
            vc = vv[:, (g // 2) * LANE:(g // 2 + 1) * LANE]
            if g % 2 == 1:
                vc = pltpu.roll(vc, HEAD_DIM, 1)
            ve = jnp.where(lane < HEAD_DIM, vc, jnp.where(lane == HEAD_DIM, 1.0, 0.0))
            ve_ref[0, g] = ve.astype(BF16)

    iq = proj(iq0, iq_w)
    for c in range(iq_w // LANE):
        ic = _rope_chunk(iq[:, c * LANE:(c + 1) * LANE], cos, s1, s2)
        if prompt:
            iqt_ref[0, c * LANE:(c + 1) * LANE, :] = ic.T.astype(BF16)
        else:
            iq_ref[:, c * LANE:(c + 1) * LANE] = ic.astype(BF16)

    mm = proj(m0, LANE)
    lane = lax.broadcasted_iota(I32, mm.shape, 1)
    mm = jnp.where(lane < IDX_DIM, _rope_chunk(mm, cos, s1, s2), mm)
    misc_ref[...] = mm
    if prompt:
        mm_t = mm.T
        ikt32_ref[0] = mm_t[:IDX_DIM, :]
        ikr_ref[...] = mm[:, :IDX_DIM].astype(BF16)
        cwt_ref[0] = mm_t[IW_OFF:IW_OFF + IDX_HEADS, :] * (IDX_HEADS ** -0.5) * (IDX_DIM ** -0.5)

    z_ref[...] = proj(z0, d_inner)
    xbc_ref[...] = proj(x0, conv_dim)
    ga_ref[...] = proj(ga0, d_model)
    gs_ref[...] = proj(gs0, d_model)


def _inproj(x2d, sc, sh, nw, weights, cos, s1, s2, *, n_batch, seq, tm, prompt, d_inner, conv_dim):
    rows, d = x2d.shape
    tb = seq // tm
    r_mod = sc.shape[1]
    kv_w = N_KV_HEADS * HEAD_DIM
    iq_w = IDX_HEADS * IDX_DIM
    n_heads = d // HEAD_DIM
    if cos.shape[0] == 1:
        tab = pl.BlockSpec((1, LANE), lambda i: (0, 0))
    else:
        tab = pl.BlockSpec((tm, LANE), lambda i: (i % tb, 0))
    mod = pl.BlockSpec((1, r_mod, d), lambda i: (i // tb, 0, 0))
    in_specs = [pl.BlockSpec((tm, d), lambda i: (i, 0)), mod, mod, _const_spec((1, d))]
    in_specs += [_const_spec(w.shape) for w in weights] + [tab, tab, tab]

    def rowspec(w):
        return pl.BlockSpec((tm, w), lambda i: (i, 0))

    def sds(shape, dt):
        return jax.ShapeDtypeStruct(shape, dt)

    if prompt:
        def tspec(w):
            return pl.BlockSpec((1, w, tm), lambda i: (i // tb, 0, i % tb))

        out_specs = [
            pl.BlockSpec((1, n_heads, tm, HEAD_DIM), lambda i: (i // tb, 0, i % tb, 0)),
            tspec(kv_w), tspec(kv_w), tspec(kv_w),
            pl.BlockSpec((1, N_KV_HEADS, tm, LANE), lambda i: (i // tb, 0, i % tb, 0)),
            tspec(iq_w), tspec(IDX_DIM), rowspec(IDX_DIM), tspec(IDX_HEADS),
            rowspec(LANE), rowspec(d_inner), rowspec(conv_dim), rowspec(d), rowspec(d),
        ]
        out_shape = [
            sds((n_batch, n_heads, seq, HEAD_DIM), BF16), sds((n_batch, kv_w, seq), F32),
            sds((n_batch, kv_w, seq), BF16), sds((n_batch, kv_w, seq), F32),
            sds((n_batch, N_KV_HEADS, seq, LANE), BF16), sds((n_batch, iq_w, seq), BF16),
            sds((n_batch, IDX_DIM, seq), F32), sds((rows, IDX_DIM), BF16), sds((n_batch, IDX_HEADS, seq), F32),
            sds((rows, LANE), F32), sds((rows, d_inner), F32), sds((rows, conv_dim), F32), sds((rows, d), F32),
            sds((rows, d), F32),
        ]
    else:
        out_specs = [rowspec(d), rowspec(kv_w), rowspec(kv_w), rowspec(iq_w), rowspec(LANE), rowspec(d_inner),
                     rowspec(conv_dim), rowspec(d), rowspec(d)]
        out_shape = [sds((rows, d), F32), sds((rows, kv_w), F32), sds((rows, kv_w), F32), sds((rows, iq_w), BF16),
                     sds((rows, LANE), F32), sds((rows, d_inner), F32), sds((rows, conv_dim), F32),
                     sds((rows, d), F32), sds((rows, d), F32)]
    return pl.pallas_call(
        functools.partial(_inproj_kernel, d_model=d, d_inner=d_inner, conv_dim=conv_dim, prompt=prompt),
        grid=(rows // tm,),
        in_specs=in_specs,
        out_specs=out_specs,
        out_shape=out_shape,
        compiler_params=_params(1),
        name="inproj_prompt" if prompt else "inproj_sample",
    )(x2d, sc, sh, nw, *weights, cos, s1, s2)


def _attn_kernel(q_ref, iqt_ref, cwt_ref, kt_ref, ve_ref, ikr_ref, o_ref, sc_scr, sc16_scr, m_scr, acc_scr, *, tq,
                 topk):
    i = pl.program_id(1)
    nblk = i + 1
    tk = tq
    tkc = SOFTMAX_KEY_BLOCKS * tk
    nblk_c = (nblk + SOFTMAX_KEY_BLOCKS - 1) // SOFTMAX_KEY_BLOCKS
    hpg = q_ref.shape[1] // N_KV_HEADS
    cw = cwt_ref[0]
    kidx = lax.broadcasted_iota(I32, (tk, tq), 0)
    qidx = lax.broadcasted_iota(I32, (tk, tq), 1)

    def phase_a(j, carry):
        off = pl.multiple_of(j * tk, tk)
        ikb = ikr_ref[pl.ds(off, tk), :]
        acc = jnp.zeros((tk, tq), F32)
        for h in range(IDX_HEADS):
            x = jnp.dot(ikb, iqt_ref[0, h * IDX_DIM:(h + 1) * IDX_DIM, :], preferred_element_type=F32)
            acc = acc + jnp.maximum(x, 0.0) * cw[h:h + 1, :]
        ok = (off + kidx) <= (i * tq + qidx)
        sc = jnp.where(ok, acc, NEG_INF)
        sc_scr[pl.ds(off, tk), :] = sc
        sc16_scr[pl.ds(off, tk), :] = _floor_to_bf16(sc)
        return carry

    lax.fori_loop(0, nblk_c * SOFTMAX_KEY_BLOCKS, phase_a, 0)

    def count(cmp, thr):
        def body(j, acc):
            off = pl.multiple_of(j * tk, tk)
            for r in range(tk // COUNT_STRIP):
                sb = sc_scr[pl.ds(off + r * COUNT_STRIP, COUNT_STRIP), :]
                acc = acc + jnp.where(cmp(sb, thr), 1.0, 0.0)
            return acc

        acc = lax.fori_loop(0, nblk, body, jnp.zeros((COUNT_STRIP, tq), F32))
        return jnp.sum(acc, axis=0, keepdims=True)

    def count16(thr):
        thr16 = thr.astype(BF16)
        one, zero = jnp.ones((), BF16), jnp.zeros((), BF16)

        def body(j, acc):
            off = pl.multiple_of(j * tk, tk)
            for r in range(tk // COUNT_STRIP):
                sb = sc16_scr[pl.ds(off + r * COUNT_STRIP, COUNT_STRIP), :]
                acc = acc + jnp.where(sb >= thr16, one, zero)
            return acc

        acc = lax.fori_loop(0, nblk, body, jnp.zeros((COUNT_STRIP, tq), BF16))
        return jnp.sum(acc.astype(F32), axis=0, keepdims=True)

    count_ge = functools.partial(count, lax.ge)
    tau, cnt_tau = _kth_largest(
        [(16, count16, False), (EARLY_EXIT_FROM_BIT, count_ge, False), (32, count_ge, True)], (1, tq), topk,
        (nblk * tk).astype(F32))

    any_tie = jnp.max(jnp.where(cnt_tau > float(topk), 1.0, 0.0)) > 0.5

    @pl.when(any_tie)
    def _():
        need = float(topk) - count(lax.gt, tau)
        lower = jnp.where(qidx <= kidx, 1.0, 0.0).astype(BF16)

        def body(j, seen):
            off = pl.multiple_of(j * tk, tk)
            sb = sc_scr[pl.ds(off, tk), :]
            eq = sb == tau
            prefix = jnp.dot(lower, jnp.where(eq, 1.0, 0.0).astype(BF16), preferred_element_type=F32)
            late = jnp.where(seen + prefix > need, NEG_INF, sb)
            sc_scr[pl.ds(off, tk), :] = jnp.where(eq, late, sb)
            return seen + prefix[tk - 1:tk, :]

        lax.fori_loop(0, nblk, body, jnp.zeros((1, tq), F32))

    tau_c = jnp.maximum(tau, F32_LOWEST)
    m_scr[...] = jnp.full(m_scr.shape, NEG, F32)
    acc_scr[...] = jnp.zeros(acc_scr.shape, F32)

    def phase_c(jc, carry):
        off = pl.multiple_of(jc * tkc, tkc)
        bias = jnp.concatenate(
            [jnp.where(sc_scr[pl.ds(off + b * tk, tk), :] >= tau_c, 0.0, NEG).T for b in range(SOFTMAX_KEY_BLOCKS)],
            axis=1)
        for g in range(N_KV_HEADS):
            qg = q_ref[0, g * hpg:(g + 1) * hpg].reshape(hpg * tq, HEAD_DIM)
            s = jnp.dot(qg, kt_ref[0, g * HEAD_DIM:(g + 1) * HEAD_DIM, pl.ds(off, tkc)],
                        preferred_element_type=F32)
            s = s.reshape(hpg, tq, tkc) + bias[None]
            m_prev = m_scr[g]
            m_new = jnp.maximum(m_prev, jnp.max(s, axis=-1, keepdims=True))
            alpha = jnp.exp2(m_prev - m_new)
            p = jnp.exp2(s - jnp.concatenate([m_new] * (tkc // LANE), axis=-1))
            m_scr[g] = m_new
            pv = jnp.dot(p.reshape(hpg * tq, tkc).astype(BF16), ve_ref[0, g, pl.ds(off, tkc), :],
                         preferred_element_type=F32)
            acc_scr[g] = alpha * acc_scr[g] + pv.reshape(hpg, tq, LANE)
        return carry

    lax.fori_loop(0, nblk_c, phase_c, 0)

    for g in range(N_KV_HEADS):
        for hh in range(hpg):
            a = acc_scr[g, hh]
            hq = g * hpg + hh
            o_ref[:, hq * HEAD_DIM:(hq + 1) * HEAD_DIM] = a[:, :HEAD_DIM] / a[:, HEAD_DIM:HEAD_DIM + 1]


def _prompt_attention(q_hm, iqt, cwt, kt, ve, ikr, *, tq, topk):
    nb, nh, seq, _ = q_hm.shape
    hpg = nh // N_KV_HEADS
    tb = seq // tq
    kv_w = N_KV_HEADS * HEAD_DIM
    assert (seq // tq) % SOFTMAX_KEY_BLOCKS == 0

    def resident(shape, imap):
        return pl.BlockSpec(shape, imap, pipeline_mode=pl.Buffered(1))

    return pl.pallas_call(
        functools.partial(_attn_kernel, tq=tq, topk=topk),
        grid=(nb, tb),
        in_specs=[
            pl.BlockSpec((1, nh, tq, HEAD_DIM), lambda b, i: (b, 0, i, 0)),
            pl.BlockSpec((1, IDX_HEADS * IDX_DIM, tq), lambda b, i: (b, 0, i)),
            pl.BlockSpec((1, IDX_HEADS, tq), lambda b, i: (b, 0, i)),
            resident((1, kv_w, seq), lambda b, i: (b, 0, 0)),
            resident((1, N_KV_HEADS, seq, LANE), lambda b, i: (b, 0, 0, 0)),
            resident((seq, IDX_DIM), lambda b, i: (b, 0)),
        ],
        out_specs=pl.BlockSpec((tq, nh * HEAD_DIM), lambda b, i: (b * tb + i, 0)),
        out_shape=jax.ShapeDtypeStruct((nb * seq, nh * HEAD_DIM), F32),
        scratch_shapes=[
            pltpu.VMEM((seq, tq), F32),
            pltpu.VMEM((seq, tq), BF16),
            pltpu.VMEM((N_KV_HEADS, hpg, tq, LANE), F32),
            pltpu.VMEM((N_KV_HEADS, hpg, tq, LANE), F32),
        ],
        compiler_params=_params(2),
        name="prompt_attention",
    )(q_hm, iqt, cwt, kt, ve, ikr)


def _gated_group_norm(y, z, nw, d_inner):
    g = y * _silu(z)
    gw = d_inner // SSM_GROUPS
    outs = []
    for k in range(SSM_GROUPS):
        gg = g[:, k * gw:(k + 1) * gw]
        outs.append(gg * lax.rsqrt(jnp.mean(gg * gg, axis=-1, keepdims=True) + EPS))
    return jnp.concatenate(outs, axis=-1) * nw


def _ssd_kernel(xbc_ref, z_ref, misc_ref, cw_ref, cb_ref, dtb_ref, alog_ref, e_ref, dsk_ref, nw_ref, y_ref, st_ref,
                ext_scr, st_scr, y_scr, *, d_inner, n_heads):
    c = pl.program_id(1)
    q = xbc_ref.shape[0]
    gn = SSM_GROUPS * D_STATE
    hpg = n_heads // SSM_GROUPS
    gw = hpg * SSM_HEAD_DIM

    @pl.when(c == 0)
    def _():
        ext_scr[0:SUBLANE, :] = jnp.zeros((SUBLANE, ext_scr.shape[1]), F32)
        st_scr[...] = jnp.zeros(st_scr.shape, F32)

    u = xbc_ref[...]
    ext_scr[SUBLANE:, :] = u
    conv = cb_ref[...] + cw_ref[CONV_W - 1:CONV_W, :] * u
    for w in range(CONV_W - 1):
        lo = SUBLANE - (CONV_W - 1) + w
        conv = conv + cw_ref[w:w + 1, :] * ext_scr[lo:lo + q, :]
    ext_scr[0:SUBLANE, :] = u[q - SUBLANE:, :]
    act = _silu(conv)
    xs = act[:, :d_inner]
    bm = act[:, d_inner:d_inner + gn]
    cm = act[:, d_inner + gn:]

    dt_t = _softplus(misc_ref[...].T + dtb_ref[...])
    da_t = dt_t * (-jnp.exp(alog_ref[...]))
    ri = lax.broadcasted_iota(I32, (q, q), 0)
    ci = lax.broadcasted_iota(I32, (q, q), 1)
    cs_t = _split3_dot(da_t, jnp.where(ri <= ci, 1.0, 0.0).astype(BF16))
    cs = cs_t.T
    dt = dt_t.T
    e = e_ref[...]
    ecs_x = _split3_dot(jnp.exp(cs), e)
    wst_x = _split3_dot(dt * jnp.exp(cs[q - 1:q, :] - cs), e)
    tri = ri >= ci

    for g in range(SSM_GROUPS):
        bg = bm[:, g * D_STATE:(g + 1) * D_STATE]
        cg = cm[:, g * D_STATE:(g + 1) * D_STATE].astype(BF16)
        bg_t = bg.T.astype(BF16)
        cb = jnp.dot(cg, bg_t, preferred_element_type=F32)
        s_t = st_scr[g]
        y_off = jnp.dot(cg, s_t.astype(BF16), preferred_element_type=F32)
        xg = xs[:, g * gw:(g + 1) * gw]
        y_g = y_off * ecs_x[:, g * gw:(g + 1) * gw]
        pieces = []
        for hh in range(hpg):
            idx = DT_OFF + g * hpg + hh
            seg = cs[:, idx:idx + 1] - cs_t[idx:idx + 1, :]
            mm = cb * jnp.exp(jnp.where(tri, seg, NEG)) * dt_t[idx:idx + 1, :]
            pieces.append(_bdot(mm, xg[:, hh * SSM_HEAD_DIM:(hh + 1) * SSM_HEAD_DIM]))
        y_scr[:, g * gw:(g + 1) * gw] = y_g + jnp.concatenate(pieces, axis=-1)
        wg = (xg * wst_x[:, g * gw:(g + 1) * gw]).astype(BF16)
        st_scr[g] = s_t * ecs_x[q - 1:q, g * gw:(g + 1) * gw] + jnp.dot(bg_t, wg, preferred_element_type=F32)

    y = y_scr[...] + dsk_ref[...] * xs
    y_ref[...] = _gated_group_norm(y, z_ref[...], nw_ref[...], d_inner)

    @pl.when(c == pl.num_programs(1) - 1)
    def _():
        for g in range(SSM_GROUPS):
            st_ref[0, g * hpg:(g + 1) * hpg] = st_scr[g].T.reshape(hpg, SSM_HEAD_DIM, D_STATE)


def _prompt_ssd(xbc, z, misc, conv_w, conv_b, dtb_col, alog_col, e_mat, dsk_row, nw_row, *, n_batch, seq, d_inner,
                n_heads):
    q = SSD_CHUNK
    nc = seq // q
    cd = xbc.shape[1]
    hpg = n_heads // SSM_GROUPS
    return pl.pallas_call(
        functools.partial(_ssd_kernel, d_inner=d_inner, n_heads=n_heads),
        grid=(n_batch, nc),
        in_specs=[
            pl.BlockSpec((q, cd), lambda b, c: (b * nc + c, 0)),
            pl.BlockSpec((q, d_inner), lambda b, c: (b * nc + c, 0)),
            pl.BlockSpec((q, LANE), lambda b, c: (b * nc + c, 0)),
            _const_spec(conv_w.shape), _const_spec(conv_b.shape), _const_spec(dtb_col.shape),
            _const_spec(alog_col.shape), _const_spec(e_mat.shape), _const_spec(dsk_row.shape),
            _const_spec(nw_row.shape),
        ],
        out_specs=[
            pl.BlockSpec((q, d_inner), lambda b, c: (b * nc + c, 0)),
            pl.BlockSpec((1, n_heads, SSM_HEAD_DIM, D_STATE), lambda b, c: (b, 0, 0, 0)),
        ],
        out_shape=[
            jax.ShapeDtypeStruct((n_batch * seq, d_inner), F32),
            jax.ShapeDtypeStruct((n_batch, n_heads, SSM_HEAD_DIM, D_STATE), F32),
        ],
        scratch_shapes=[
            pltpu.VMEM((q + SUBLANE, cd), F32),
            pltpu.VMEM((SSM_GROUPS, D_STATE, hpg * SSM_HEAD_DIM), F32),
            pltpu.VMEM((q, d_inner), F32),
        ],
        compiler_params=_params(2),
        name="prompt_ssd",
    )(xbc, z, misc, conv_w, conv_b, dtb_col, alog_col, e_mat, dsk_row, nw_row)


def _mlp_kernel(x_ref, att_ref, y_ref, ga_ref, gs_ref, gt1_ref, sc2_ref, sh2_ref, gt2_ref, n2_ref, fn_ref, wo_ref,
                wu_ref, wd_ref, o_ref, *, ff_chunk):
    merged = _sigmoid(ga_ref[...]) * att_ref[...] + _sigmoid(gs_ref[...]) * y_ref[...]
    x1 = x_ref[...] + gt1_ref[0] * jnp.dot(merged.astype(BF16), wo_ref[...], preferred_element_type=F32)
    h2 = x1 * lax.rsqrt(jnp.mean(x1 * x1, axis=-1, keepdims=True) + EPS) * n2_ref[...]
    hb = (h2 * (1.0 + sc2_ref[0]) + sh2_ref[0]).astype(BF16)
    acc = jnp.zeros(x1.shape, F32)
    for c in range(wu_ref.shape[1] // ff_chunk):
        u = jnp.maximum(jnp.dot(hb, wu_ref[:, c * ff_chunk:(c + 1) * ff_chunk], preferred_element_type=F32), 0.0)
        acc = acc + jnp.dot((u * u).astype(BF16), wd_ref[c * ff_chunk:(c + 1) * ff_chunk, :],
                            preferred_element_type=F32)
    x2 = x1 + gt2_ref[0] * acc
    o_ref[...] = x2 * lax.rsqrt(jnp.mean(x2 * x2, axis=-1, keepdims=True) + EPS) * fn_ref[...]


def _mlp(x2d, att, y, ga, gs, gt1, sc2, sh2, gt2, n2, fn, wo, wu, wd, *, seq, tm, ff_chunk=1024):
    rows, d = x2d.shape
    tb = seq // tm
    r_mod = gt1.shape[1]
    row = pl.BlockSpec((tm, d), lambda i: (i, 0))
    mod = pl.BlockSpec((1, r_mod, d), lambda i: (i // tb, 0, 0))
    return pl.pallas_call(
        functools.partial(_mlp_kernel, ff_chunk=ff_chunk),
        grid=(rows // tm,),
        in_specs=[row, row, row, row, row, mod, mod, mod, mod, _const_spec((1, d)), _const_spec((1, d)),
                  _const_spec(wo.shape), _const_spec(wu.shape), _const_spec(wd.shape)],
        out_specs=row,
        out_shape=jax.ShapeDtypeStruct((rows, d), F32),
        compiler_params=_params(1),
        name="merge_mlp",
    )(x2d, att, y, ga, gs, gt1, sc2, sh2, gt2, n2, fn, wo, wu, wd)


def _sidx_kernel(pt_ref, iq_ref, cw_ref, iknew_ref, *rest, pages_per_step, n_steps):
    del pt_ref
    page_refs, o_ref, ik_scr = rest[:pages_per_step], rest[pages_per_step], rest[pages_per_step + 1]
    r = pl.program_id(1)
    iq = iq_ref[0]
    cw = cw_ref[0]

    def score(ik_t):
        x = _bdot(iq, ik_t)
        return jnp.sum(jnp.maximum(x, 0.0) * cw, axis=0, keepdims=True)

    @pl.when(r < n_steps)
    def _():
        for m in range(pages_per_step):
            ik_scr[:, m * LANE:(m + 1) * LANE] = page_refs[m][...].astype(BF16)
        o_ref[0] = score(ik_scr[...])

    @pl.when(r == n_steps)
    def _():
        s_new = score(iknew_ref[0])[:, 0:1]
        lane = lax.broadcasted_iota(I32, (1, o_ref.shape[2]), 1)
        o_ref[0] = jnp.where(lane == 0, jnp.broadcast_to(s_new, lane.shape), NEG_INF)


def _sample_index_scores(page_table, iq3, cw3, iknew_t, idx_cache_t, *, pages_per_step):
    db, n_pages = page_table.shape
    page = idx_cache_t.shape[3]
    n_steps = n_pages // pages_per_step
    width = pages_per_step * page

    def page_spec(m):
        return pl.BlockSpec(
            (None, None, IDX_DIM, page),
            lambda b, r, pt: (0, pt[b, jnp.minimum(r, n_steps - 1) * pages_per_step + m], 0, 0))

    grid_spec = pltpu.PrefetchScalarGridSpec(
        num_scalar_prefetch=1,
        grid=(db, n_steps + 1),
        in_specs=[
            pl.BlockSpec((1, IDX_HEADS, IDX_DIM), lambda b, r, pt: (b, 0, 0)),
            pl.BlockSpec((1, IDX_HEADS, 1), lambda b, r, pt: (b, 0, 0)),
            pl.BlockSpec((1, IDX_DIM, page), lambda b, r, pt: (b, 0, 0)),
        ] + [page_spec(m) for m in range(pages_per_step)],
        out_specs=pl.BlockSpec((1, 1, width), lambda b, r, pt: (b, 0, r)),
        scratch_shapes=[pltpu.VMEM((IDX_DIM, width), BF16)],
    )
    return pl.pallas_call(
        functools.partial(_sidx_kernel, pages_per_step=pages_per_step, n_steps=n_steps),
        grid_spec=grid_spec,
        out_shape=jax.ShapeDtypeStruct((db, 1, (n_steps + 1) * width), F32),
        compiler_params=_params(2),
        name="sample_index_scores",
    )(page_table, iq3, cw3, iknew_t, *([idx_cache_t] * pages_per_step))


def _ssel_kernel(sc_ref, bias_ref, *, topk):
    sc = sc_ref[...]
    db, lk = sc.shape

    def count_ge(thr):
        return jnp.sum(jnp.where(sc >= thr, 1.0, 0.0), axis=1, keepdims=True)

    tau, _ = _kth_largest([(32, count_ge, True)], (db, 1), topk, jnp.float32(lk))
    need = float(topk) - jnp.sum(jnp.where(sc > tau, 1.0, 0.0), axis=1, keepdims=True)
    ri = lax.broadcasted_iota(I32, (LANE, LANE), 0)
    ci = lax.broadcasted_iota(I32, (LANE, LANE), 1)
    upper = jnp.where(ri <= ci, 1.0, 0.0).astype(BF16)

    def body(j, seen):
        off = pl.multiple_of(j * LANE, LANE)
        sb = sc_ref[:, pl.ds(off, LANE)]
        eq = sb == tau
        prefix = jnp.dot(jnp.where(eq, 1.0, 0.0).astype(BF16), upper, preferred_element_type=F32)
        keep_eq = jnp.where(seen + prefix <= need, 0.0, NEG)
        sel = jnp.where(sb > tau, 0.0, jnp.where(eq, keep_eq, NEG))
        bias_ref[:, pl.ds(off, LANE)] = jnp.where(sb == NEG_INF, NEG, sel)
        return seen + prefix[:, LANE - 1:LANE]

    lax.fori_loop(0, lk // LANE, body, jnp.zeros((db, 1), F32))


def _sample_select(keys2d, *, topk):
    return pl.pallas_call(
        functools.partial(_ssel_kernel, topk=topk),
        out_shape=jax.ShapeDtypeStruct(keys2d.shape, F32),
        compiler_params=pltpu.CompilerParams(vmem_limit_bytes=VMEM_LIMIT),
        name="sample_select",
    )(keys2d)


def _sattn_kernel(pt_ref, q_ref, bias_ref, knew_ref, vnew_ref, *rest, pages_per_step, n_steps):
    del pt_ref
    k_refs = rest[:pages_per_step]
    v_refs = rest[pages_per_step:2 * pages_per_step]
    o_ref, m_scr, l_scr, acc_scr, k_scr, v_scr = rest[2 * pages_per_step:]
    r = pl.program_id(1)
    q = q_ref[0]
    bias = bias_ref[0]

    @pl.when(r == 0)
    def _():
        m_scr[...] = jnp.full(m_scr.shape, NEG, F32)
        l_scr[...] = jnp.zeros(l_scr.shape, F32)
        acc_scr[...] = jnp.zeros(acc_scr.shape, F32)

    def update(s, pv_fn):
        m_prev = m_scr[...]
        m_new = jnp.maximum(m_prev, jnp.max(s, axis=1, keepdims=True))
        alpha = jnp.exp(m_prev - m_new)
        p = jnp.exp(s - m_new[:, 0:1])
        m_scr[...] = m_new
        l_scr[...] = alpha * l_scr[...] + jnp.sum(p, axis=1, keepdims=True)
        acc_scr[...] = alpha[:, 0:1] * acc_scr[...] + pv_fn(p)

    @pl.when(r < n_steps)
    def _():
        for m in range(pages_per_step):
            k_scr[:, m * LANE:(m + 1) * LANE] = k_refs[m][...].astype(BF16)
            v_scr[:, m * LANE:(m + 1) * LANE] = v_refs[m][...].astype(BF16)
        s = jnp.dot(q, k_scr[...], preferred_element_type=F32) + bias
        update(s, lambda p: _bdot_nt(p, v_scr[...]))

    @pl.when(r == n_steps)
    def _():
        s = _bdot(q, knew_ref[0]) + bias[:, 0:LANE]
        update(s, lambda p: _bdot_nt(p, vnew_ref[0]))
        o_ref[0] = acc_scr[...] / l_scr[:, 0:1]


def _sample_attention(page_table, qmat, bias3, knew_t, vnew_t, cache_kt, cache_vt, *, pages_per_step):
    db, n_pages = page_table.shape
    kv_w, page = cache_kt.shape[2], cache_kt.shape[3]
    nh = qmat.shape[1]
    n_steps = n_pages // pages_per_step
    width = pages_per_step * page

    def page_spec(m):
        return pl.BlockSpec(
            (None, None, kv_w, page),
            lambda b, r, pt: (0, pt[b, jnp.minimum(r, n_steps - 1) * pages_per_step + m], 0, 0))

    pages = [page_spec(m) for m in range(pages_per_step)]
    grid_spec = pltpu.PrefetchScalarGridSpec(
        num_scalar_prefetch=1,
        grid=(db, n_steps + 1),
        in_specs=[
            pl.BlockSpec((1, nh, kv_w), lambda b, r, pt: (b, 0, 0)),
            pl.BlockSpec((1, 1, width), lambda b, r, pt: (b, 0, r)),
            pl.BlockSpec((1, kv_w, page), lambda b, r, pt: (b, 0, 0)),
            pl.BlockSpec((1, kv_w, page), lambda b, r, pt: (b, 0, 0)),
        ] + pages + pages,
        out_specs=pl.BlockSpec((1, nh, kv_w), lambda b, r, pt: (b, 0, 0)),
        scratch_shapes=[pltpu.VMEM((nh, LANE), F32), pltpu.VMEM((nh, LANE), F32), pltpu.VMEM((nh, kv_w), F32),
                        pltpu.VMEM((kv_w, width), BF16), pltpu.VMEM((kv_w, width), BF16)],
    )
    return pl.pallas_call(
        functools.partial(_sattn_kernel, pages_per_step=pages_per_step, n_steps=n_steps),
        grid_spec=grid_spec,
        out_shape=jax.ShapeDtypeStruct((db, nh, kv_w), F32),
        compiler_params=_params(2),
        name="sample_attention",
    )(page_table, qmat, bias3, knew_t, vnew_t, *([cache_kt] * pages_per_step), *([cache_vt] * pages_per_step))


def _sssd_kernel(xbc_ref, prev_ref, misc_ref, z_ref, st_ref, cw_ref, cb_ref, dtb_ref, alog_ref, e_ref, dsk_ref,
                 nw_ref, y_ref, sto_ref, *, d_inner, n_heads):
    gn = SSM_GROUPS * D_STATE
    hpg = n_heads // SSM_GROUPS
    gw = hpg * SSM_HEAD_DIM
    u = xbc_ref[0]
    prev = prev_ref[0]
    conv = cb_ref[...] + cw_ref[CONV_W - 1:CONV_W, :] * u
    for w in range(CONV_W - 1):
        conv = conv + cw_ref[w:w + 1, :] * prev[w:w + 1, :]
    act = _silu(conv)
    xs = act[:, :d_inner]
    bm = act[:, d_inner:d_inner + gn]
    cm = act[:, d_inner + gn:]
    dt = _softplus(misc_ref[0] + dtb_ref[...])
    dec = jnp.exp(dt * (-jnp.exp(alog_ref[...])))
    pad6 = jnp.zeros((SUBLANE - 2, LANE), F32)
    ex = _split3_dot(jnp.concatenate([dt, dec, pad6], axis=0), e_ref[...])
    rows = jnp.concatenate([xs * ex[0:1, :], ex[1:2, :], jnp.zeros((LANE - 2, d_inner), F32)], axis=0)
    cols = rows.T
    pad15 = jnp.zeros((2 * SUBLANE - 1, D_STATE), F32)
    ys = []
    for g in range(SSM_GROUPS):
        s0 = st_ref[0, g * hpg:(g + 1) * hpg].reshape(gw, D_STATE)
        cg = cols[g * gw:(g + 1) * gw, :]
        s1 = s0 * cg[:, 1:2] + cg[:, 0:1] * bm[:, g * D_STATE:(g + 1) * D_STATE]
        sto_ref[0, g * hpg:(g + 1) * hpg] = s1.reshape(hpg, SSM_HEAD_DIM, D_STATE)
        c16 = jnp.concatenate([cm[:, g * D_STATE:(g + 1) * D_STATE], pad15], axis=0)
        ys.append(_bdot_nt(c16, s1)[0:1, :])
    y = jnp.concatenate(ys, axis=-1) + dsk_ref[...] * xs
    y_ref[0] = _gated_group_norm(y, z_ref[0], nw_ref[...], d_inner)


def _sample_ssd(xbc3, prev, misc3, z3, state, conv_w, conv_b, dtb_row, alog_row, e_mat, dsk_row, nw_row, *,
                d_inner, n_heads):
    db = xbc3.shape[0]
    cd = xbc3.shape[2]

    def per(shape):
        n = len(shape)
        return pl.BlockSpec((1,) + tuple(shape[1:]), lambda b: (b,) + (0,) * (n - 1))

    return pl.pallas_call(
        functools.partial(_sssd_kernel, d_inner=d_inner, n_heads=n_heads),
        grid=(db,),
        in_specs=[per(xbc3.shape), per(prev.shape), per(misc3.shape), per(z3.shape), per(state.shape),
                  _const_spec(conv_w.shape), _const_spec(conv_b.shape), _const_spec(dtb_row.shape),
                  _const_spec(alog_row.shape), _const_spec(e_mat.shape), _const_spec(dsk_row.shape),
                  _const_spec(nw_row.shape)],
        out_specs=[per((db, 1, d_inner)), per(state.shape)],
        out_shape=[jax.ShapeDtypeStruct((db, 1, d_inner), F32), jax.ShapeDtypeStruct(state.shape, F32)],
        compiler_params=_params(1),
        name="sample_ssd",
    )(xbc3, prev, misc3, z3, state, conv_w, conv_b, dtb_row, alog_row, e_mat, dsk_row, nw_row)


def _rope_tables(pos):
    inv = ROPE_THETA ** (-jnp.arange(ROT_HALF, dtype=F32) * 2.0 / (2 * ROT_HALF))
    ang = pos.astype(F32)[:, None] * inv[None, :]
    cos, sin = jnp.cos(ang), jnp.sin(ang)
    n = pos.shape[0]
    one = jnp.ones((n, HEAD_DIM - 2 * ROT_HALF), F32)
    zero = jnp.zeros((n, HEAD_DIM - 2 * ROT_HALF), F32)
    z8 = jnp.zeros((n, ROT_HALF), F32)
    c_head = jnp.concatenate([cos, cos, one], axis=1)
    s1_head = jnp.concatenate([-sin, z8, zero], axis=1)
    s2_head = jnp.concatenate([z8, sin, zero], axis=1)
    rep = LANE // HEAD_DIM
    return jnp.tile(c_head, (1, rep)), jnp.tile(s1_head, (1, rep)), jnp.tile(s2_head, (1, rep))


def kernel(x_prompt, x_sample, cache_k, cache_v, cache_idx_k, state_conv, state_ssm, page_table, c_prompt, c_sample,
           w_ada, b_ada, norm1_w, w_in, conv_w, conv_b, dt_bias, a_log, d_skip, ssm_norm_w, w_out, norm2_w, w_up,
           w_down, final_norm_w):
    nb, seq, d = x_prompt.shape
    db, dec_seq, _ = x_sample.shape
    depth = w_in.shape[0]
    assert depth == 1 and dec_seq == 1 and seq % SSD_CHUNK == 0
    n_heads = d // HEAD_DIM
    n_ssm_heads = a_log.shape[1]
    d_inner = n_ssm_heads * SSM_HEAD_DIM
    conv_dim = conv_w.shape[2]
    kv_w = N_KV_HEADS * HEAD_DIM
    iq_w = IDX_HEADS * IDX_DIM
    n_pages, page = page_table.shape[1], cache_k.shape[2]
    past = n_pages * page

    wi = w_in[0]
    offs = np.cumsum([0, d, kv_w, kv_w, iq_w, IDX_DIM, IDX_HEADS, d_inner, conv_dim, n_ssm_heads, d, d])
    misc_w = jnp.concatenate(
        [wi[:, offs[4]:offs[6]], wi[:, offs[8]:offs[9]],
         jnp.zeros((d, LANE - IDX_DIM - IDX_HEADS - n_ssm_heads), F32)], axis=1)
    w_parts = tuple(w.astype(BF16) for w in (wi[:, :offs[4]], misc_w, wi[:, offs[6]:offs[8]], wi[:, offs[9]:]))
    wo, wu, wd = w_out[0].astype(BF16), w_up[0].astype(BF16), w_down[0].astype(BF16)
    nw1, nw2, fnw = norm1_w[0][None, :], norm2_w[0][None, :], final_norm_w[None, :]
    head_lane = jnp.zeros((LANE,), F32)
    dtb_row = head_lane.at[DT_OFF:DT_OFF + n_ssm_heads].set(dt_bias[0])[None, :]
    alog_row = head_lane.at[DT_OFF:DT_OFF + n_ssm_heads].set(a_log[0])[None, :]
    e_mat = jnp.zeros((LANE, d_inner), F32).at[DT_OFF:DT_OFF + n_ssm_heads].set(
        jnp.repeat(jnp.eye(n_ssm_heads, dtype=F32), SSM_HEAD_DIM, axis=1)).astype(BF16)
    dsk_row = jnp.repeat(d_skip[0], SSM_HEAD_DIM)[None, :]
    snw_row = ssm_norm_w[0][None, :]
    cw2, cb2 = conv_w[0], conv_b[0][None, :]

    n_mod = nb + db
    r_mod = -(-n_mod // SUBLANE) * SUBLANE
    c_all = jnp.concatenate([c_prompt, c_sample, jnp.zeros((r_mod - n_mod, d), F32)], axis=0)
    mod = _ada(c_all, w_ada[0], b_ada[0][None, :])
    sh1, sc1, gt1, sh2, sc2, gt2 = [mod[:, k * d:(k + 1) * d] for k in range(6)]

    def pmod(a):
        return a[:nb].reshape(nb, 1, d)

    def smod(a):
        return a[nb:nb + db].reshape(1, db, d)

    xp = x_prompt.reshape(nb * seq, d)
    cos_p, s1_p, s2_p = _rope_tables(jnp.arange(seq))
    (q_hm, kt32, kt, vt32, ve, iqt, ikt32, ikr, cwt, misc_p, z_p, xbc_p, ga_p, gs_p) = _inproj(
        xp, pmod(sc1), pmod(sh1), nw1, w_parts, cos_p, s1_p, s2_p, n_batch=nb, seq=seq, tm=256, prompt=True,
        d_inner=d_inner, conv_dim=conv_dim)
    topk_p = min(TOPK_MAX, seq // 4)
    att_p = _prompt_attention(q_hm, iqt, cwt, kt, ve, ikr, tq=256, topk=topk_p)
    y_p, ssm_p = _prompt_ssd(xbc_p, z_p, misc_p, cw2, cb2, dtb_row.T, alog_row.T, e_mat, dsk_row, snw_row,
                             n_batch=nb, seq=seq, d_inner=d_inner, n_heads=n_ssm_heads)
    out_p = _mlp(xp, att_p, y_p, ga_p, gs_p, pmod(gt1), pmod(sc2), pmod(sh2), pmod(gt2), nw2, fnw, wo, wu, wd,
                 seq=seq, tm=256)

    xs2 = x_sample.reshape(db, d)
    cos_s, s1_s, s2_s = _rope_tables(jnp.full((1,), past, jnp.int32))
    (q_s, k_s, v_s, iq_s, misc_s, z_s, xbc_s, ga_s, gs_s) = _inproj(
        xs2, smod(sc1), smod(sh1), nw1, w_parts, cos_s, s1_s, s2_s, n_batch=1, seq=db, tm=db, prompt=False,
        d_inner=d_inner, conv_dim=conv_dim)
    pps_idx = _largest_divisor(n_pages, IDX_PAGES_PER_STEP)
    pps_kv = _largest_divisor(n_pages, KV_PAGES_PER_STEP)
    cw_s = (misc_s[:, IW_OFF:IW_OFF + IDX_HEADS] * (IDX_HEADS ** -0.5) * (IDX_DIM ** -0.5)).reshape(db, IDX_HEADS, 1)

    def as_page_t(a):
        return jnp.pad(a[:, :, None], ((0, 0), (0, 0), (0, page - 1)))

    idx_cache_t = jnp.transpose(cache_idx_k, (0, 1, 3, 2))
    kv_shape_t = (depth, cache_k.shape[1], kv_w, page)
    cache_kt = jnp.transpose(cache_k, (0, 1, 3, 4, 2)).reshape(kv_shape_t)
    cache_vt = jnp.transpose(cache_v, (0, 1, 3, 4, 2)).reshape(kv_shape_t)

    sc_s = _sample_index_scores(page_table, iq_s.reshape(db, IDX_HEADS, IDX_DIM), cw_s,
                                as_page_t(misc_s[:, :IDX_DIM]), idx_cache_t, pages_per_step=pps_idx)
    topk_s = min(TOPK_MAX, (past + 1) // 4)
    lk = past + pps_kv * page
    assert pps_idx >= pps_kv
    bias_s = _sample_select(sc_s[:, 0, :lk], topk=topk_s).reshape(db, 1, lk)
    hpg = n_heads // N_KV_HEADS
    q5 = q_s.reshape(db, N_KV_HEADS, hpg, 1, HEAD_DIM)
    eye = jnp.eye(N_KV_HEADS, dtype=F32)[None, :, None, :, None]
    qmat = (q5 * eye).reshape(db, n_heads, kv_w).astype(BF16)
    acc_s = _sample_attention(page_table, qmat, bias_s, as_page_t(k_s), as_page_t(v_s), cache_kt, cache_vt,
                              pages_per_step=pps_kv)
    a5 = acc_s.reshape(db, N_KV_HEADS, hpg, N_KV_HEADS, HEAD_DIM)
    att_s = jnp.einsum('bghgd->bghd', a5).reshape(db, d)
    y_s, ssm_s = _sample_ssd(xbc_s.reshape(db, 1, conv_dim), state_conv[0], misc_s.reshape(db, 1, LANE),
                             z_s.reshape(db, 1, d_inner), state_ssm[0], cw2, cb2, dtb_row, alog_row, e_mat, dsk_row,
                             snw_row, d_inner=d_inner, n_heads=n_ssm_heads)
    out_s = _mlp(xs2, att_s, y_s.reshape(db, d_inner), ga_s, gs_s, smod(gt1), smod(sc2), smod(sh2), smod(gt2), nw2,
                 fnw, wo, wu, wd, seq=db, tm=db)

    conv_p = xbc_p.reshape(nb, seq, conv_dim)[:, seq - (CONV_W - 1):, :]
    conv_s = jnp.concatenate([state_conv[0][:, 1:, :], xbc_s[:, None, :]], axis=1)
    def kv_out(a_t):
        return jnp.transpose(a_t.reshape(nb, N_KV_HEADS, HEAD_DIM, seq), (0, 3, 1, 2))[None]

    return (
        out_p.reshape(nb, seq, d),
        out_s.reshape(db, 1, d),
        kv_out(kt32),
        kv_out(vt32),
        jnp.transpose(ikt32, (0, 2, 1))[None],
        conv_p[None],
        ssm_p[None],
        k_s.reshape(1, db, 1, N_KV_HEADS, HEAD_DIM),
        v_s.reshape(1, db, 1, N_KV_HEADS, HEAD_DIM),
        misc_s[:, :IDX_DIM].reshape(1, db, 1, IDX_DIM),
        conv_s[None],
        ssm_s[None],
    )
```

```python
import functools

import jax
import jax.numpy as jnp
import numpy as np
from jax import lax
from jax.experimental import pallas as pl
from jax.experimental.pallas import tpu as pltpu

F32 = jnp.float32
BF16 = jnp.bfloat16
I32 = jnp.int32

HEAD_DIM = 64
N_KV_HEADS = 4
ROT_HALF = 8
ROPE_THETA = 500000.0
IDX_HEADS = 8
IDX_DIM = 64
TOPK_MAX = 256
SSM_HEAD_DIM = 64
SSM_GROUPS = 4
D_STATE = 128
CONV_W = 4
SSD_CHUNK = 128
EPS = 1e-6

LANE = 128
SUBLANE = 8
VMEM_LIMIT = 56 * 1024 * 1024
KV_PAGES_PER_STEP = 16
COUNT_STRIP = 64
SOFTMAX_KEY_BLOCKS = 2
EARLY_EXIT_FROM_BIT = 24

IW_OFF = IDX_DIM
DT_OFF = IDX_DIM + IDX_HEADS
INT_MIN = -(2 ** 31)
NEG = -1e30
NEG_INF = float("-inf")
F32_LOWEST = float(np.finfo(np.float32).min)
LOG2E = 1.4426950408889634
BF16_STEP_DOWN = 1.25 * 2.0 ** -8


def _sigmoid(x):
    return 1.0 / (1.0 + jnp.exp(-x))


def _silu(x):
    return x * _sigmoid(x)


def _softplus(x):
    return jnp.maximum(x, 0.0) + jnp.log1p(jnp.exp(-jnp.abs(x)))


def _bdot(a, b):
    return jnp.dot(a.astype(BF16), b.astype(BF16), preferred_element_type=F32)


def _bdot_nt(a, b):
    return lax.dot_general(a.astype(BF16), b.astype(BF16), (((1,), (1,)), ((), ())), preferred_element_type=F32)


def _split3_dot(v, e):
    hi = v.astype(BF16)
    r1 = v - hi.astype(F32)
    mid = r1.astype(BF16)
    lo = (r1 - mid.astype(F32)).astype(BF16)
    d = functools.partial(jnp.dot, preferred_element_type=F32)
    return d(hi, e) + d(mid, e) + d(lo, e)


def _f32_at_rank(u):
    key = u ^ jnp.int32(INT_MIN)
    bits = jnp.where(key < 0, jnp.int32(INT_MIN) - key, key)
    return lax.bitcast_convert_type(bits, F32)


def _floor_to_bf16(x):
    r = x.astype(BF16).astype(F32)
    below = r - jnp.abs(r) * BF16_STEP_DOWN
    return jnp.where(r > x, below, r).astype(BF16)


def _kth_largest(stages, shape, topk, n_total):
    def step(t, tau_u, cnt_tau, count_ge):
        cand_u = tau_u | jnp.left_shift(jnp.int32(1), 31 - t)
        cnt = count_ge(_f32_at_rank(cand_u))
        take = cnt >= float(topk)
        return jnp.where(take, cand_u, tau_u), jnp.where(take, cnt, cnt_tau)

    t0 = 0
    tau_u, cnt_tau = jnp.zeros(shape, I32), jnp.zeros(shape, F32) + n_total
    for end_bit, count_ge, early_exit in stages:
        if early_exit:
            def cond(c, end_bit=end_bit):
                return jnp.logical_and(c[0] < end_bit, c[3] > 0)

            def body(c, count_ge=count_ge):
                tau_n, cnt_n = step(c[0], c[1], c[2], count_ge)
                return c[0] + 1, tau_n, cnt_n, (jnp.max(cnt_n) > float(topk)).astype(I32)

            init = (jnp.int32(t0), tau_u, cnt_tau, (jnp.max(cnt_tau) > float(topk)).astype(I32))
            _, tau_u, cnt_tau, _ = lax.while_loop(cond, body, init)
        else:
            tau_u, cnt_tau = lax.fori_loop(
                t0, end_bit, lambda t, c, count_ge=count_ge: step(t, c[0], c[1], count_ge), (tau_u, cnt_tau))
        t0 = end_bit
    return jnp.where(tau_u == 0, NEG_INF, _f32_at_rank(tau_u)), cnt_tau


def _largest_divisor(n, cap):
    return max(k for k in range(1, cap + 1) if n % k == 0)


def _const_spec(shape):
    n = len(shape)
    return pl.BlockSpec(shape, lambda *a: (0,) * n, pipeline_mode=pl.Buffered(1))


def _params(n_axes):
    return pltpu.CompilerParams(dimension_semantics=("arbitrary",) * n_axes, vmem_limit_bytes=VMEM_LIMIT)


def _ada_kernel(c_ref, w_ref, b_ref, o_ref):
    s = _silu(c_ref[...])
    s_hi = s.astype(BF16)
    s_lo = (s - s_hi.astype(F32)).astype(BF16)
    w = w_ref[...]
    w_hi = w.astype(BF16)
    w_lo = (w - w_hi.astype(F32)).astype(BF16)
    d = functools.partial(jnp.dot, preferred_element_type=F32)
    o_ref[...] = d(s_hi, w_hi) + d(s_lo, w_hi) + d(s_hi, w_lo) + b_ref[...]


def _ada(c_all, w_ada, b_ada, tn=1024):
    r, d = c_all.shape
    n = w_ada.shape[1]
    return pl.pallas_call(
        _ada_kernel,
        grid=(n // tn,),
        in_specs=[
            pl.BlockSpec((r, d), lambda j: (0, 0)),
            pl.BlockSpec((d, tn), lambda j: (0, j)),
            pl.BlockSpec((1, tn), lambda j: (0, j)),
        ],
        out_specs=pl.BlockSpec((r, tn), lambda j: (0, j)),
        out_shape=jax.ShapeDtypeStruct((r, n), F32),
        compiler_params=_params(1),
        name="ada",
    )(c_all, w_ada, b_ada)


def _rope_chunk(c, cos, s1, s2):
    return c * cos + pltpu.roll(c, LANE - ROT_HALF, 1) * s1 + pltpu.roll(c, ROT_HALF, 1) * s2


def _inproj_kernel(x_ref, sc_ref, sh_ref, nw_ref, wa_ref, wm_ref, wb_ref, wc_ref, cos_ref, s1_ref, s2_ref, *outs,
                   d_model, d_inner, conv_dim, prompt):
    x = x_ref[...]
    h = x * lax.rsqrt(jnp.mean(x * x, axis=-1, keepdims=True) + EPS) * nw_ref[...]
    hb = (h * (1.0 + sc_ref[0]) + sh_ref[0]).astype(BF16)
    cos, s1, s2 = cos_ref[...], s1_ref[...], s2_ref[...]
    kv_w = N_KV_HEADS * HEAD_DIM
    iq_w = IDX_HEADS * IDX_DIM
    q0 = (wa_ref, 0)
    k0 = (wa_ref, d_model)
    v0 = (wa_ref, d_model + kv_w)
    iq0 = (wa_ref, d_model + 2 * kv_w)
    m0 = (wm_ref, 0)
    z0 = (wb_ref, 0)
    x0 = (wb_ref, d_inner)
    ga0 = (wc_ref, 0)
    gs0 = (wc_ref, d_model)

    def proj(group, width):
        w_ref, lo = group
        return jnp.dot(hb, w_ref[:, lo:lo + width], preferred_element_type=F32)

    if prompt:
        (q_ref, kt32_ref, kt_ref, vt32_ref, ve_ref, iqt_ref, ikt32_ref, ikr_ref, cwt_ref, misc_ref, z_ref, xbc_ref,
         ga_ref, gs_ref) = outs
    else:
        q_ref, k_ref, v_ref, iq_ref, misc_ref, z_ref, xbc_ref, ga_ref, gs_ref = outs

    q = proj(q0, d_model)
    q_scale = HEAD_DIM ** -0.5 * (LOG2E if prompt else 1.0)
    for c in range(d_model // LANE):
        qc = _rope_chunk(q[:, c * LANE:(c + 1) * LANE], cos, s1, s2) * q_scale
        if prompt:
            q_ref[0, 2 * c] = qc[:, :HEAD_DIM].astype(BF16)
            q_ref[0, 2 * c + 1] = qc[:, HEAD_DIM:].astype(BF16)
        else:
            q_ref[:, c * LANE:(c + 1) * LANE] = qc

    kk = proj(k0, kv_w)
    for c in range(kv_w // LANE):
        kc = _rope_chunk(kk[:, c * LANE:(c + 1) * LANE], cos, s1, s2)
        if prompt:
            kc_t = kc.T
            kt32_ref[0, c * LANE:(c + 1) * LANE, :] = kc_t
            kt_ref[0, c * LANE:(c + 1) * LANE, :] = kc_t.astype(BF16)
        else:
            k_ref[:, c * LANE:(c + 1) * LANE] = kc

    vv = proj(v0, kv_w)
    if not prompt:
        v_ref[...] = vv
    else:
        for c in range(kv_w // LANE):
            vt32_ref[0, c * LANE:(c + 1) * LANE, :] = vv[:, c * LANE:(c + 1) * LANE].T
        lane = lax.broadcasted_iota(I32, (vv.shape[0], LANE), 1)
        for g in range(N_KV_HEADS):
            vc = vv[:, (g // 2) * LANE:(g // 2 + 1) * LANE]
            if g % 2 == 1:
                vc = pltpu.roll(vc, HEAD_DIM, 1)
            ve = jnp.where(lane < HEAD_DIM, vc, jnp.where(lane == HEAD_DIM, 1.0, 0.0))
            ve_ref[0, g] = ve.astype(BF16)

    iq = proj(iq0, iq_w)
    for c in range(iq_w // LANE):
        ic = _rope_chunk(iq[:, c * LANE:(c + 1) * LANE], cos, s1, s2)
        if prompt:
            iqt_ref[0, c * LANE:(c + 1) * LANE, :] = ic.T.astype(BF16)
        else:
            iq_ref[:, c * LANE:(c + 1) * LANE] = ic.astype(BF16)

    mm = proj(m0, LANE)
    lane = lax.broadcasted_iota(I32, mm.shape, 1)
    mm = jnp.where(lane < IDX_DIM, _rope_chunk(mm, cos, s1, s2), mm)
    misc_ref[...] = mm
    if prompt:
        mm_t = mm.T
        ikt32_ref[0] = mm_t[:IDX_DIM, :]
        ikr_ref[...] = mm[:, :IDX_DIM].astype(BF16)
        cwt_ref[0] = mm_t[IW_OFF:IW_OFF + IDX_HEADS, :] * (IDX_HEADS ** -0.5) * (IDX_DIM ** -0.5)

    z_ref[...] = proj(z0, d_inner)
    xbc_ref[...] = proj(x0, conv_dim)
    ga_ref[...] = proj(ga0, d_model)
    gs_ref[...] = proj(gs0, d_model)


def _inproj(x2d, sc, sh, nw, weights, cos, s1, s2, *, n_batch, seq, tm, prompt, d_inner, conv_dim):
    rows, d = x2d.shape
    tb = seq // tm
    r_mod = sc.shape[1]
    kv_w = N_KV_HEADS * HEAD_DIM
    iq_w = IDX_HEADS * IDX_DIM
    n_heads = d // HEAD_DIM
    if cos.shape[0] == 1:
        tab = pl.BlockSpec((1, LANE), lambda i: (0, 0))
    else:
        tab = pl.BlockSpec((tm, LANE), lambda i: (i % tb, 0))
    mod = pl.BlockSpec((1, r_mod, d), lambda i: (i // tb, 0, 0))
    in_specs = [pl.BlockSpec((tm, d), lambda i: (i, 0)), mod, mod, _const_spec((1, d))]
    in_specs += [_const_spec(w.shape) for w in weights] + [tab, tab, tab]

    def rowspec(w):
        return pl.BlockSpec((tm, w), lambda i: (i, 0))

    def sds(shape, dt):
        return jax.ShapeDtypeStruct(shape, dt)

    if prompt:
        def tspec(w):
            return pl.BlockSpec((1, w, tm), lambda i: (i // tb, 0, i % tb))

        out_specs = [
            pl.BlockSpec((1, n_heads, tm, HEAD_DIM), lambda i: (i // tb, 0, i % tb, 0)),
            tspec(kv_w), tspec(kv_w), tspec(kv_w),
            pl.BlockSpec((1, N_KV_HEADS, tm, LANE), lambda i: (i // tb, 0, i % tb, 0)),
            tspec(iq_w), tspec(IDX_DIM), rowspec(IDX_DIM), tspec(IDX_HEADS),
            rowspec(LANE), rowspec(d_inner), rowspec(conv_dim), rowspec(d), rowspec(d),
        ]
        out_shape = [
            sds((n_batch, n_heads, seq, HEAD_DIM), BF16), sds((n_batch, kv_w, seq), F32),
            sds((n_batch, kv_w, seq), BF16), sds((n_batch, kv_w, seq), F32),
            sds((n_batch, N_KV_HEADS, seq, LANE), BF16), sds((n_batch, iq_w, seq), BF16),
            sds((n_batch, IDX_DIM, seq), F32), sds((rows, IDX_DIM), BF16), sds((n_batch, IDX_HEADS, seq), F32),
            sds((rows, LANE), F32), sds((rows, d_inner), F32), sds((rows, conv_dim), F32), sds((rows, d), F32),
            sds((rows, d), F32),
        ]
    else:
        out_specs = [rowspec(d), rowspec(kv_w), rowspec(kv_w), rowspec(iq_w), rowspec(LANE), rowspec(d_inner),
                     rowspec(conv_dim), rowspec(d), rowspec(d)]
        out_shape = [sds((rows, d), F32), sds((rows, kv_w), F32), sds((rows, kv_w), F32), sds((rows, iq_w), BF16),
                     sds((rows, LANE), F32), sds((rows, d_inner), F32), sds((rows, conv_dim), F32),
                     sds((rows, d), F32), sds((rows, d), F32)]
    return pl.pallas_call(
        functools.partial(_inproj_kernel, d_model=d, d_inner=d_inner, conv_dim=conv_dim, prompt=prompt),
        grid=(rows // tm,),
        in_specs=in_specs,
        out_specs=out_specs,
        out_shape=out_shape,
        compiler_params=_params(1),
        name="inproj_prompt" if prompt else "inproj_sample",
    )(x2d, sc, sh, nw, *weights, cos, s1, s2)


def _attn_kernel(q_ref, iqt_ref, cwt_ref, kt_ref, ve_ref, ikr_ref, o_ref, sc_scr, sc16_scr, m_scr, acc_scr, *, tq,
                 topk):
    i = pl.program_id(1)
    nblk = i + 1
    tk = tq
    tkc = SOFTMAX_KEY_BLOCKS * tk
    nblk_c = (nblk + SOFTMAX_KEY_BLOCKS - 1) // SOFTMAX_KEY_BLOCKS
    hpg = q_ref.shape[1] // N_KV_HEADS
    cw = cwt_ref[0]
    kidx = lax.broadcasted_iota(I32, (tk, tq), 0)
    qidx = lax.broadcasted_iota(I32, (tk, tq), 1)

    def store_scores(j, diagonal):
        off = pl.multiple_of(j * tk, tk)
        ikb = ikr_ref[pl.ds(off, tk), :]
        acc = jnp.zeros((tk, tq), F32)
        for h in range(IDX_HEADS):
            x = jnp.dot(ikb, iqt_ref[0, h * IDX_DIM:(h + 1) * IDX_DIM, :], preferred_element_type=F32)
            acc = acc + jnp.maximum(x, 0.0) * cw[h:h + 1, :]
        sc = jnp.where(kidx <= qidx, acc, NEG_INF) if diagonal else acc
        sc_scr[pl.ds(off, tk), :] = sc
        sc16_scr[pl.ds(off, tk), :] = _floor_to_bf16(sc)

    def full_block(j, carry):
        store_scores(j, False)
        return carry

    def masked_block(j, carry):
        off = pl.multiple_of(j * tk, tk)
        sc_scr[pl.ds(off, tk), :] = jnp.full((tk, tq), NEG_INF, F32)
        sc16_scr[pl.ds(off, tk), :] = jnp.full((tk, tq), NEG_INF, BF16)
        return carry

    lax.fori_loop(0, i, full_block, 0)
    store_scores(i, True)
    lax.fori_loop(nblk, nblk_c * SOFTMAX_KEY_BLOCKS, masked_block, 0)

    def count(cmp, thr):
        def body(j, acc):
            off = pl.multiple_of(j * tk, tk)
            for r in range(tk // COUNT_STRIP):
                sb = sc_scr[pl.ds(off + r * COUNT_STRIP, COUNT_STRIP), :]
                acc = acc + jnp.where(cmp(sb, thr), 1.0, 0.0)
            return acc

        acc = lax.fori_loop(0, nblk, body, jnp.zeros((COUNT_STRIP, tq), F32))
        return jnp.sum(acc, axis=0, keepdims=True)

    def count16(thr):
        thr16 = thr.astype(BF16)
        one, zero = jnp.ones((), BF16), jnp.zeros((), BF16)

        def body(j, acc):
            off = pl.multiple_of(j * tk, tk)
            for r in range(tk // COUNT_STRIP):
                sb = sc16_scr[pl.ds(off + r * COUNT_STRIP, COUNT_STRIP), :]
                acc = acc + jnp.where(sb >= thr16, one, zero)
            return acc

        acc = lax.fori_loop(0, nblk, body, jnp.zeros((COUNT_STRIP, tq), BF16))
        return jnp.sum(acc.astype(F32), axis=0, keepdims=True)

    count_ge = functools.partial(count, lax.ge)
    tau, cnt_tau = _kth_largest(
        [(16, count16, False), (EARLY_EXIT_FROM_BIT, count_ge, False), (32, count_ge, True)], (1, tq), topk,
        (nblk * tk).astype(F32))

    any_tie = jnp.max(jnp.where(cnt_tau > float(topk), 1.0, 0.0)) > 0.5

    @pl.when(any_tie)
    def _():
        need = float(topk) - count(lax.gt, tau)
        lower = jnp.where(qidx <= kidx, 1.0, 0.0).astype(BF16)

        def body(j, seen):
            off = pl.multiple_of(j * tk, tk)
            sb = sc_scr[pl.ds(off, tk), :]
            eq = sb == tau
            prefix = jnp.dot(lower, jnp.where(eq, 1.0, 0.0).astype(BF16), preferred_element_type=F32)
            late = jnp.where(seen + prefix > need, NEG_INF, sb)
            sc_scr[pl.ds(off, tk), :] = jnp.where(eq, late, sb)
            return seen + prefix[tk - 1:tk, :]

        lax.fori_loop(0, nblk, body, jnp.zeros((1, tq), F32))

    tau_c = jnp.maximum(tau, F32_LOWEST)
    m_scr[...] = jnp.full(m_scr.shape, NEG, F32)
    acc_scr[...] = jnp.zeros(acc_scr.shape, F32)

    def phase_c(jc, carry):
        off = pl.multiple_of(jc * tkc, tkc)
        bias = jnp.concatenate(
            [jnp.where(sc_scr[pl.ds(off + b * tk, tk), :] >= tau_c, 0.0, NEG).T for b in range(SOFTMAX_KEY_BLOCKS)],
            axis=1)
        for g in range(N_KV_HEADS):
            qg = q_ref[0, g * hpg:(g + 1) * hpg].reshape(hpg * tq, HEAD_DIM)
            s = jnp.dot(qg, kt_ref[0, g * HEAD_DIM:(g + 1) * HEAD_DIM, pl.ds(off, tkc)],
                        preferred_element_type=F32)
            s = s.reshape(hpg, tq, tkc) + bias[None]
            m_prev = m_scr[g]
            m_new = jnp.maximum(m_prev, jnp.max(s, axis=-1, keepdims=True))
            alpha = jnp.exp2(m_prev - m_new)
            p = jnp.exp2(s - jnp.concatenate([m_new] * (tkc // LANE), axis=-1))
            m_scr[g] = m_new
            pv = jnp.dot(p.reshape(hpg * tq, tkc).astype(BF16), ve_ref[0, g, pl.ds(off, tkc), :],
                         preferred_element_type=F32)
            acc_scr[g] = alpha * acc_scr[g] + pv.reshape(hpg, tq, LANE)
        return carry

    lax.fori_loop(0, nblk_c, phase_c, 0)

    for g in range(N_KV_HEADS):
        for hh in range(hpg):
            a = acc_scr[g, hh]
            hq = g * hpg + hh
            o_ref[:, hq * HEAD_DIM:(hq + 1) * HEAD_DIM] = a[:, :HEAD_DIM] / a[:, HEAD_DIM:HEAD_DIM + 1]


def _prompt_attention(q_hm, iqt, cwt, kt, ve, ikr, *, tq, topk):
    nb, nh, seq, _ = q_hm.shape
    hpg = nh // N_KV_HEADS
    tb = seq // tq
    kv_w = N_KV_HEADS * HEAD_DIM
    assert (seq // tq) % SOFTMAX_KEY_BLOCKS == 0

    def resident(shape, imap):
        return pl.BlockSpec(shape, imap, pipeline_mode=pl.Buffered(1))

    return pl.pallas_call(
        functools.partial(_attn_kernel, tq=tq, topk=topk),
        grid=(nb, tb),
        in_specs=[
            pl.BlockSpec((1, nh, tq, HEAD_DIM), lambda b, i: (b, 0, i, 0)),
            pl.BlockSpec((1, IDX_HEADS * IDX_DIM, tq), lambda b, i: (b, 0, i)),
            pl.BlockSpec((1, IDX_HEADS, tq), lambda b, i: (b, 0, i)),
            resident((1, kv_w, seq), lambda b, i: (b, 0, 0)),
            resident((1, N_KV_HEADS, seq, LANE), lambda b, i: (b, 0, 0, 0)),
            resident((seq, IDX_DIM), lambda b, i: (b, 0)),
        ],
        out_specs=pl.BlockSpec((tq, nh * HEAD_DIM), lambda b, i: (b * tb + i, 0)),
        out_shape=jax.ShapeDtypeStruct((nb * seq, nh * HEAD_DIM), F32),
        scratch_shapes=[
            pltpu.VMEM((seq, tq), F32),
            pltpu.VMEM((seq, tq), BF16),
            pltpu.VMEM((N_KV_HEADS, hpg, tq, LANE), F32),
            pltpu.VMEM((N_KV_HEADS, hpg, tq, LANE), F32),
        ],
        compiler_params=_params(2),
        name="prompt_attention",
    )(q_hm, iqt, cwt, kt, ve, ikr)


def _gated_group_norm(y, z, nw, d_inner):
    g = y * _silu(z)
    gw = d_inner // SSM_GROUPS
    outs = []
    for k in range(SSM_GROUPS):
        gg = g[:, k * gw:(k + 1) * gw]
        outs.append(gg * lax.rsqrt(jnp.mean(gg * gg, axis=-1, keepdims=True) + EPS))
    return jnp.concatenate(outs, axis=-1) * nw


def _ssd_kernel(xbc_ref, z_ref, misc_ref, cw_ref, cb_ref, dtb_ref, alog_ref, e_ref, dsk_ref, nw_ref, y_ref, st_ref,
                ext_scr, st_scr, y_scr, *, d_inner, n_heads):
    c = pl.program_id(1)
    q = xbc_ref.shape[0]
    gn = SSM_GROUPS * D_STATE
    hpg = n_heads // SSM_GROUPS
    gw = hpg * SSM_HEAD_DIM

    @pl.when(c == 0)
    def _():
        ext_scr[0:SUBLANE, :] = jnp.zeros((SUBLANE, ext_scr.shape[1]), F32)
        st_scr[...] = jnp.zeros(st_scr.shape, F32)

    u = xbc_ref[...]
    ext_scr[SUBLANE:, :] = u
    conv = cb_ref[...] + cw_ref[CONV_W - 1:CONV_W, :] * u
    for w in range(CONV_W - 1):
        lo = SUBLANE - (CONV_W - 1) + w
        conv = conv + cw_ref[w:w + 1, :] * ext_scr[lo:lo + q, :]
    ext_scr[0:SUBLANE, :] = u[q - SUBLANE:, :]
    act = _silu(conv)
    xs = act[:, :d_inner]
    bm = act[:, d_inner:d_inner + gn]
    cm = act[:, d_inner + gn:]

    dt_t = _softplus(misc_ref[...].T + dtb_ref[...])
    da_t = dt_t * (-jnp.exp(alog_ref[...]))
    ri = lax.broadcasted_iota(I32, (q, q), 0)
    ci = lax.broadcasted_iota(I32, (q, q), 1)
    cs_t = _split3_dot(da_t, jnp.where(ri <= ci, 1.0, 0.0).astype(BF16))
    cs = cs_t.T
    dt = dt_t.T
    e = e_ref[...]
    ecs_x = _split3_dot(jnp.exp(cs), e)
    wst_x = _split3_dot(dt * jnp.exp(cs[q - 1:q, :] - cs), e)
    tri = ri >= ci

    for g in range(SSM_GROUPS):
        bg = bm[:, g * D_STATE:(g + 1) * D_STATE]
        cg = cm[:, g * D_STATE:(g + 1) * D_STATE].astype(BF16)
        bg_t = bg.T.astype(BF16)
        cb = jnp.dot(cg, bg_t, preferred_element_type=F32)
        s_t = st_scr[g]
        y_off = jnp.dot(cg, s_t.astype(BF16), preferred_element_type=F32)
        xg = xs[:, g * gw:(g + 1) * gw]
        y_g = y_off * ecs_x[:, g * gw:(g + 1) * gw]
        pieces = []
        for hh in range(hpg):
            idx = DT_OFF + g * hpg + hh
            seg = cs[:, idx:idx + 1] - cs_t[idx:idx + 1, :]
            mm = cb * jnp.exp(jnp.where(tri, seg, NEG)) * dt_t[idx:idx + 1, :]
            pieces.append(_bdot(mm, xg[:, hh * SSM_HEAD_DIM:(hh + 1) * SSM_HEAD_DIM]))
        y_scr[:, g * gw:(g + 1) * gw] = y_g + jnp.concatenate(pieces, axis=-1)
        wg = (xg * wst_x[:, g * gw:(g + 1) * gw]).astype(BF16)
        st_scr[g] = s_t * ecs_x[q - 1:q, g * gw:(g + 1) * gw] + jnp.dot(bg_t, wg, preferred_element_type=F32)

    y = y_scr[...] + dsk_ref[...] * xs
    y_ref[...] = _gated_group_norm(y, z_ref[...], nw_ref[...], d_inner)

    @pl.when(c == pl.num_programs(1) - 1)
    def _():
        for g in range(SSM_GROUPS):
            st_ref[0, g * hpg:(g + 1) * hpg] = st_scr[g].T.reshape(hpg, SSM_HEAD_DIM, D_STATE)


def _prompt_ssd(xbc, z, misc, conv_w, conv_b, dtb_col, alog_col, e_mat, dsk_row, nw_row, *, n_batch, seq, d_inner,
                n_heads):
    q = SSD_CHUNK
    nc = seq // q
    cd = xbc.shape[1]
    hpg = n_heads // SSM_GROUPS
    return pl.pallas_call(
        functools.partial(_ssd_kernel, d_inner=d_inner, n_heads=n_heads),
        grid=(n_batch, nc),
        in_specs=[
            pl.BlockSpec((q, cd), lambda b, c: (b * nc + c, 0)),
            pl.BlockSpec((q, d_inner), lambda b, c: (b * nc + c, 0)),
            pl.BlockSpec((q, LANE), lambda b, c: (b * nc + c, 0)),
            _const_spec(conv_w.shape), _const_spec(conv_b.shape), _const_spec(dtb_col.shape),
            _const_spec(alog_col.shape), _const_spec(e_mat.shape), _const_spec(dsk_row.shape),
            _const_spec(nw_row.shape),
        ],
        out_specs=[
            pl.BlockSpec((q, d_inner), lambda b, c: (b * nc + c, 0)),
            pl.BlockSpec((1, n_heads, SSM_HEAD_DIM, D_STATE), lambda b, c: (b, 0, 0, 0)),
        ],
        out_shape=[
            jax.ShapeDtypeStruct((n_batch * seq, d_inner), F32),
            jax.ShapeDtypeStruct((n_batch, n_heads, SSM_HEAD_DIM, D_STATE), F32),
        ],
        scratch_shapes=[
            pltpu.VMEM((q + SUBLANE, cd), F32),
            pltpu.VMEM((SSM_GROUPS, D_STATE, hpg * SSM_HEAD_DIM), F32),
            pltpu.VMEM((q, d_inner), F32),
        ],
        compiler_params=_params(2),
        name="prompt_ssd",
    )(xbc, z, misc, conv_w, conv_b, dtb_col, alog_col, e_mat, dsk_row, nw_row)


def _mlp_kernel(x_ref, att_ref, y_ref, ga_ref, gs_ref, gt1_ref, sc2_ref, sh2_ref, gt2_ref, n2_ref, fn_ref, wo_ref,
                wu_ref, wd_ref, o_ref, *, ff_chunk):
    merged = _sigmoid(ga_ref[...]) * att_ref[...] + _sigmoid(gs_ref[...]) * y_ref[...]
    x1 = x_ref[...] + gt1_ref[0] * jnp.dot(merged.astype(BF16), wo_ref[...], preferred_element_type=F32)
    h2 = x1 * lax.rsqrt(jnp.mean(x1 * x1, axis=-1, keepdims=True) + EPS) * n2_ref[...]
    hb = (h2 * (1.0 + sc2_ref[0]) + sh2_ref[0]).astype(BF16)
    acc = jnp.zeros(x1.shape, F32)
    for c in range(wu_ref.shape[1] // ff_chunk):
        u = jnp.maximum(jnp.dot(hb, wu_ref[:, c * ff_chunk:(c + 1) * ff_chunk], preferred_element_type=F32), 0.0)
        acc = acc + jnp.dot((u * u).astype(BF16), wd_ref[c * ff_chunk:(c + 1) * ff_chunk, :],
                            preferred_element_type=F32)
    x2 = x1 + gt2_ref[0] * acc
    o_ref[...] = x2 * lax.rsqrt(jnp.mean(x2 * x2, axis=-1, keepdims=True) + EPS) * fn_ref[...]


def _mlp(x2d, att, y, ga, gs, gt1, sc2, sh2, gt2, n2, fn, wo, wu, wd, *, seq, tm, ff_chunk=1024):
    rows, d = x2d.shape
    tb = seq // tm
    r_mod = gt1.shape[1]
    row = pl.BlockSpec((tm, d), lambda i: (i, 0))
    mod = pl.BlockSpec((1, r_mod, d), lambda i: (i // tb, 0, 0))
    return pl.pallas_call(
        functools.partial(_mlp_kernel, ff_chunk=ff_chunk),
        grid=(rows // tm,),
        in_specs=[row, row, row, row, row, mod, mod, mod, mod, _const_spec((1, d)), _const_spec((1, d)),
                  _const_spec(wo.shape), _const_spec(wu.shape), _const_spec(wd.shape)],
        out_specs=row,
        out_shape=jax.ShapeDtypeStruct((rows, d), F32),
        compiler_params=_params(1),
        name="merge_mlp",
    )(x2d, att, y, ga, gs, gt1, sc2, sh2, gt2, n2, fn, wo, wu, wd)


def _page_copy(cache_ref, page_id, buf, slot, m, sem):
    return pltpu.make_async_copy(cache_ref.at[0, page_id], buf.at[slot, m], sem.at[slot])


def _sidx_kernel(pt_ref, iq_ref, cw_ref, iknew_ref, ic_ref, o_ref, buf, sem, ik_scr, *, n_pages, tail):
    b = pl.program_id(0)
    slot = b % 2
    page = buf.shape[3]
    past = n_pages * page
    iq = iq_ref[0]
    cw = cw_ref[0]

    def fetch(sample, sl):
        for m in range(n_pages):
            _page_copy(ic_ref, pt_ref[sample, m], buf, sl, m, sem).start()

    @pl.when(b == 0)
    def _():
        fetch(0, 0)

    @pl.when(b + 1 < pl.num_programs(0))
    def _():
        fetch(b + 1, 1 - slot)

    for m in range(n_pages):
        _page_copy(ic_ref, 0, buf, slot, m, sem).wait()

    def score(ik_t):
        x = _bdot(iq, ik_t)
        return jnp.sum(jnp.maximum(x, 0.0) * cw, axis=0, keepdims=True)

    for m in range(n_pages):
        ik_scr[:, m * LANE:(m + 1) * LANE] = buf[slot, m].astype(BF16)
    o_ref[0, :, 0:past] = score(ik_scr[...])
    s_new = score(iknew_ref[0])[:, 0:1]
    lane = lax.broadcasted_iota(I32, (1, tail), 1)
    o_ref[0, :, past:past + tail] = jnp.where(lane == 0, jnp.broadcast_to(s_new, lane.shape), NEG_INF)


def _sample_index_scores(page_table, iq3, cw3, iknew_t, idx_cache_t, *, tail):
    db, n_pages = page_table.shape
    page = idx_cache_t.shape[3]
    past = n_pages * page
    assert page == LANE

    def per_sample(shape):
        return pl.BlockSpec((1,) + tuple(shape[1:]), lambda b, pt: (b, 0, 0))

    grid_spec = pltpu.PrefetchScalarGridSpec(
        num_scalar_prefetch=1,
        grid=(db,),
        in_specs=[per_sample(iq3.shape), per_sample(cw3.shape), per_sample(iknew_t.shape),
                  pl.BlockSpec(memory_space=pl.ANY)],
        out_specs=per_sample((db, 1, past + tail)),
        scratch_shapes=[pltpu.VMEM((2, n_pages, IDX_DIM, page), F32), pltpu.SemaphoreType.DMA((2,)),
                        pltpu.VMEM((IDX_DIM, past), BF16)],
    )
    return pl.pallas_call(
        functools.partial(_sidx_kernel, n_pages=n_pages, tail=tail),
        grid_spec=grid_spec,
        out_shape=jax.ShapeDtypeStruct((db, 1, past + tail), F32),
        compiler_params=_params(1),
        name="sample_index_scores",
    )(page_table, iq3, cw3, iknew_t, idx_cache_t)


def _ssel_kernel(sc_ref, bias_ref, *, topk):
    sc = sc_ref[...]
    db, lk = sc.shape

    def count_ge(thr):
        return jnp.sum(jnp.where(sc >= thr, 1.0, 0.0), axis=1, keepdims=True)

    tau, _ = _kth_largest([(32, count_ge, True)], (db, 1), topk, jnp.float32(lk))
    need = float(topk) - jnp.sum(jnp.where(sc > tau, 1.0, 0.0), axis=1, keepdims=True)
    ri = lax.broadcasted_iota(I32, (LANE, LANE), 0)
    ci = lax.broadcasted_iota(I32, (LANE, LANE), 1)
    upper = jnp.where(ri <= ci, 1.0, 0.0).astype(BF16)

    def body(j, seen):
        off = pl.multiple_of(j * LANE, LANE)
        sb = sc_ref[:, pl.ds(off, LANE)]
        eq = sb == tau
        prefix = jnp.dot(jnp.where(eq, 1.0, 0.0).astype(BF16), upper, preferred_element_type=F32)
        keep_eq = jnp.where(seen + prefix <= need, 0.0, NEG)
        sel = jnp.where(sb > tau, 0.0, jnp.where(eq, keep_eq, NEG))
        bias_ref[:, pl.ds(off, LANE)] = jnp.where(sb == NEG_INF, NEG, sel)
        return seen + prefix[:, LANE - 1:LANE]

    lax.fori_loop(0, lk // LANE, body, jnp.zeros((db, 1), F32))


def _sample_select(keys2d, *, topk):
    return pl.pallas_call(
        functools.partial(_ssel_kernel, topk=topk),
        out_shape=jax.ShapeDtypeStruct(keys2d.shape, F32),
        compiler_params=pltpu.CompilerParams(vmem_limit_bytes=VMEM_LIMIT),
        name="sample_select",
    )(keys2d)


def _sattn_kernel(pt_ref, q_ref, bias_ref, knew_ref, vnew_ref, kc_ref, vc_ref, o_ref, kbuf, vbuf, ksem, vsem, k_scr,
                  v_scr, m_scr, l_scr, acc_scr, *, pages_per_step, n_steps):
    b = pl.program_id(0)
    width = pages_per_step * kbuf.shape[3]
    q = q_ref[0]
    bias = bias_ref[0]

    def fetch(sample, step, slot):
        for m in range(pages_per_step):
            page_id = pt_ref[sample, step * pages_per_step + m]
            _page_copy(kc_ref, page_id, kbuf, slot, m, ksem).start()
            _page_copy(vc_ref, page_id, vbuf, slot, m, vsem).start()

    def wait(slot):
        for m in range(pages_per_step):
            _page_copy(kc_ref, 0, kbuf, slot, m, ksem).wait()
            _page_copy(vc_ref, 0, vbuf, slot, m, vsem).wait()

    @pl.when(b == 0)
    def _():
        fetch(0, 0, 0)

    m_scr[...] = jnp.full(m_scr.shape, NEG, F32)
    l_scr[...] = jnp.zeros(l_scr.shape, F32)
    acc_scr[...] = jnp.zeros(acc_scr.shape, F32)

    def update(s, pv_fn):
        m_prev = m_scr[...]
        m_new = jnp.maximum(m_prev, jnp.max(s, axis=1, keepdims=True))
        alpha = jnp.exp(m_prev - m_new)
        p = jnp.exp(s - m_new[:, 0:1])
        m_scr[...] = m_new
        l_scr[...] = alpha * l_scr[...] + jnp.sum(p, axis=1, keepdims=True)
        acc_scr[...] = alpha[:, 0:1] * acc_scr[...] + pv_fn(p)

    for step in range(n_steps):
        slot = step % 2
        if step + 1 < n_steps:
            fetch(b, step + 1, 1 - slot)
        else:
            @pl.when(b + 1 < pl.num_programs(0))
            def _():
                fetch(b + 1, 0, 0)
        wait(slot)
        for m in range(pages_per_step):
            k_scr[:, m * LANE:(m + 1) * LANE] = kbuf[slot, m].astype(BF16)
            v_scr[:, m * LANE:(m + 1) * LANE] = vbuf[slot, m].astype(BF16)
        s = jnp.dot(q, k_scr[...], preferred_element_type=F32) + bias[:, step * width:(step + 1) * width]
        update(s, lambda p: _bdot_nt(p, v_scr[...]))

    past = n_steps * width
    s = _bdot(q, knew_ref[0]) + bias[:, past:past + LANE]
    update(s, lambda p: _bdot_nt(p, vnew_ref[0]))
    o_ref[0] = acc_scr[...] / l_scr[:, 0:1]


def _sample_attention(page_table, qmat, bias3, knew_t, vnew_t, cache_kt, cache_vt, *, pages_per_step):
    db, n_pages = page_table.shape
    kv_w, page = cache_kt.shape[2], cache_kt.shape[3]
    nh = qmat.shape[1]
    n_steps = n_pages // pages_per_step
    width = pages_per_step * page
    assert n_steps % 2 == 0 and bias3.shape[2] >= n_steps * width + LANE

    def per_sample(shape):
        return pl.BlockSpec((1,) + tuple(shape[1:]), lambda b, pt: (b, 0, 0))

    grid_spec = pltpu.PrefetchScalarGridSpec(
        num_scalar_prefetch=1,
        grid=(db,),
        in_specs=[per_sample(qmat.shape), per_sample(bias3.shape), per_sample(knew_t.shape), per_sample(vnew_t.shape),
                  pl.BlockSpec(memory_space=pl.ANY), pl.BlockSpec(memory_space=pl.ANY)],
        out_specs=per_sample((db, nh, kv_w)),
        scratch_shapes=[
            pltpu.VMEM((2, pages_per_step, kv_w, page), F32), pltpu.VMEM((2, pages_per_step, kv_w, page), F32),
            pltpu.SemaphoreType.DMA((2,)), pltpu.SemaphoreType.DMA((2,)),
            pltpu.VMEM((kv_w, width), BF16), pltpu.VMEM((kv_w, width), BF16),
            pltpu.VMEM((nh, LANE), F32), pltpu.VMEM((nh, LANE), F32), pltpu.VMEM((nh, kv_w), F32),
        ],
    )
    return pl.pallas_call(
        functools.partial(_sattn_kernel, pages_per_step=pages_per_step, n_steps=n_steps),
        grid_spec=grid_spec,
        out_shape=jax.ShapeDtypeStruct((db, nh, kv_w), F32),
        compiler_params=_params(1),
        name="sample_attention",
    )(page_table, qmat, bias3, knew_t, vnew_t, cache_kt, cache_vt)


def _sssd_kernel(xbc_ref, prev_ref, misc_ref, z_ref, st_ref, cw_ref, cb_ref, dtb_ref, alog_ref, e_ref, dsk_ref,
                 nw_ref, y_ref, sto_ref, *, d_inner, n_heads):
    gn = SSM_GROUPS * D_STATE
    hpg = n_heads // SSM_GROUPS
    gw = hpg * SSM_HEAD_DIM
    u = xbc_ref[0]
    prev = prev_ref[0]
    conv = cb_ref[...] + cw_ref[CONV_W - 1:CONV_W, :] * u
    for w in range(CONV_W - 1):
        conv = conv + cw_ref[w:w + 1, :] * prev[w:w + 1, :]
    act = _silu(conv)
    xs = act[:, :d_inner]
    bm = act[:, d_inner:d_inner + gn]
    cm = act[:, d_inner + gn:]
    dt = _softplus(misc_ref[0] + dtb_ref[...])
    dec = jnp.exp(dt * (-jnp.exp(alog_ref[...])))
    pad6 = jnp.zeros((SUBLANE - 2, LANE), F32)
    ex = _split3_dot(jnp.concatenate([dt, dec, pad6], axis=0), e_ref[...])
    rows = jnp.concatenate([xs * ex[0:1, :], ex[1:2, :], jnp.zeros((LANE - 2, d_inner), F32)], axis=0)
    cols = rows.T
    pad15 = jnp.zeros((2 * SUBLANE - 1, D_STATE), F32)
    ys = []
    for g in range(SSM_GROUPS):
        s0 = st_ref[0, g * hpg:(g + 1) * hpg].reshape(gw, D_STATE)
        cg = cols[g * gw:(g + 1) * gw, :]
        s1 = s0 * cg[:, 1:2] + cg[:, 0:1] * bm[:, g * D_STATE:(g + 1) * D_STATE]
        sto_ref[0, g * hpg:(g + 1) * hpg] = s1.reshape(hpg, SSM_HEAD_DIM, D_STATE)
        c16 = jnp.concatenate([cm[:, g * D_STATE:(g + 1) * D_STATE], pad15], axis=0)
        ys.append(_bdot_nt(c16, s1)[0:1, :])
    y = jnp.concatenate(ys, axis=-1) + dsk_ref[...] * xs
    y_ref[0] = _gated_group_norm(y, z_ref[0], nw_ref[...], d_inner)


def _sample_ssd(xbc3, prev, misc3, z3, state, conv_w, conv_b, dtb_row, alog_row, e_mat, dsk_row, nw_row, *,
                d_inner, n_heads):
    db = xbc3.shape[0]
    cd = xbc3.shape[2]

    def per(shape):
        n = len(shape)
        return pl.BlockSpec((1,) + tuple(shape[1:]), lambda b: (b,) + (0,) * (n - 1))

    return pl.pallas_call(
        functools.partial(_sssd_kernel, d_inner=d_inner, n_heads=n_heads),
        grid=(db,),
        in_specs=[per(xbc3.shape), per(prev.shape), per(misc3.shape), per(z3.shape), per(state.shape),
                  _const_spec(conv_w.shape), _const_spec(conv_b.shape), _const_spec(dtb_row.shape),
                  _const_spec(alog_row.shape), _const_spec(e_mat.shape), _const_spec(dsk_row.shape),
                  _const_spec(nw_row.shape)],
        out_specs=[per((db, 1, d_inner)), per(state.shape)],
        out_shape=[jax.ShapeDtypeStruct((db, 1, d_inner), F32), jax.ShapeDtypeStruct(state.shape, F32)],
        compiler_params=_params(1),
        name="sample_ssd",
    )(xbc3, prev, misc3, z3, state, conv_w, conv_b, dtb_row, alog_row, e_mat, dsk_row, nw_row)


def _rope_tables(pos):
    inv = ROPE_THETA ** (-jnp.arange(ROT_HALF, dtype=F32) * 2.0 / (2 * ROT_HALF))
    ang = pos.astype(F32)[:, None] * inv[None, :]
    cos, sin = jnp.cos(ang), jnp.sin(ang)
    n = pos.shape[0]
    one = jnp.ones((n, HEAD_DIM - 2 * ROT_HALF), F32)
    zero = jnp.zeros((n, HEAD_DIM - 2 * ROT_HALF), F32)
    z8 = jnp.zeros((n, ROT_HALF), F32)
    c_head = jnp.concatenate([cos, cos, one], axis=1)
    s1_head = jnp.concatenate([-sin, z8, zero], axis=1)
    s2_head = jnp.concatenate([z8, sin, zero], axis=1)
    rep = LANE // HEAD_DIM
    return jnp.tile(c_head, (1, rep)), jnp.tile(s1_head, (1, rep)), jnp.tile(s2_head, (1, rep))


def kernel(x_prompt, x_sample, cache_k, cache_v, cache_idx_k, state_conv, state_ssm, page_table, c_prompt, c_sample,
           w_ada, b_ada, norm1_w, w_in, conv_w, conv_b, dt_bias, a_log, d_skip, ssm_norm_w, w_out, norm2_w, w_up,
           w_down, final_norm_w):
    nb, seq, d = x_prompt.shape
    db, dec_seq, _ = x_sample.shape
    depth = w_in.shape[0]
    assert depth == 1 and dec_seq == 1 and seq % SSD_CHUNK == 0
    n_heads = d // HEAD_DIM
    n_ssm_heads = a_log.shape[1]
    d_inner = n_ssm_heads * SSM_HEAD_DIM
    conv_dim = conv_w.shape[2]
    kv_w = N_KV_HEADS * HEAD_DIM
    iq_w = IDX_HEADS * IDX_DIM
    n_pages, page = page_table.shape[1], cache_k.shape[2]
    past = n_pages * page

    wi = w_in[0]
    offs = np.cumsum([0, d, kv_w, kv_w, iq_w, IDX_DIM, IDX_HEADS, d_inner, conv_dim, n_ssm_heads, d, d])
    misc_w = jnp.concatenate(
        [wi[:, offs[4]:offs[6]], wi[:, offs[8]:offs[9]],
         jnp.zeros((d, LANE - IDX_DIM - IDX_HEADS - n_ssm_heads), F32)], axis=1)
    w_parts = tuple(w.astype(BF16) for w in (wi[:, :offs[4]], misc_w, wi[:, offs[6]:offs[8]], wi[:, offs[9]:]))
    wo, wu, wd = w_out[0].astype(BF16), w_up[0].astype(BF16), w_down[0].astype(BF16)
    nw1, nw2, fnw = norm1_w[0][None, :], norm2_w[0][None, :], final_norm_w[None, :]
    head_lane = jnp.zeros((LANE,), F32)
    dtb_row = head_lane.at[DT_OFF:DT_OFF + n_ssm_heads].set(dt_bias[0])[None, :]
    alog_row = head_lane.at[DT_OFF:DT_OFF + n_ssm_heads].set(a_log[0])[None, :]
    e_mat = jnp.zeros((LANE, d_inner), F32).at[DT_OFF:DT_OFF + n_ssm_heads].set(
        jnp.repeat(jnp.eye(n_ssm_heads, dtype=F32), SSM_HEAD_DIM, axis=1)).astype(BF16)
    dsk_row = jnp.repeat(d_skip[0], SSM_HEAD_DIM)[None, :]
    snw_row = ssm_norm_w[0][None, :]
    cw2, cb2 = conv_w[0], conv_b[0][None, :]

    n_mod = nb + db
    r_mod = -(-n_mod // SUBLANE) * SUBLANE
    c_all = jnp.concatenate([c_prompt, c_sample, jnp.zeros((r_mod - n_mod, d), F32)], axis=0)
    mod = _ada(c_all, w_ada[0], b_ada[0][None, :])
    sh1, sc1, gt1, sh2, sc2, gt2 = [mod[:, k * d:(k + 1) * d] for k in range(6)]

    def pmod(a):
        return a[:nb].reshape(nb, 1, d)

    def smod(a):
        return a[nb:nb + db].reshape(1, db, d)

    xp = x_prompt.reshape(nb * seq, d)
    cos_p, s1_p, s2_p = _rope_tables(jnp.arange(seq))
    (q_hm, kt32, kt, vt32, ve, iqt, ikt32, ikr, cwt, misc_p, z_p, xbc_p, ga_p, gs_p) = _inproj(
        xp, pmod(sc1), pmod(sh1), nw1, w_parts, cos_p, s1_p, s2_p, n_batch=nb, seq=seq, tm=256, prompt=True,
        d_inner=d_inner, conv_dim=conv_dim)
    topk_p = min(TOPK_MAX, seq // 4)
    att_p = _prompt_attention(q_hm, iqt, cwt, kt, ve, ikr, tq=256, topk=topk_p)
    y_p, ssm_p = _prompt_ssd(xbc_p, z_p, misc_p, cw2, cb2, dtb_row.T, alog_row.T, e_mat, dsk_row, snw_row,
                             n_batch=nb, seq=seq, d_inner=d_inner, n_heads=n_ssm_heads)
    out_p = _mlp(xp, att_p, y_p, ga_p, gs_p, pmod(gt1), pmod(sc2), pmod(sh2), pmod(gt2), nw2, fnw, wo, wu, wd,
                 seq=seq, tm=256)

    xs2 = x_sample.reshape(db, d)
    cos_s, s1_s, s2_s = _rope_tables(jnp.full((1,), past, jnp.int32))
    (q_s, k_s, v_s, iq_s, misc_s, z_s, xbc_s, ga_s, gs_s) = _inproj(
        xs2, smod(sc1), smod(sh1), nw1, w_parts, cos_s, s1_s, s2_s, n_batch=1, seq=db, tm=db, prompt=False,
        d_inner=d_inner, conv_dim=conv_dim)
    pps_kv = _largest_divisor(n_pages, KV_PAGES_PER_STEP)
    cw_s = (misc_s[:, IW_OFF:IW_OFF + IDX_HEADS] * (IDX_HEADS ** -0.5) * (IDX_DIM ** -0.5)).reshape(db, IDX_HEADS, 1)

    def as_page_t(a):
        return jnp.pad(a[:, :, None], ((0, 0), (0, 0), (0, page - 1)))

    idx_cache_t = jnp.transpose(cache_idx_k, (0, 1, 3, 2))
    kv_shape_t = (depth, cache_k.shape[1], kv_w, page)
    cache_kt = jnp.transpose(cache_k, (0, 1, 3, 4, 2)).reshape(kv_shape_t)
    cache_vt = jnp.transpose(cache_v, (0, 1, 3, 4, 2)).reshape(kv_shape_t)

    lk = past + pps_kv * page
    sc_s = _sample_index_scores(page_table, iq_s.reshape(db, IDX_HEADS, IDX_DIM), cw_s,
                                as_page_t(misc_s[:, :IDX_DIM]), idx_cache_t, tail=pps_kv * page)
    topk_s = min(TOPK_MAX, (past + 1) // 4)
    bias_s = _sample_select(sc_s.reshape(db, lk), topk=topk_s).reshape(db, 1, lk)
    hpg = n_heads // N_KV_HEADS
    q5 = q_s.reshape(db, N_KV_HEADS, hpg, 1, HEAD_DIM)
    eye = jnp.eye(N_KV_HEADS, dtype=F32)[None, :, None, :, None]
    qmat = (q5 * eye).reshape(db, n_heads, kv_w).astype(BF16)
    acc_s = _sample_attention(page_table, qmat, bias_s, as_page_t(k_s), as_page_t(v_s), cache_kt, cache_vt,
                              pages_per_step=pps_kv)
    a5 = acc_s.reshape(db, N_KV_HEADS, hpg, N_KV_HEADS, HEAD_DIM)
    att_s = jnp.einsum('bghgd->bghd', a5).reshape(db, d)
    y_s, ssm_s = _sample_ssd(xbc_s.reshape(db, 1, conv_dim), state_conv[0], misc_s.reshape(db, 1, LANE),
                             z_s.reshape(db, 1, d_inner), state_ssm[0], cw2, cb2, dtb_row, alog_row, e_mat, dsk_row,
                             snw_row, d_inner=d_inner, n_heads=n_ssm_heads)
    out_s = _mlp(xs2, att_s, y_s.reshape(db, d_inner), ga_s, gs_s, smod(gt1), smod(sc2), smod(sh2), smod(gt2), nw2,
                 fnw, wo, wu, wd, seq=db, tm=db)

    conv_p = xbc_p.reshape(nb, seq, conv_dim)[:, seq - (CONV_W - 1):, :]
    conv_s = jnp.concatenate([state_conv[0][:, 1:, :], xbc_s[:, None, :]], axis=1)
    def kv_out(a_t):
        return jnp.transpose(a_t.reshape(nb, N_KV_HEADS, HEAD_DIM, seq), (0, 3, 1, 2))[None]

    return (
        out_p.reshape(nb, seq, d),
        out_s.reshape(db, 1, d),
        kv_out(kt32),
        kv_out(vt32),
        jnp.transpose(ikt32, (0, 2, 1))[None],
        conv_p[None],
        ssm_p[None],
        k_s.reshape(1, db, 1, N_KV_HEADS, HEAD_DIM),
        v_s.reshape(1, db, 1, N_KV_HEADS, HEAD_DIM),
        misc_s[:, :IDX_DIM].reshape(1, db, 1, IDX_DIM),
        conv_s[None],
        ssm_s[None],
    )
```

```python
import functools

import jax
import jax.numpy as jnp
import numpy as np
from jax import lax
from jax.experimental import pallas as pl
from jax.experimental.pallas import tpu as pltpu

F32 = jnp.float32
BF16 = jnp.bfloat16
I32 = jnp.int32

HEAD_DIM = 64
N_KV_HEADS = 4
ROT_HALF = 8
ROPE_THETA = 500000.0
IDX_HEADS = 8
IDX_DIM = 64
TOPK_MAX = 256
SSM_HEAD_DIM = 64
SSM_GROUPS = 4
D_STATE = 128
CONV_W = 4
SSD_CHUNK = 128
EPS = 1e-6

LANE = 128
SUBLANE = 8
VMEM_LIMIT = 56 * 1024 * 1024
KV_PAGES_PER_STEP = 16
KV_FETCH_SLOTS = 3
COUNT_STRIP = 64
SOFTMAX_KEY_BLOCKS = 2
EARLY_EXIT_FROM_BIT = 24

IW_OFF = IDX_DIM
DT_OFF = IDX_DIM + IDX_HEADS
INT_MIN = -(2 ** 31)
NEG = -1e30
NEG_INF = float("-inf")
F32_LOWEST = float(np.finfo(np.float32).min)
LOG2E = 1.4426950408889634
BF16_STEP_DOWN = 1.25 * 2.0 ** -8


def _sigmoid(x):
    return 1.0 / (1.0 + jnp.exp(-x))


def _silu(x):
    return x * _sigmoid(x)


def _softplus(x):
    return jnp.maximum(x, 0.0) + jnp.log1p(jnp.exp(-jnp.abs(x)))


def _bdot(a, b):
    return jnp.dot(a.astype(BF16), b.astype(BF16), preferred_element_type=F32)


def _bdot_nt(a, b):
    return lax.dot_general(a.astype(BF16), b.astype(BF16), (((1,), (1,)), ((), ())), preferred_element_type=F32)


def _split3_dot(v, e):
    hi = v.astype(BF16)
    r1 = v - hi.astype(F32)
    mid = r1.astype(BF16)
    lo = (r1 - mid.astype(F32)).astype(BF16)
    d = functools.partial(jnp.dot, preferred_element_type=F32)
    return d(hi, e) + d(mid, e) + d(lo, e)


def _f32_at_rank(u):
    key = u ^ jnp.int32(INT_MIN)
    bits = jnp.where(key < 0, jnp.int32(INT_MIN) - key, key)
    return lax.bitcast_convert_type(bits, F32)


def _floor_to_bf16(x):
    r = x.astype(BF16).astype(F32)
    below = r - jnp.abs(r) * BF16_STEP_DOWN
    return jnp.where(r > x, below, r).astype(BF16)


def _kth_largest(stages, shape, topk, n_total):
    def step(t, tau_u, cnt_tau, count_ge):
        cand_u = tau_u | jnp.left_shift(jnp.int32(1), 31 - t)
        cnt = count_ge(_f32_at_rank(cand_u))
        take = cnt >= float(topk)
        return jnp.where(take, cand_u, tau_u), jnp.where(take, cnt, cnt_tau)

    t0 = 0
    tau_u, cnt_tau = jnp.zeros(shape, I32), jnp.zeros(shape, F32) + n_total
    for end_bit, count_ge, early_exit in stages:
        if early_exit:
            def cond(c, end_bit=end_bit):
                return jnp.logical_and(c[0] < end_bit, c[3] > 0)

            def body(c, count_ge=count_ge):
                tau_n, cnt_n = step(c[0], c[1], c[2], count_ge)
                return c[0] + 1, tau_n, cnt_n, (jnp.max(cnt_n) > float(topk)).astype(I32)

            init = (jnp.int32(t0), tau_u, cnt_tau, (jnp.max(cnt_tau) > float(topk)).astype(I32))
            _, tau_u, cnt_tau, _ = lax.while_loop(cond, body, init)
        else:
            tau_u, cnt_tau = lax.fori_loop(
                t0, end_bit, lambda t, c, count_ge=count_ge: step(t, c[0], c[1], count_ge), (tau_u, cnt_tau))
        t0 = end_bit
    return jnp.where(tau_u == 0, NEG_INF, _f32_at_rank(tau_u)), cnt_tau


def _largest_divisor(n, cap):
    return max(k for k in range(1, cap + 1) if n % k == 0)


def _const_spec(shape):
    n = len(shape)
    return pl.BlockSpec(shape, lambda *a: (0,) * n, pipeline_mode=pl.Buffered(1))


def _params(n_axes):
    return pltpu.CompilerParams(dimension_semantics=("arbitrary",) * n_axes, vmem_limit_bytes=VMEM_LIMIT)


def _ada_kernel(c_ref, w_ref, b_ref, o_ref):
    s = _silu(c_ref[...])
    s_hi = s.astype(BF16)
    s_lo = (s - s_hi.astype(F32)).astype(BF16)
    w = w_ref[...]
    w_hi = w.astype(BF16)
    w_lo = (w - w_hi.astype(F32)).astype(BF16)
    d = functools.partial(jnp.dot, preferred_element_type=F32)
    o_ref[...] = d(s_hi, w_hi) + d(s_lo, w_hi) + d(s_hi, w_lo) + b_ref[...]


def _ada(c_all, w_ada, b_ada, tn=1024):
    r, d = c_all.shape
    n = w_ada.shape[1]
    return pl.pallas_call(
        _ada_kernel,
        grid=(n // tn,),
        in_specs=[
            pl.BlockSpec((r, d), lambda j: (0, 0)),
            pl.BlockSpec((d, tn), lambda j: (0, j)),
            pl.BlockSpec((1, tn), lambda j: (0, j)),
        ],
        out_specs=pl.BlockSpec((r, tn), lambda j: (0, j)),
        out_shape=jax.ShapeDtypeStruct((r, n), F32),
        compiler_params=_params(1),
        name="ada",
    )(c_all, w_ada, b_ada)


def _rope_chunk(c, cos, s1, s2):
    return c * cos + pltpu.roll(c, LANE - ROT_HALF, 1) * s1 + pltpu.roll(c, ROT_HALF, 1) * s2


def _inproj_kernel(x_ref, sc_ref, sh_ref, nw_ref, wa_ref, wm_ref, wb_ref, wc_ref, cos_ref, s1_ref, s2_ref, *outs,
                   d_model, d_inner, conv_dim, prompt):
    x = x_ref[...]
    h = x * lax.rsqrt(jnp.mean(x * x, axis=-1, keepdims=True) + EPS) * nw_ref[...]
    hb = (h * (1.0 + sc_ref[0]) + sh_ref[0]).astype(BF16)
    cos, s1, s2 = cos_ref[...], s1_ref[...], s2_ref[...]
    kv_w = N_KV_HEADS * HEAD_DIM
    iq_w = IDX_HEADS * IDX_DIM
    q0 = (wa_ref, 0)
    k0 = (wa_ref, d_model)
    v0 = (wa_ref, d_model + kv_w)
    iq0 = (wa_ref, d_model + 2 * kv_w)
    m0 = (wm_ref, 0)
    z0 = (wb_ref, 0)
    x0 = (wb_ref, d_inner)
    ga0 = (wc_ref, 0)
    gs0 = (wc_ref, d_model)

    def proj(group, width):
        w_ref, lo = group
        return jnp.dot(hb, w_ref[:, lo:lo + width], preferred_element_type=F32)

    if prompt:
        (q_ref, kt32_ref, kt_ref, vt32_ref, ve_ref, iqt_ref, ikt32_ref, ikr_ref, cwt_ref, misc_ref, z_ref, xbc_ref,
         ga_ref, gs_ref) = outs
    else:
        q_ref, k_ref, v_ref, iq_ref, misc_ref, z_ref, xbc_ref, ga_ref, gs_ref = outs

    q = proj(q0, d_model)
    q_scale = HEAD_DIM ** -0.5 * (LOG2E if prompt else 1.0)
    for c in range(d_model // LANE):
        qc = _rope_chunk(q[:, c * LANE:(c + 1) * LANE], cos, s1, s2) * q_scale
        if prompt:
            q_ref[0, 2 * c] = qc[:, :HEAD_DIM].astype(BF16)
            q_ref[0, 2 * c + 1] = qc[:, HEAD_DIM:].astype(BF16)
        else:
            q_ref[:, c * LANE:(c + 1) * LANE] = qc

    kk = proj(k0, kv_w)
    for c in range(kv_w // LANE):
        kc = _rope_chunk(kk[:, c * LANE:(c + 1) * LANE], cos, s1, s2)
        if prompt:
            kc_t = kc.T
            kt32_ref[0, c * LANE:(c + 1) * LANE, :] = kc_t
            kt_ref[0, c * LANE:(c + 1) * LANE, :] = kc_t.astype(BF16)
        else:
            k_ref[:, c * LANE:(c + 1) * LANE] = kc

    vv = proj(v0, kv_w)
    if not prompt:
        v_ref[...] = vv
    else:
        for c in range(kv_w // LANE):
            vt32_ref[0, c * LANE:(c + 1) * LANE, :] = vv[:, c * LANE:(c + 1) * LANE].T
        lane = lax.broadcasted_iota(I32, (vv.shape[0], LANE), 1)
        for g in range(N_KV_HEADS):
            vc = vv[:, (g // 2) * LANE:(g // 2 + 1) * LANE]
            if g % 2 == 1:
                vc = pltpu.roll(vc, HEAD_DIM, 1)
            ve = jnp.where(lane < HEAD_DIM, vc, jnp.where(lane == HEAD_DIM, 1.0, 0.0))
            ve_ref[0, g] = ve.astype(BF16)

    iq = proj(iq0, iq_w)
    for c in range(iq_w // LANE):
        ic = _rope_chunk(iq[:, c * LANE:(c + 1) * LANE], cos, s1, s2)
        if prompt:
            iqt_ref[0, c * LANE:(c + 1) * LANE, :] = ic.T.astype(BF16)
        else:
            iq_ref[:, c * LANE:(c + 1) * LANE] = ic.astype(BF16)

    mm = proj(m0, LANE)
    lane = lax.broadcasted_iota(I32, mm.shape, 1)
    mm = jnp.where(lane < IDX_DIM, _rope_chunk(mm, cos, s1, s2), mm)
    misc_ref[...] = mm
    if prompt:
        mm_t = mm.T
        ikt32_ref[0] = mm_t[:IDX_DIM, :]
        ikr_ref[...] = mm[:, :IDX_DIM].astype(BF16)
        cwt_ref[0] = mm_t[IW_OFF:IW_OFF + IDX_HEADS, :] * (IDX_HEADS ** -0.5) * (IDX_DIM ** -0.5)

    z_ref[...] = proj(z0, d_inner)
    xbc_ref[...] = proj(x0, conv_dim)
    ga_ref[...] = proj(ga0, d_model)
    gs_ref[...] = proj(gs0, d_model)


def _inproj(x2d, sc, sh, nw, weights, cos, s1, s2, *, n_batch, seq, tm, prompt, d_inner, conv_dim):
    rows, d = x2d.shape
    tb = seq // tm
    r_mod = sc.shape[1]
    kv_w = N_KV_HEADS * HEAD_DIM
    iq_w = IDX_HEADS * IDX_DIM
    n_heads = d // HEAD_DIM
    if cos.shape[0] == 1:
        tab = pl.BlockSpec((1, LANE), lambda i: (0, 0))
    else:
        tab = pl.BlockSpec((tm, LANE), lambda i: (i % tb, 0))
    mod = pl.BlockSpec((1, r_mod, d), lambda i: (i // tb, 0, 0))
    in_specs = [pl.BlockSpec((tm, d), lambda i: (i, 0)), mod, mod, _const_spec((1, d))]
    in_specs += [_const_spec(w.shape) for w in weights] + [tab, tab, tab]

    def rowspec(w):
        return pl.BlockSpec((tm, w), lambda i: (i, 0))

    def sds(shape, dt):
        return jax.ShapeDtypeStruct(shape, dt)

    if prompt:
        def tspec(w):
            return pl.BlockSpec((1, w, tm), lambda i: (i // tb, 0, i % tb))

        out_specs = [
            pl.BlockSpec((1, n_heads, tm, HEAD_DIM), lambda i: (i // tb, 0, i % tb, 0)),
            tspec(kv_w), tspec(kv_w), tspec(kv_w),
            pl.BlockSpec((1, N_KV_HEADS, tm, LANE), lambda i: (i // tb, 0, i % tb, 0)),
            tspec(iq_w), tspec(IDX_DIM), rowspec(IDX_DIM), tspec(IDX_HEADS),
            rowspec(LANE), rowspec(d_inner), rowspec(conv_dim), rowspec(d), rowspec(d),
        ]
        out_shape = [
            sds((n_batch, n_heads, seq, HEAD_DIM), BF16), sds((n_batch, kv_w, seq), F32),
            sds((n_batch, kv_w, seq), BF16), sds((n_batch, kv_w, seq), F32),
            sds((n_batch, N_KV_HEADS, seq, LANE), BF16), sds((n_batch, iq_w, seq), BF16),
            sds((n_batch, IDX_DIM, seq), F32), sds((rows, IDX_DIM), BF16), sds((n_batch, IDX_HEADS, seq), F32),
            sds((rows, LANE), F32), sds((rows, d_inner), F32), sds((rows, conv_dim), F32), sds((rows, d), F32),
            sds((rows, d), F32),
        ]
    else:
        out_specs = [rowspec(d), rowspec(kv_w), rowspec(kv_w), rowspec(iq_w), rowspec(LANE), rowspec(d_inner),
                     rowspec(conv_dim), rowspec(d), rowspec(d)]
        out_shape = [sds((rows, d), F32), sds((rows, kv_w), F32), sds((rows, kv_w), F32), sds((rows, iq_w), BF16),
                     sds((rows, LANE), F32), sds((rows, d_inner), F32), sds((rows, conv_dim), F32),
                     sds((rows, d), F32), sds((rows, d), F32)]
    return pl.pallas_call(
        functools.partial(_inproj_kernel, d_model=d, d_inner=d_inner, conv_dim=conv_dim, prompt=prompt),
        grid=(rows // tm,),
        in_specs=in_specs,
        out_specs=out_specs,
        out_shape=out_shape,
        compiler_params=_params(1),
        name="inproj_prompt" if prompt else "inproj_sample",
    )(x2d, sc, sh, nw, *weights, cos, s1, s2)


def _attn_kernel(q_ref, iqt_ref, cwt_ref, kt_ref, ve_ref, ikr_ref, o_ref, sc_scr, sc16_scr, m_scr, acc_scr, *, tq,
                 topk):
    i = pl.program_id(1)
    nblk = i + 1
    tk = tq
    tkc = SOFTMAX_KEY_BLOCKS * tk
    nblk_c = (nblk + SOFTMAX_KEY_BLOCKS - 1) // SOFTMAX_KEY_BLOCKS
    hpg = q_ref.shape[1] // N_KV_HEADS
    cw = cwt_ref[0]
    kidx = lax.broadcasted_iota(I32, (tk, tq), 0)
    qidx = lax.broadcasted_iota(I32, (tk, tq), 1)

    def store_scores(j, diagonal):
        off = pl.multiple_of(j * tk, tk)
        ikb = ikr_ref[pl.ds(off, tk), :]
        acc = jnp.zeros((tk, tq), F32)
        for h in range(IDX_HEADS):
            x = jnp.dot(ikb, iqt_ref[0, h * IDX_DIM:(h + 1) * IDX_DIM, :], preferred_element_type=F32)
            acc = acc + jnp.maximum(x, 0.0) * cw[h:h + 1, :]
        sc = jnp.where(kidx <= qidx, acc, NEG_INF) if diagonal else acc
        sc_scr[pl.ds(off, tk), :] = sc
        sc16_scr[pl.ds(off, tk), :] = _floor_to_bf16(sc)

    def full_block(j, carry):
        store_scores(j, False)
        return carry

    def masked_block(j, carry):
        off = pl.multiple_of(j * tk, tk)
        sc_scr[pl.ds(off, tk), :] = jnp.full((tk, tq), NEG_INF, F32)
        sc16_scr[pl.ds(off, tk), :] = jnp.full((tk, tq), NEG_INF, BF16)
        return carry

    lax.fori_loop(0, i, full_block, 0)
    store_scores(i, True)
    lax.fori_loop(nblk, nblk_c * SOFTMAX_KEY_BLOCKS, masked_block, 0)

    def count(cmp, thr):
        def body(j, acc):
            off = pl.multiple_of(j * tk, tk)
            for r in range(tk // COUNT_STRIP):
                sb = sc_scr[pl.ds(off + r * COUNT_STRIP, COUNT_STRIP), :]
                acc = acc + jnp.where(cmp(sb, thr), 1.0, 0.0)
            return acc

        acc = lax.fori_loop(0, nblk, body, jnp.zeros((COUNT_STRIP, tq), F32))
        return jnp.sum(acc, axis=0, keepdims=True)

    def count16(thr):
        thr16 = thr.astype(BF16)
        one, zero = jnp.ones((), BF16), jnp.zeros((), BF16)

        def body(j, acc):
            off = pl.multiple_of(j * tk, tk)
            for r in range(tk // COUNT_STRIP):
                sb = sc16_scr[pl.ds(off + r * COUNT_STRIP, COUNT_STRIP), :]
                acc = acc + jnp.where(sb >= thr16, one, zero)
            return acc

        acc = lax.fori_loop(0, nblk, body, jnp.zeros((COUNT_STRIP, tq), BF16))
        return jnp.sum(acc.astype(F32), axis=0, keepdims=True)

    count_ge = functools.partial(count, lax.ge)
    tau, cnt_tau = _kth_largest(
        [(16, count16, False), (EARLY_EXIT_FROM_BIT, count_ge, False), (32, count_ge, True)], (1, tq), topk,
        (nblk * tk).astype(F32))

    any_tie = jnp.max(jnp.where(cnt_tau > float(topk), 1.0, 0.0)) > 0.5

    @pl.when(any_tie)
    def _():
        need = float(topk) - count(lax.gt, tau)
        lower = jnp.where(qidx <= kidx, 1.0, 0.0).astype(BF16)

        def body(j, seen):
            off = pl.multiple_of(j * tk, tk)
            sb = sc_scr[pl.ds(off, tk), :]
            eq = sb == tau
            prefix = jnp.dot(lower, jnp.where(eq, 1.0, 0.0).astype(BF16), preferred_element_type=F32)
            late = jnp.where(seen + prefix > need, NEG_INF, sb)
            sc_scr[pl.ds(off, tk), :] = jnp.where(eq, late, sb)
            return seen + prefix[tk - 1:tk, :]

        lax.fori_loop(0, nblk, body, jnp.zeros((1, tq), F32))

    tau_c = jnp.maximum(tau, F32_LOWEST)
    m_scr[...] = jnp.full(m_scr.shape, NEG, F32)
    acc_scr[...] = jnp.zeros(acc_scr.shape, F32)

    def phase_c(jc, carry):
        off = pl.multiple_of(jc * tkc, tkc)
        bias = jnp.concatenate(
            [jnp.where(sc_scr[pl.ds(off + b * tk, tk), :] >= tau_c, 0.0, NEG).T for b in range(SOFTMAX_KEY_BLOCKS)],
            axis=1)
        for g in range(N_KV_HEADS):
            qg = q_ref[0, g * hpg:(g + 1) * hpg].reshape(hpg * tq, HEAD_DIM)
            s = jnp.dot(qg, kt_ref[0, g * HEAD_DIM:(g + 1) * HEAD_DIM, pl.ds(off, tkc)],
                        preferred_element_type=F32)
            s = s.reshape(hpg, tq, tkc) + bias[None]
            m_prev = m_scr[g]
            m_new = jnp.maximum(m_prev, jnp.max(s, axis=-1, keepdims=True))
            alpha = jnp.exp2(m_prev - m_new)
            p = jnp.exp2(s - jnp.concatenate([m_new] * (tkc // LANE), axis=-1))
            m_scr[g] = m_new
            pv = jnp.dot(p.reshape(hpg * tq, tkc).astype(BF16), ve_ref[0, g, pl.ds(off, tkc), :],
                         preferred_element_type=F32)
            acc_scr[g] = alpha * acc_scr[g] + pv.reshape(hpg, tq, LANE)
        return carry

    lax.fori_loop(0, nblk_c, phase_c, 0)

    for g in range(N_KV_HEADS):
        for hh in range(hpg):
            a = acc_scr[g, hh]
            hq = g * hpg + hh
            o_ref[:, hq * HEAD_DIM:(hq + 1) * HEAD_DIM] = a[:, :HEAD_DIM] / a[:, HEAD_DIM:HEAD_DIM + 1]


def _prompt_attention(q_hm, iqt, cwt, kt, ve, ikr, *, tq, topk):
    nb, nh, seq, _ = q_hm.shape
    hpg = nh // N_KV_HEADS
    tb = seq // tq
    kv_w = N_KV_HEADS * HEAD_DIM
    assert (seq // tq) % SOFTMAX_KEY_BLOCKS == 0

    def resident(shape, imap):
        return pl.BlockSpec(shape, imap, pipeline_mode=pl.Buffered(1))

    return pl.pallas_call(
        functools.partial(_attn_kernel, tq=tq, topk=topk),
        grid=(nb, tb),
        in_specs=[
            pl.BlockSpec((1, nh, tq, HEAD_DIM), lambda b, i: (b, 0, i, 0)),
            pl.BlockSpec((1, IDX_HEADS * IDX_DIM, tq), lambda b, i: (b, 0, i)),
            pl.BlockSpec((1, IDX_HEADS, tq), lambda b, i: (b, 0, i)),
            resident((1, kv_w, seq), lambda b, i: (b, 0, 0)),
            resident((1, N_KV_HEADS, seq, LANE), lambda b, i: (b, 0, 0, 0)),
            resident((seq, IDX_DIM), lambda b, i: (b, 0)),
        ],
        out_specs=pl.BlockSpec((tq, nh * HEAD_DIM), lambda b, i: (b * tb + i, 0)),
        out_shape=jax.ShapeDtypeStruct((nb * seq, nh * HEAD_DIM), F32),
        scratch_shapes=[
            pltpu.VMEM((seq, tq), F32),
            pltpu.VMEM((seq, tq), BF16),
            pltpu.VMEM((N_KV_HEADS, hpg, tq, LANE), F32),
            pltpu.VMEM((N_KV_HEADS, hpg, tq, LANE), F32),
        ],
        compiler_params=_params(2),
        name="prompt_attention",
    )(q_hm, iqt, cwt, kt, ve, ikr)


def _gated_group_norm(y, z, nw, d_inner):
    g = y * _silu(z)
    gw = d_inner // SSM_GROUPS
    outs = []
    for k in range(SSM_GROUPS):
        gg = g[:, k * gw:(k + 1) * gw]
        outs.append(gg * lax.rsqrt(jnp.mean(gg * gg, axis=-1, keepdims=True) + EPS))
    return jnp.concatenate(outs, axis=-1) * nw


def _ssd_kernel(xbc_ref, z_ref, misc_ref, cw_ref, cb_ref, dtb_ref, alog_ref, e_ref, dsk_ref, nw_ref, y_ref, st_ref,
                ext_scr, st_scr, y_scr, *, d_inner, n_heads):
    c = pl.program_id(1)
    q = xbc_ref.shape[0]
    gn = SSM_GROUPS * D_STATE
    hpg = n_heads // SSM_GROUPS
    gw = hpg * SSM_HEAD_DIM

    @pl.when(c == 0)
    def _():
        ext_scr[0:SUBLANE, :] = jnp.zeros((SUBLANE, ext_scr.shape[1]), F32)
        st_scr[...] = jnp.zeros(st_scr.shape, F32)

    u = xbc_ref[...]
    ext_scr[SUBLANE:, :] = u
    conv = cb_ref[...] + cw_ref[CONV_W - 1:CONV_W, :] * u
    for w in range(CONV_W - 1):
        lo = SUBLANE - (CONV_W - 1) + w
        conv = conv + cw_ref[w:w + 1, :] * ext_scr[lo:lo + q, :]
    ext_scr[0:SUBLANE, :] = u[q - SUBLANE:, :]
    act = _silu(conv)
    xs = act[:, :d_inner]
    bm = act[:, d_inner:d_inner + gn]
    cm = act[:, d_inner + gn:]

    dt_t = _softplus(misc_ref[...].T + dtb_ref[...])
    da_t = dt_t * (-jnp.exp(alog_ref[...]))
    ri = lax.broadcasted_iota(I32, (q, q), 0)
    ci = lax.broadcasted_iota(I32, (q, q), 1)
    cs_t = _split3_dot(da_t, jnp.where(ri <= ci, 1.0, 0.0).astype(BF16))
    cs = cs_t.T
    dt = dt_t.T
    e = e_ref[...]
    ecs_x = _split3_dot(jnp.exp(cs), e)
    wst_x = _split3_dot(dt * jnp.exp(cs[q - 1:q, :] - cs), e)
    tri = ri >= ci

    for g in range(SSM_GROUPS):
        bg = bm[:, g * D_STATE:(g + 1) * D_STATE]
        cg = cm[:, g * D_STATE:(g + 1) * D_STATE].astype(BF16)
        bg_t = bg.T.astype(BF16)
        cb = jnp.dot(cg, bg_t, preferred_element_type=F32)
        s_t = st_scr[g]
        y_off = jnp.dot(cg, s_t.astype(BF16), preferred_element_type=F32)
        xg = xs[:, g * gw:(g + 1) * gw]
        y_g = y_off * ecs_x[:, g * gw:(g + 1) * gw]
        pieces = []
        for hh in range(hpg):
            idx = DT_OFF + g * hpg + hh
            seg = cs[:, idx:idx + 1] - cs_t[idx:idx + 1, :]
            mm = cb * jnp.exp(jnp.where(tri, seg, NEG)) * dt_t[idx:idx + 1, :]
            pieces.append(_bdot(mm, xg[:, hh * SSM_HEAD_DIM:(hh + 1) * SSM_HEAD_DIM]))
        y_scr[:, g * gw:(g + 1) * gw] = y_g + jnp.concatenate(pieces, axis=-1)
        wg = (xg * wst_x[:, g * gw:(g + 1) * gw]).astype(BF16)
        st_scr[g] = s_t * ecs_x[q - 1:q, g * gw:(g + 1) * gw] + jnp.dot(bg_t, wg, preferred_element_type=F32)

    y = y_scr[...] + dsk_ref[...] * xs
    y_ref[...] = _gated_group_norm(y, z_ref[...], nw_ref[...], d_inner)

    @pl.when(c == pl.num_programs(1) - 1)
    def _():
        for g in range(SSM_GROUPS):
            st_ref[0, g * hpg:(g + 1) * hpg] = st_scr[g].T.reshape(hpg, SSM_HEAD_DIM, D_STATE)


def _prompt_ssd(xbc, z, misc, conv_w, conv_b, dtb_col, alog_col, e_mat, dsk_row, nw_row, *, n_batch, seq, d_inner,
                n_heads):
    q = SSD_CHUNK
    nc = seq // q
    cd = xbc.shape[1]
    hpg = n_heads // SSM_GROUPS
    return pl.pallas_call(
        functools.partial(_ssd_kernel, d_inner=d_inner, n_heads=n_heads),
        grid=(n_batch, nc),
        in_specs=[
            pl.BlockSpec((q, cd), lambda b, c: (b * nc + c, 0)),
            pl.BlockSpec((q, d_inner), lambda b, c: (b * nc + c, 0)),
            pl.BlockSpec((q, LANE), lambda b, c: (b * nc + c, 0)),
            _const_spec(conv_w.shape), _const_spec(conv_b.shape), _const_spec(dtb_col.shape),
            _const_spec(alog_col.shape), _const_spec(e_mat.shape), _const_spec(dsk_row.shape),
            _const_spec(nw_row.shape),
        ],
        out_specs=[
            pl.BlockSpec((q, d_inner), lambda b, c: (b * nc + c, 0)),
            pl.BlockSpec((1, n_heads, SSM_HEAD_DIM, D_STATE), lambda b, c: (b, 0, 0, 0)),
        ],
        out_shape=[
            jax.ShapeDtypeStruct((n_batch * seq, d_inner), F32),
            jax.ShapeDtypeStruct((n_batch, n_heads, SSM_HEAD_DIM, D_STATE), F32),
        ],
        scratch_shapes=[
            pltpu.VMEM((q + SUBLANE, cd), F32),
            pltpu.VMEM((SSM_GROUPS, D_STATE, hpg * SSM_HEAD_DIM), F32),
            pltpu.VMEM((q, d_inner), F32),
        ],
        compiler_params=_params(2),
        name="prompt_ssd",
    )(xbc, z, misc, conv_w, conv_b, dtb_col, alog_col, e_mat, dsk_row, nw_row)


def _mlp_kernel(x_ref, att_ref, y_ref, ga_ref, gs_ref, gt1_ref, sc2_ref, sh2_ref, gt2_ref, n2_ref, fn_ref, wo_ref,
                wu_ref, wd_ref, o_ref, *, ff_chunk):
    merged = _sigmoid(ga_ref[...]) * att_ref[...] + _sigmoid(gs_ref[...]) * y_ref[...]
    x1 = x_ref[...] + gt1_ref[0] * jnp.dot(merged.astype(BF16), wo_ref[...], preferred_element_type=F32)
    h2 = x1 * lax.rsqrt(jnp.mean(x1 * x1, axis=-1, keepdims=True) + EPS) * n2_ref[...]
    hb = (h2 * (1.0 + sc2_ref[0]) + sh2_ref[0]).astype(BF16)
    acc = jnp.zeros(x1.shape, F32)
    for c in range(wu_ref.shape[1] // ff_chunk):
        u = jnp.maximum(jnp.dot(hb, wu_ref[:, c * ff_chunk:(c + 1) * ff_chunk], preferred_element_type=F32), 0.0)
        acc = acc + jnp.dot((u * u).astype(BF16), wd_ref[c * ff_chunk:(c + 1) * ff_chunk, :],
                            preferred_element_type=F32)
    x2 = x1 + gt2_ref[0] * acc
    o_ref[...] = x2 * lax.rsqrt(jnp.mean(x2 * x2, axis=-1, keepdims=True) + EPS) * fn_ref[...]


def _mlp(x2d, att, y, ga, gs, gt1, sc2, sh2, gt2, n2, fn, wo, wu, wd, *, seq, tm, ff_chunk=1024):
    rows, d = x2d.shape
    tb = seq // tm
    r_mod = gt1.shape[1]
    row = pl.BlockSpec((tm, d), lambda i: (i, 0))
    mod = pl.BlockSpec((1, r_mod, d), lambda i: (i // tb, 0, 0))
    return pl.pallas_call(
        functools.partial(_mlp_kernel, ff_chunk=ff_chunk),
        grid=(rows // tm,),
        in_specs=[row, row, row, row, row, mod, mod, mod, mod, _const_spec((1, d)), _const_spec((1, d)),
                  _const_spec(wo.shape), _const_spec(wu.shape), _const_spec(wd.shape)],
        out_specs=row,
        out_shape=jax.ShapeDtypeStruct((rows, d), F32),
        compiler_params=_params(1),
        name="merge_mlp",
    )(x2d, att, y, ga, gs, gt1, sc2, sh2, gt2, n2, fn, wo, wu, wd)


def _page_copy(cache_ref, page_id, buf, slot, m, sem):
    return pltpu.make_async_copy(cache_ref.at[0, page_id], buf.at[slot, m], sem.at[slot])


def _sidx_kernel(pt_ref, iq_ref, cw_ref, iknew_ref, ic_ref, o_ref, buf, sem, ik_scr, *, n_pages, tail):
    b = pl.program_id(0)
    slot = b % 2
    page = buf.shape[3]
    past = n_pages * page
    iq = iq_ref[0]
    cw = cw_ref[0]

    def fetch(sample, sl):
        for m in range(n_pages):
            _page_copy(ic_ref, pt_ref[sample, m], buf, sl, m, sem).start()

    @pl.when(b == 0)
    def _():
        fetch(0, 0)

    @pl.when(b + 1 < pl.num_programs(0))
    def _():
        fetch(b + 1, 1 - slot)

    for m in range(n_pages):
        _page_copy(ic_ref, 0, buf, slot, m, sem).wait()

    def score(ik_t):
        x = _bdot(iq, ik_t)
        return jnp.sum(jnp.maximum(x, 0.0) * cw, axis=0, keepdims=True)

    for m in range(n_pages):
        ik_scr[:, m * LANE:(m + 1) * LANE] = buf[slot, m].astype(BF16)
    o_ref[0, :, 0:past] = score(ik_scr[...])
    s_new = score(iknew_ref[0])[:, 0:1]
    lane = lax.broadcasted_iota(I32, (1, tail), 1)
    o_ref[0, :, past:past + tail] = jnp.where(lane == 0, jnp.broadcast_to(s_new, lane.shape), NEG_INF)


def _sample_index_scores(page_table, iq3, cw3, iknew_t, idx_cache_t, *, tail):
    db, n_pages = page_table.shape
    page = idx_cache_t.shape[3]
    past = n_pages * page
    assert page == LANE

    def per_sample(shape):
        return pl.BlockSpec((1,) + tuple(shape[1:]), lambda b, pt: (b, 0, 0))

    grid_spec = pltpu.PrefetchScalarGridSpec(
        num_scalar_prefetch=1,
        grid=(db,),
        in_specs=[per_sample(iq3.shape), per_sample(cw3.shape), per_sample(iknew_t.shape),
                  pl.BlockSpec(memory_space=pl.ANY)],
        out_specs=per_sample((db, 1, past + tail)),
        scratch_shapes=[pltpu.VMEM((2, n_pages, IDX_DIM, page), F32), pltpu.SemaphoreType.DMA((2,)),
                        pltpu.VMEM((IDX_DIM, past), BF16)],
    )
    return pl.pallas_call(
        functools.partial(_sidx_kernel, n_pages=n_pages, tail=tail),
        grid_spec=grid_spec,
        out_shape=jax.ShapeDtypeStruct((db, 1, past + tail), F32),
        compiler_params=_params(1),
        name="sample_index_scores",
    )(page_table, iq3, cw3, iknew_t, idx_cache_t)


def _ssel_kernel(sc_ref, bias_ref, *, topk):
    sc = sc_ref[...]
    db, lk = sc.shape

    def count_ge(thr):
        return jnp.sum(jnp.where(sc >= thr, 1.0, 0.0), axis=1, keepdims=True)

    tau, _ = _kth_largest([(32, count_ge, True)], (db, 1), topk, jnp.float32(lk))
    need = float(topk) - jnp.sum(jnp.where(sc > tau, 1.0, 0.0), axis=1, keepdims=True)
    ri = lax.broadcasted_iota(I32, (LANE, LANE), 0)
    ci = lax.broadcasted_iota(I32, (LANE, LANE), 1)
    upper = jnp.where(ri <= ci, 1.0, 0.0).astype(BF16)

    def body(j, seen):
        off = pl.multiple_of(j * LANE, LANE)
        sb = sc_ref[:, pl.ds(off, LANE)]
        eq = sb == tau
        prefix = jnp.dot(jnp.where(eq, 1.0, 0.0).astype(BF16), upper, preferred_element_type=F32)
        keep_eq = jnp.where(seen + prefix <= need, 0.0, NEG)
        sel = jnp.where(sb > tau, 0.0, jnp.where(eq, keep_eq, NEG))
        bias_ref[:, pl.ds(off, LANE)] = jnp.where(sb == NEG_INF, NEG, sel)
        return seen + prefix[:, LANE - 1:LANE]

    lax.fori_loop(0, lk // LANE, body, jnp.zeros((db, 1), F32))


def _sample_select(keys2d, *, topk):
    return pl.pallas_call(
        functools.partial(_ssel_kernel, topk=topk),
        out_shape=jax.ShapeDtypeStruct(keys2d.shape, F32),
        compiler_params=pltpu.CompilerParams(vmem_limit_bytes=VMEM_LIMIT),
        name="sample_select",
    )(keys2d)


def _sattn_kernel(pt_ref, q_ref, bias_ref, knew_ref, vnew_ref, kc_ref, vc_ref, o_ref, kbuf, vbuf, ksem, vsem, k_scr,
                  v_scr, m_scr, l_scr, acc_scr, *, pages_per_step, n_steps):
    b = pl.program_id(0)
    width = pages_per_step * kbuf.shape[3]
    q = q_ref[0]
    bias = bias_ref[0]

    def fetch(sample, step, slot):
        for m in range(pages_per_step):
            page_id = pt_ref[sample, step * pages_per_step + m]
            _page_copy(kc_ref, page_id, kbuf, slot, m, ksem).start()
            _page_copy(vc_ref, page_id, vbuf, slot, m, vsem).start()

    def wait(slot):
        for m in range(pages_per_step):
            _page_copy(kc_ref, 0, kbuf, slot, m, ksem).wait()
            _page_copy(vc_ref, 0, vbuf, slot, m, vsem).wait()

    n_slots = kbuf.shape[0]
    ahead = n_slots - 1

    def fetch_ahead(step, lookahead):
        g = b * n_steps + step + lookahead
        sample_off, step_t = divmod(step + lookahead, n_steps)

        @pl.when(b + sample_off < pl.num_programs(0))
        def _():
            fetch(b + sample_off, step_t, lax.rem(g, n_slots))

    @pl.when(b == 0)
    def _():
        for lookahead in range(ahead):
            fetch_ahead(0, lookahead)

    m_scr[...] = jnp.full(m_scr.shape, NEG, F32)
    l_scr[...] = jnp.zeros(l_scr.shape, F32)
    acc_scr[...] = jnp.zeros(acc_scr.shape, F32)

    def update(s, pv_fn):
        m_prev = m_scr[...]
        m_new = jnp.maximum(m_prev, jnp.max(s, axis=1, keepdims=True))
        alpha = jnp.exp(m_prev - m_new)
        p = jnp.exp(s - m_new[:, 0:1])
        m_scr[...] = m_new
        l_scr[...] = alpha * l_scr[...] + jnp.sum(p, axis=1, keepdims=True)
        acc_scr[...] = alpha[:, 0:1] * acc_scr[...] + pv_fn(p)

    for step in range(n_steps):
        slot = lax.rem(b * n_steps + step, n_slots)
        fetch_ahead(step, ahead)
        wait(slot)
        for m in range(pages_per_step):
            k_scr[:, m * LANE:(m + 1) * LANE] = kbuf[slot, m].astype(BF16)
            v_scr[:, m * LANE:(m + 1) * LANE] = vbuf[slot, m].astype(BF16)
        s = jnp.dot(q, k_scr[...], preferred_element_type=F32) + bias[:, step * width:(step + 1) * width]
        update(s, lambda p: _bdot_nt(p, v_scr[...]))

    past = n_steps * width
    s = _bdot(q, knew_ref[0]) + bias[:, past:past + LANE]
    update(s, lambda p: _bdot_nt(p, vnew_ref[0]))
    o_ref[0] = acc_scr[...] / l_scr[:, 0:1]


def _sample_attention(page_table, qmat, bias3, knew_t, vnew_t, cache_kt, cache_vt, *, pages_per_step):
    db, n_pages = page_table.shape
    kv_w, page = cache_kt.shape[2], cache_kt.shape[3]
    nh = qmat.shape[1]
    n_steps = n_pages // pages_per_step
    width = pages_per_step * page
    assert bias3.shape[2] >= n_steps * width + LANE
    slots = KV_FETCH_SLOTS

    def per_sample(shape):
        return pl.BlockSpec((1,) + tuple(shape[1:]), lambda b, pt: (b, 0, 0))

    grid_spec = pltpu.PrefetchScalarGridSpec(
        num_scalar_prefetch=1,
        grid=(db,),
        in_specs=[per_sample(qmat.shape), per_sample(bias3.shape), per_sample(knew_t.shape), per_sample(vnew_t.shape),
                  pl.BlockSpec(memory_space=pl.ANY), pl.BlockSpec(memory_space=pl.ANY)],
        out_specs=per_sample((db, nh, kv_w)),
        scratch_shapes=[
            pltpu.VMEM((slots, pages_per_step, kv_w, page), F32), pltpu.VMEM((slots, pages_per_step, kv_w, page), F32),
            pltpu.SemaphoreType.DMA((slots,)), pltpu.SemaphoreType.DMA((slots,)),
            pltpu.VMEM((kv_w, width), BF16), pltpu.VMEM((kv_w, width), BF16),
            pltpu.VMEM((nh, LANE), F32), pltpu.VMEM((nh, LANE), F32), pltpu.VMEM((nh, kv_w), F32),
        ],
    )
    return pl.pallas_call(
        functools.partial(_sattn_kernel, pages_per_step=pages_per_step, n_steps=n_steps),
        grid_spec=grid_spec,
        out_shape=jax.ShapeDtypeStruct((db, nh, kv_w), F32),
        compiler_params=_params(1),
        name="sample_attention",
    )(page_table, qmat, bias3, knew_t, vnew_t, cache_kt, cache_vt)


def _sssd_kernel(xbc_ref, prev_ref, misc_ref, z_ref, st_ref, cw_ref, cb_ref, dtb_ref, alog_ref, e_ref, dsk_ref,
                 nw_ref, y_ref, sto_ref, *, d_inner, n_heads):
    gn = SSM_GROUPS * D_STATE
    hpg = n_heads // SSM_GROUPS
    gw = hpg * SSM_HEAD_DIM
    u = xbc_ref[0]
    prev = prev_ref[0]
    conv = cb_ref[...] + cw_ref[CONV_W - 1:CONV_W, :] * u
    for w in range(CONV_W - 1):
        conv = conv + cw_ref[w:w + 1, :] * prev[w:w + 1, :]
    act = _silu(conv)
    xs = act[:, :d_inner]
    bm = act[:, d_inner:d_inner + gn]
    cm = act[:, d_inner + gn:]
    dt = _softplus(misc_ref[0] + dtb_ref[...])
    dec = jnp.exp(dt * (-jnp.exp(alog_ref[...])))
    pad6 = jnp.zeros((SUBLANE - 2, LANE), F32)
    ex = _split3_dot(jnp.concatenate([dt, dec, pad6], axis=0), e_ref[...])
    rows = jnp.concatenate([xs * ex[0:1, :], ex[1:2, :], jnp.zeros((LANE - 2, d_inner), F32)], axis=0)
    cols = rows.T
    pad15 = jnp.zeros((2 * SUBLANE - 1, D_STATE), F32)
    ys = []
    for g in range(SSM_GROUPS):
        s0 = st_ref[0, g * hpg:(g + 1) * hpg].reshape(gw, D_STATE)
        cg = cols[g * gw:(g + 1) * gw, :]
        s1 = s0 * cg[:, 1:2] + cg[:, 0:1] * bm[:, g * D_STATE:(g + 1) * D_STATE]
        sto_ref[0, g * hpg:(g + 1) * hpg] = s1.reshape(hpg, SSM_HEAD_DIM, D_STATE)
        c16 = jnp.concatenate([cm[:, g * D_STATE:(g + 1) * D_STATE], pad15], axis=0)
        ys.append(_bdot_nt(c16, s1)[0:1, :])
    y = jnp.concatenate(ys, axis=-1) + dsk_ref[...] * xs
    y_ref[0] = _gated_group_norm(y, z_ref[0], nw_ref[...], d_inner)


def _sample_ssd(xbc3, prev, misc3, z3, state, conv_w, conv_b, dtb_row, alog_row, e_mat, dsk_row, nw_row, *,
                d_inner, n_heads):
    db = xbc3.shape[0]
    cd = xbc3.shape[2]

    def per(shape):
        n = len(shape)
        return pl.BlockSpec((1,) + tuple(shape[1:]), lambda b: (b,) + (0,) * (n - 1))

    return pl.pallas_call(
        functools.partial(_sssd_kernel, d_inner=d_inner, n_heads=n_heads),
        grid=(db,),
        in_specs=[per(xbc3.shape), per(prev.shape), per(misc3.shape), per(z3.shape), per(state.shape),
                  _const_spec(conv_w.shape), _const_spec(conv_b.shape), _const_spec(dtb_row.shape),
                  _const_spec(alog_row.shape), _const_spec(e_mat.shape), _const_spec(dsk_row.shape),
                  _const_spec(nw_row.shape)],
        out_specs=[per((db, 1, d_inner)), per(state.shape)],
        out_shape=[jax.ShapeDtypeStruct((db, 1, d_inner), F32), jax.ShapeDtypeStruct(state.shape, F32)],
        compiler_params=_params(1),
        name="sample_ssd",
    )(xbc3, prev, misc3, z3, state, conv_w, conv_b, dtb_row, alog_row, e_mat, dsk_row, nw_row)


def _rope_tables(pos):
    inv = ROPE_THETA ** (-jnp.arange(ROT_HALF, dtype=F32) * 2.0 / (2 * ROT_HALF))
    ang = pos.astype(F32)[:, None] * inv[None, :]
    cos, sin = jnp.cos(ang), jnp.sin(ang)
    n = pos.shape[0]
    one = jnp.ones((n, HEAD_DIM - 2 * ROT_HALF), F32)
    zero = jnp.zeros((n, HEAD_DIM - 2 * ROT_HALF), F32)
    z8 = jnp.zeros((n, ROT_HALF), F32)
    c_head = jnp.concatenate([cos, cos, one], axis=1)
    s1_head = jnp.concatenate([-sin, z8, zero], axis=1)
    s2_head = jnp.concatenate([z8, sin, zero], axis=1)
    rep = LANE // HEAD_DIM
    return jnp.tile(c_head, (1, rep)), jnp.tile(s1_head, (1, rep)), jnp.tile(s2_head, (1, rep))


def kernel(x_prompt, x_sample, cache_k, cache_v, cache_idx_k, state_conv, state_ssm, page_table, c_prompt, c_sample,
           w_ada, b_ada, norm1_w, w_in, conv_w, conv_b, dt_bias, a_log, d_skip, ssm_norm_w, w_out, norm2_w, w_up,
           w_down, final_norm_w):
    nb, seq, d = x_prompt.shape
    db, dec_seq, _ = x_sample.shape
    depth = w_in.shape[0]
    assert depth == 1 and dec_seq == 1 and seq % SSD_CHUNK == 0
    n_heads = d // HEAD_DIM
    n_ssm_heads = a_log.shape[1]
    d_inner = n_ssm_heads * SSM_HEAD_DIM
    conv_dim = conv_w.shape[2]
    kv_w = N_KV_HEADS * HEAD_DIM
    iq_w = IDX_HEADS * IDX_DIM
    n_pages, page = page_table.shape[1], cache_k.shape[2]
    past = n_pages * page

    wi = w_in[0]
    offs = np.cumsum([0, d, kv_w, kv_w, iq_w, IDX_DIM, IDX_HEADS, d_inner, conv_dim, n_ssm_heads, d, d])
    misc_w = jnp.concatenate(
        [wi[:, offs[4]:offs[6]], wi[:, offs[8]:offs[9]],
         jnp.zeros((d, LANE - IDX_DIM - IDX_HEADS - n_ssm_heads), F32)], axis=1)
    w_parts = tuple(w.astype(BF16) for w in (wi[:, :offs[4]], misc_w, wi[:, offs[6]:offs[8]], wi[:, offs[9]:]))
    wo, wu, wd = w_out[0].astype(BF16), w_up[0].astype(BF16), w_down[0].astype(BF16)
    nw1, nw2, fnw = norm1_w[0][None, :], norm2_w[0][None, :], final_norm_w[None, :]
    head_lane = jnp.zeros((LANE,), F32)
    dtb_row = head_lane.at[DT_OFF:DT_OFF + n_ssm_heads].set(dt_bias[0])[None, :]
    alog_row = head_lane.at[DT_OFF:DT_OFF + n_ssm_heads].set(a_log[0])[None, :]
    e_mat = jnp.zeros((LANE, d_inner), F32).at[DT_OFF:DT_OFF + n_ssm_heads].set(
        jnp.repeat(jnp.eye(n_ssm_heads, dtype=F32), SSM_HEAD_DIM, axis=1)).astype(BF16)
    dsk_row = jnp.repeat(d_skip[0], SSM_HEAD_DIM)[None, :]
    snw_row = ssm_norm_w[0][None, :]
    cw2, cb2 = conv_w[0], conv_b[0][None, :]

    n_mod = nb + db
    r_mod = -(-n_mod // SUBLANE) * SUBLANE
    c_all = jnp.concatenate([c_prompt, c_sample, jnp.zeros((r_mod - n_mod, d), F32)], axis=0)
    mod = _ada(c_all, w_ada[0], b_ada[0][None, :])
    sh1, sc1, gt1, sh2, sc2, gt2 = [mod[:, k * d:(k + 1) * d] for k in range(6)]

    def pmod(a):
        return a[:nb].reshape(nb, 1, d)

    def smod(a):
        return a[nb:nb + db].reshape(1, db, d)

    xp = x_prompt.reshape(nb * seq, d)
    cos_p, s1_p, s2_p = _rope_tables(jnp.arange(seq))
    (q_hm, kt32, kt, vt32, ve, iqt, ikt32, ikr, cwt, misc_p, z_p, xbc_p, ga_p, gs_p) = _inproj(
        xp, pmod(sc1), pmod(sh1), nw1, w_parts, cos_p, s1_p, s2_p, n_batch=nb, seq=seq, tm=256, prompt=True,
        d_inner=d_inner, conv_dim=conv_dim)
    topk_p = min(TOPK_MAX, seq // 4)
    att_p = _prompt_attention(q_hm, iqt, cwt, kt, ve, ikr, tq=256, topk=topk_p)
    y_p, ssm_p = _prompt_ssd(xbc_p, z_p, misc_p, cw2, cb2, dtb_row.T, alog_row.T, e_mat, dsk_row, snw_row,
                             n_batch=nb, seq=seq, d_inner=d_inner, n_heads=n_ssm_heads)
    out_p = _mlp(xp, att_p, y_p, ga_p, gs_p, pmod(gt1), pmod(sc2), pmod(sh2), pmod(gt2), nw2, fnw, wo, wu, wd,
                 seq=seq, tm=256)

    xs2 = x_sample.reshape(db, d)
    cos_s, s1_s, s2_s = _rope_tables(jnp.full((1,), past, jnp.int32))
    (q_s, k_s, v_s, iq_s, misc_s, z_s, xbc_s, ga_s, gs_s) = _inproj(
        xs2, smod(sc1), smod(sh1), nw1, w_parts, cos_s, s1_s, s2_s, n_batch=1, seq=db, tm=db, prompt=False,
        d_inner=d_inner, conv_dim=conv_dim)
    pps_kv = _largest_divisor(n_pages, KV_PAGES_PER_STEP)
    cw_s = (misc_s[:, IW_OFF:IW_OFF + IDX_HEADS] * (IDX_HEADS ** -0.5) * (IDX_DIM ** -0.5)).reshape(db, IDX_HEADS, 1)

    def as_page_t(a):
        return jnp.pad(a[:, :, None], ((0, 0), (0, 0), (0, page - 1)))

    idx_cache_t = jnp.transpose(cache_idx_k, (0, 1, 3, 2))
    kv_shape_t = (depth, cache_k.shape[1], kv_w, page)
    cache_kt = jnp.transpose(cache_k, (0, 1, 3, 4, 2)).reshape(kv_shape_t)
    cache_vt = jnp.transpose(cache_v, (0, 1, 3, 4, 2)).reshape(kv_shape_t)

    lk = past + pps_kv * page
    sc_s = _sample_index_scores(page_table, iq_s.reshape(db, IDX_HEADS, IDX_DIM), cw_s,
                                as_page_t(misc_s[:, :IDX_DIM]), idx_cache_t, tail=pps_kv * page)
    topk_s = min(TOPK_MAX, (past + 1) // 4)
    bias_s = _sample_select(sc_s.reshape(db, lk), topk=topk_s).reshape(db, 1, lk)
    hpg = n_heads // N_KV_HEADS
    q5 = q_s.reshape(db, N_KV_HEADS, hpg, 1, HEAD_DIM)
    eye = jnp.eye(N_KV_HEADS, dtype=F32)[None, :, None, :, None]
    qmat = (q5 * eye).reshape(db, n_heads, kv_w).astype(BF16)
    acc_s = _sample_attention(page_table, qmat, bias_s, as_page_t(k_s), as_page_t(v_s), cache_kt, cache_vt,
                              pages_per_step=pps_kv)
    a5 = acc_s.reshape(db, N_KV_HEADS, hpg, N_KV_HEADS, HEAD_DIM)
    att_s = jnp.einsum('bghgd->bghd', a5).reshape(db, d)
    y_s, ssm_s = _sample_ssd(xbc_s.reshape(db, 1, conv_dim), state_conv[0], misc_s.reshape(db, 1, LANE),
                             z_s.reshape(db, 1, d_inner), state_ssm[0], cw2, cb2, dtb_row, alog_row, e_mat, dsk_row,
                             snw_row, d_inner=d_inner, n_heads=n_ssm_heads)
    out_s = _mlp(xs2, att_s, y_s.reshape(db, d_inner), ga_s, gs_s, smod(gt1), smod(sc2), smod(sh2), smod(gt2), nw2,
                 fnw, wo, wu, wd, seq=db, tm=db)

    conv_p = xbc_p.reshape(nb, seq, conv_dim)[:, seq - (CONV_W - 1):, :]
    conv_s = jnp.concatenate([state_conv[0][:, 1:, :], xbc_s[:, None, :]], axis=1)
    def kv_out(a_t):
        return jnp.transpose(a_t.reshape(nb, N_KV_HEADS, HEAD_DIM, seq), (0, 3, 1, 2))[None]

    return (
        out_p.reshape(nb, seq, d),
        out_s.reshape(db, 1, d),
        kv_out(kt32),
        kv_out(vt32),
        jnp.transpose(ikt32, (0, 2, 1))[None],
        conv_p[None],
        ssm_p[None],
        k_s.reshape(1, db, 1, N_KV_HEADS, HEAD_DIM),
        v_s.reshape(1, db, 1, N_KV_HEADS, HEAD_DIM),
        misc_s[:, :IDX_DIM].reshape(1, db, 1, IDX_DIM),
        conv_s[None],
        ssm_s[None],
    )
```

```python
import functools

import jax
import jax.numpy as jnp
import numpy as np
from jax import lax
from jax.experimental import pallas as pl
from jax.experimental.pallas import tpu as pltpu

F32 = jnp.float32
BF16 = jnp.bfloat16
I32 = jnp.int32

HEAD_DIM = 64
N_KV_HEADS = 4
ROT_HALF = 8
ROPE_THETA = 500000.0
IDX_HEADS = 8
IDX_DIM = 64
TOPK_MAX = 256
SSM_HEAD_DIM = 64
SSM_GROUPS = 4
D_STATE = 128
CONV_W = 4
SSD_CHUNK = 128
EPS = 1e-6

LANE = 128
SUBLANE = 8
VMEM_LIMIT = 56 * 1024 * 1024
KV_PAGES_PER_STEP = 16
KV_FETCH_SLOTS = 3
COUNT_STRIP = 64
SOFTMAX_KEY_BLOCKS = 2
EARLY_EXIT_FROM_BIT = 24

IW_OFF = IDX_DIM
DT_OFF = IDX_DIM + IDX_HEADS
INT_MIN = -(2 ** 31)
NEG = -1e30
NEG_INF = float("-inf")
F32_LOWEST = float(np.finfo(np.float32).min)
LOG2E = 1.4426950408889634
BF16_STEP_DOWN = 1.25 * 2.0 ** -8


def _sigmoid(x):
    return 1.0 / (1.0 + jnp.exp(-x))


def _silu(x):
    return x * _sigmoid(x)


def _softplus(x):
    return jnp.maximum(x, 0.0) + jnp.log1p(jnp.exp(-jnp.abs(x)))


def _bdot(a, b):
    return jnp.dot(a.astype(BF16), b.astype(BF16), preferred_element_type=F32)


def _bdot_nt(a, b):
    return lax.dot_general(a.astype(BF16), b.astype(BF16), (((1,), (1,)), ((), ())), preferred_element_type=F32)


def _split3_dot(v, e):
    hi = v.astype(BF16)
    r1 = v - hi.astype(F32)
    mid = r1.astype(BF16)
    lo = (r1 - mid.astype(F32)).astype(BF16)
    d = functools.partial(jnp.dot, preferred_element_type=F32)
    return d(hi, e) + d(mid, e) + d(lo, e)


def _f32_at_rank(u):
    key = u ^ jnp.int32(INT_MIN)
    bits = jnp.where(key < 0, jnp.int32(INT_MIN) - key, key)
    return lax.bitcast_convert_type(bits, F32)


def _floor_to_bf16(x):
    r = x.astype(BF16).astype(F32)
    below = r - jnp.abs(r) * BF16_STEP_DOWN
    return jnp.where(r > x, below, r).astype(BF16)


def _kth_largest(stages, shape, topk, n_total):
    def step(t, tau_u, cnt_tau, count_ge):
        cand_u = tau_u | jnp.left_shift(jnp.int32(1), 31 - t)
        cnt = count_ge(_f32_at_rank(cand_u))
        take = cnt >= float(topk)
        return jnp.where(take, cand_u, tau_u), jnp.where(take, cnt, cnt_tau)

    t0 = 0
    tau_u, cnt_tau = jnp.zeros(shape, I32), jnp.zeros(shape, F32) + n_total
    for end_bit, count_ge, early_exit in stages:
        if early_exit:
            def cond(c, end_bit=end_bit):
                return jnp.logical_and(c[0] < end_bit, c[3] > 0)

            def body(c, count_ge=count_ge):
                tau_n, cnt_n = step(c[0], c[1], c[2], count_ge)
                return c[0] + 1, tau_n, cnt_n, (jnp.max(cnt_n) > float(topk)).astype(I32)

            init = (jnp.int32(t0), tau_u, cnt_tau, (jnp.max(cnt_tau) > float(topk)).astype(I32))
            _, tau_u, cnt_tau, _ = lax.while_loop(cond, body, init)
        else:
            tau_u, cnt_tau = lax.fori_loop(
                t0, end_bit, lambda t, c, count_ge=count_ge: step(t, c[0], c[1], count_ge), (tau_u, cnt_tau))
        t0 = end_bit
    return jnp.where(tau_u == 0, NEG_INF, _f32_at_rank(tau_u)), cnt_tau


def _largest_divisor(n, cap):
    return max(k for k in range(1, cap + 1) if n % k == 0)


def _const_spec(shape):
    n = len(shape)
    return pl.BlockSpec(shape, lambda *a: (0,) * n, pipeline_mode=pl.Buffered(1))


def _params(n_axes):
    return pltpu.CompilerParams(dimension_semantics=("arbitrary",) * n_axes, vmem_limit_bytes=VMEM_LIMIT)


def _ada_kernel(c_ref, w_ref, b_ref, o_ref):
    s = _silu(c_ref[...])
    s_hi = s.astype(BF16)
    s_lo = (s - s_hi.astype(F32)).astype(BF16)
    w = w_ref[...]
    w_hi = w.astype(BF16)
    w_lo = (w - w_hi.astype(F32)).astype(BF16)
    d = functools.partial(jnp.dot, preferred_element_type=F32)
    o_ref[...] = d(s_hi, w_hi) + d(s_lo, w_hi) + d(s_hi, w_lo) + b_ref[...]


def _ada(c_all, w_ada, b_ada, tn=1024):
    r, d = c_all.shape
    n = w_ada.shape[1]
    return pl.pallas_call(
        _ada_kernel,
        grid=(n // tn,),
        in_specs=[
            pl.BlockSpec((r, d), lambda j: (0, 0)),
            pl.BlockSpec((d, tn), lambda j: (0, j)),
            pl.BlockSpec((1, tn), lambda j: (0, j)),
        ],
        out_specs=pl.BlockSpec((r, tn), lambda j: (0, j)),
        out_shape=jax.ShapeDtypeStruct((r, n), F32),
        compiler_params=_params(1),
        name="ada",
    )(c_all, w_ada, b_ada)


def _rope_chunk(c, cos, s1, s2):
    return c * cos + pltpu.roll(c, LANE - ROT_HALF, 1) * s1 + pltpu.roll(c, ROT_HALF, 1) * s2


def _inproj_kernel(x_ref, sc_ref, sh_ref, nw_ref, wa_ref, wm_ref, wb_ref, wc_ref, cos_ref, s1_ref, s2_ref, *outs,
                   d_model, d_inner, conv_dim, prompt):
    x = x_ref[...]
    h = x * lax.rsqrt(jnp.mean(x * x, axis=-1, keepdims=True) + EPS) * nw_ref[...]
    hb = (h * (1.0 + sc_ref[0]) + sh_ref[0]).astype(BF16)
    cos, s1, s2 = cos_ref[...], s1_ref[...], s2_ref[...]
    kv_w = N_KV_HEADS * HEAD_DIM
    iq_w = IDX_HEADS * IDX_DIM
    q0 = (wa_ref, 0)
    k0 = (wa_ref, d_model)
    v0 = (wa_ref, d_model + kv_w)
    iq0 = (wa_ref, d_model + 2 * kv_w)
    m0 = (wm_ref, 0)
    z0 = (wb_ref, 0)
    x0 = (wb_ref, d_inner)
    ga0 = (wc_ref, 0)
    gs0 = (wc_ref, d_model)

    def proj(group, width):
        w_ref, lo = group
        return jnp.dot(hb, w_ref[:, lo:lo + width], preferred_element_type=F32)

    if prompt:
        (q_ref, kt32_ref, kt_ref, vt32_ref, ve_ref, iqt_ref, ikt32_ref, ikr_ref, cwt_ref, misc_ref, z_ref, xbc_ref,
         ga_ref, gs_ref) = outs
    else:
        q_ref, k_ref, v_ref, iq_ref, misc_ref, z_ref, xbc_ref, ga_ref, gs_ref = outs

    q = proj(q0, d_model)
    q_scale = HEAD_DIM ** -0.5 * (LOG2E if prompt else 1.0)
    for c in range(d_model // LANE):
        qc = _rope_chunk(q[:, c * LANE:(c + 1) * LANE], cos, s1, s2) * q_scale
        if prompt:
            q_ref[0, 2 * c] = qc[:, :HEAD_DIM].astype(BF16)
            q_ref[0, 2 * c + 1] = qc[:, HEAD_DIM:].astype(BF16)
        else:
            q_ref[:, c * LANE:(c + 1) * LANE] = qc

    kk = proj(k0, kv_w)
    for c in range(kv_w // LANE):
        kc = _rope_chunk(kk[:, c * LANE:(c + 1) * LANE], cos, s1, s2)
        if prompt:
            kc_t = kc.T
            kt32_ref[0, c * LANE:(c + 1) * LANE, :] = kc_t
            kt_ref[0, c * LANE:(c + 1) * LANE, :] = kc_t.astype(BF16)
        else:
            k_ref[:, c * LANE:(c + 1) * LANE] = kc

    vv = proj(v0, kv_w)
    if not prompt:
        v_ref[...] = vv
    else:
        for c in range(kv_w // LANE):
            vt32_ref[0, c * LANE:(c + 1) * LANE, :] = vv[:, c * LANE:(c + 1) * LANE].T
        lane = lax.broadcasted_iota(I32, (vv.shape[0], LANE), 1)
        for g in range(N_KV_HEADS):
            vc = vv[:, (g // 2) * LANE:(g // 2 + 1) * LANE]
            if g % 2 == 1:
                vc = pltpu.roll(vc, HEAD_DIM, 1)
            ve = jnp.where(lane < HEAD_DIM, vc, jnp.where(lane == HEAD_DIM, 1.0, 0.0))
            ve_ref[0, g] = ve.astype(BF16)

    iq = proj(iq0, iq_w)
    for c in range(iq_w // LANE):
        ic = _rope_chunk(iq[:, c * LANE:(c + 1) * LANE], cos, s1, s2)
        if prompt:
            iqt_ref[0, c * LANE:(c + 1) * LANE, :] = ic.T.astype(BF16)
        else:
            iq_ref[:, c * LANE:(c + 1) * LANE] = ic.astype(BF16)

    mm = proj(m0, LANE)
    lane = lax.broadcasted_iota(I32, mm.shape, 1)
    mm = jnp.where(lane < IDX_DIM, _rope_chunk(mm, cos, s1, s2), mm)
    misc_ref[...] = mm
    if prompt:
        mm_t = mm.T
        ikt32_ref[0] = mm_t[:IDX_DIM, :]
        ikr_ref[...] = mm[:, :IDX_DIM].astype(BF16)
        cwt_ref[0] = mm_t[IW_OFF:IW_OFF + IDX_HEADS, :] * (IDX_HEADS ** -0.5) * (IDX_DIM ** -0.5)

    z_ref[...] = proj(z0, d_inner)
    xbc_ref[...] = proj(x0, conv_dim)
    ga_ref[...] = proj(ga0, d_model)
    gs_ref[...] = proj(gs0, d_model)


def _inproj(x2d, sc, sh, nw, weights, cos, s1, s2, *, n_batch, seq, tm, prompt, d_inner, conv_dim):
    rows, d = x2d.shape
    tb = seq // tm
    r_mod = sc.shape[1]
    kv_w = N_KV_HEADS * HEAD_DIM
    iq_w = IDX_HEADS * IDX_DIM
    n_heads = d // HEAD_DIM
    if cos.shape[0] == 1:
        tab = pl.BlockSpec((1, LANE), lambda i: (0, 0))
    else:
        tab = pl.BlockSpec((tm, LANE), lambda i: (i % tb, 0))
    mod = pl.BlockSpec((1, r_mod, d), lambda i: (i // tb, 0, 0))
    in_specs = [pl.BlockSpec((tm, d), lambda i: (i, 0)), mod, mod, _const_spec((1, d))]
    in_specs += [_const_spec(w.shape) for w in weights] + [tab, tab, tab]

    def rowspec(w):
        return pl.BlockSpec((tm, w), lambda i: (i, 0))

    def sds(shape, dt):
        return jax.ShapeDtypeStruct(shape, dt)

    if prompt:
        def tspec(w):
            return pl.BlockSpec((1, w, tm), lambda i: (i // tb, 0, i % tb))

        out_specs = [
            pl.BlockSpec((1, n_heads, tm, HEAD_DIM), lambda i: (i // tb, 0, i % tb, 0)),
            tspec(kv_w), tspec(kv_w), tspec(kv_w),
            pl.BlockSpec((1, N_KV_HEADS, tm, LANE), lambda i: (i // tb, 0, i % tb, 0)),
            tspec(iq_w), tspec(IDX_DIM), rowspec(IDX_DIM), tspec(IDX_HEADS),
            rowspec(LANE), rowspec(d_inner), rowspec(conv_dim), rowspec(d), rowspec(d),
        ]
        out_shape = [
            sds((n_batch, n_heads, seq, HEAD_DIM), BF16), sds((n_batch, kv_w, seq), F32),
            sds((n_batch, kv_w, seq), BF16), sds((n_batch, kv_w, seq), F32),
            sds((n_batch, N_KV_HEADS, seq, LANE), BF16), sds((n_batch, iq_w, seq), BF16),
            sds((n_batch, IDX_DIM, seq), F32), sds((rows, IDX_DIM), BF16), sds((n_batch, IDX_HEADS, seq), F32),
            sds((rows, LANE), F32), sds((rows, d_inner), F32), sds((rows, conv_dim), F32), sds((rows, d), F32),
            sds((rows, d), F32),
        ]
    else:
        out_specs = [rowspec(d), rowspec(kv_w), rowspec(kv_w), rowspec(iq_w), rowspec(LANE), rowspec(d_inner),
                     rowspec(conv_dim), rowspec(d), rowspec(d)]
        out_shape = [sds((rows, d), F32), sds((rows, kv_w), F32), sds((rows, kv_w), F32), sds((rows, iq_w), BF16),
                     sds((rows, LANE), F32), sds((rows, d_inner), F32), sds((rows, conv_dim), F32),
                     sds((rows, d), F32), sds((rows, d), F32)]
    return pl.pallas_call(
        functools.partial(_inproj_kernel, d_model=d, d_inner=d_inner, conv_dim=conv_dim, prompt=prompt),
        grid=(rows // tm,),
        in_specs=in_specs,
        out_specs=out_specs,
        out_shape=out_shape,
        compiler_params=_params(1),
        name="inproj_prompt" if prompt else "inproj_sample",
    )(x2d, sc, sh, nw, *weights, cos, s1, s2)


def _attn_kernel(q_ref, iqt_ref, cwt_ref, kt_ref, ve_ref, ikr_ref, o_ref, sc_scr, sc16_scr, m_scr, acc_scr, *, tq,
                 topk):
    i = pl.program_id(1)
    nblk = i + 1
    tk = tq
    tkc = SOFTMAX_KEY_BLOCKS * tk
    hpg = q_ref.shape[1] // N_KV_HEADS
    cw = cwt_ref[0]
    kidx = lax.broadcasted_iota(I32, (tk, tq), 0)
    qidx = lax.broadcasted_iota(I32, (tk, tq), 1)

    def store_scores(j, diagonal):
        off = pl.multiple_of(j * tk, tk)
        ikb = ikr_ref[pl.ds(off, tk), :]
        acc = jnp.zeros((tk, tq), F32)
        for h in range(IDX_HEADS):
            x = jnp.dot(ikb, iqt_ref[0, h * IDX_DIM:(h + 1) * IDX_DIM, :], preferred_element_type=F32)
            acc = acc + jnp.maximum(x, 0.0) * cw[h:h + 1, :]
        sc = jnp.where(kidx <= qidx, acc, NEG_INF) if diagonal else acc
        sc_scr[pl.ds(off, tk), :] = sc
        sc16_scr[pl.ds(off, tk), :] = _floor_to_bf16(sc)

    def full_block(j, carry):
        store_scores(j, False)
        return carry

    lax.fori_loop(0, i, full_block, 0)
    store_scores(i, True)

    def count(cmp, thr):
        def body(j, acc):
            off = pl.multiple_of(j * tk, tk)
            for r in range(tk // COUNT_STRIP):
                sb = sc_scr[pl.ds(off + r * COUNT_STRIP, COUNT_STRIP), :]
                acc = acc + jnp.where(cmp(sb, thr), 1.0, 0.0)
            return acc

        acc = lax.fori_loop(0, nblk, body, jnp.zeros((COUNT_STRIP, tq), F32))
        return jnp.sum(acc, axis=0, keepdims=True)

    def count16(thr):
        thr16 = thr.astype(BF16)
        one, zero = jnp.ones((), BF16), jnp.zeros((), BF16)

        def body(j, acc):
            off = pl.multiple_of(j * tk, tk)
            for r in range(tk // COUNT_STRIP):
                sb = sc16_scr[pl.ds(off + r * COUNT_STRIP, COUNT_STRIP), :]
                acc = acc + jnp.where(sb >= thr16, one, zero)
            return acc

        acc = lax.fori_loop(0, nblk, body, jnp.zeros((COUNT_STRIP, tq), BF16))
        return jnp.sum(acc.astype(F32), axis=0, keepdims=True)

    count_ge = functools.partial(count, lax.ge)
    tau, cnt_tau = _kth_largest(
        [(16, count16, False), (EARLY_EXIT_FROM_BIT, count_ge, False), (32, count_ge, True)], (1, tq), topk,
        (nblk * tk).astype(F32))

    any_tie = jnp.max(jnp.where(cnt_tau > float(topk), 1.0, 0.0)) > 0.5

    @pl.when(any_tie)
    def _():
        need = float(topk) - count(lax.gt, tau)
        lower = jnp.where(qidx <= kidx, 1.0, 0.0).astype(BF16)

        def body(j, seen):
            off = pl.multiple_of(j * tk, tk)
            sb = sc_scr[pl.ds(off, tk), :]
            eq = sb == tau
            prefix = jnp.dot(lower, jnp.where(eq, 1.0, 0.0).astype(BF16), preferred_element_type=F32)
            late = jnp.where(seen + prefix > need, NEG_INF, sb)
            sc_scr[pl.ds(off, tk), :] = jnp.where(eq, late, sb)
            return seen + prefix[tk - 1:tk, :]

        lax.fori_loop(0, nblk, body, jnp.zeros((1, tq), F32))

    tau_c = jnp.maximum(tau, F32_LOWEST)
    m_scr[...] = jnp.full(m_scr.shape, NEG, F32)
    acc_scr[...] = jnp.zeros(acc_scr.shape, F32)

    def softmax_step(off, n_kb):
        w = n_kb * tk
        bias = jnp.concatenate(
            [jnp.where(sc_scr[pl.ds(off + b * tk, tk), :] >= tau_c, 0.0, NEG).T for b in range(n_kb)], axis=1)
        for g in range(N_KV_HEADS):
            qg = q_ref[0, g * hpg:(g + 1) * hpg].reshape(hpg * tq, HEAD_DIM)
            s = jnp.dot(qg, kt_ref[0, g * HEAD_DIM:(g + 1) * HEAD_DIM, pl.ds(off, w)],
                        preferred_element_type=F32)
            s = s.reshape(hpg, tq, w) + bias[None]
            m_prev = m_scr[g]
            m_new = jnp.maximum(m_prev, jnp.max(s, axis=-1, keepdims=True))
            alpha = jnp.exp2(m_prev - m_new)
            p = jnp.exp2(s - jnp.concatenate([m_new] * (w // LANE), axis=-1))
            m_scr[g] = m_new
            pv = jnp.dot(p.reshape(hpg * tq, w).astype(BF16), ve_ref[0, g, pl.ds(off, w), :],
                         preferred_element_type=F32)
            acc_scr[g] = alpha * acc_scr[g] + pv.reshape(hpg, tq, LANE)

    def full_step(jc, carry):
        softmax_step(pl.multiple_of(jc * tkc, tkc), SOFTMAX_KEY_BLOCKS)
        return carry

    def tail_step(j, carry):
        softmax_step(pl.multiple_of(j * tk, tk), 1)
        return carry

    n_full = nblk // SOFTMAX_KEY_BLOCKS
    lax.fori_loop(0, n_full, full_step, 0)
    lax.fori_loop(n_full * SOFTMAX_KEY_BLOCKS, nblk, tail_step, 0)

    for g in range(N_KV_HEADS):
        for hh in range(hpg):
            a = acc_scr[g, hh]
            hq = g * hpg + hh
            o_ref[:, hq * HEAD_DIM:(hq + 1) * HEAD_DIM] = a[:, :HEAD_DIM] / a[:, HEAD_DIM:HEAD_DIM + 1]


def _prompt_attention(q_hm, iqt, cwt, kt, ve, ikr, *, tq, topk):
    nb, nh, seq, _ = q_hm.shape
    hpg = nh // N_KV_HEADS
    tb = seq // tq
    kv_w = N_KV_HEADS * HEAD_DIM

    def resident(shape, imap):
        return pl.BlockSpec(shape, imap, pipeline_mode=pl.Buffered(1))

    return pl.pallas_call(
        functools.partial(_attn_kernel, tq=tq, topk=topk),
        grid=(nb, tb),
        in_specs=[
            pl.BlockSpec((1, nh, tq, HEAD_DIM), lambda b, i: (b, 0, i, 0)),
            pl.BlockSpec((1, IDX_HEADS * IDX_DIM, tq), lambda b, i: (b, 0, i)),
            pl.BlockSpec((1, IDX_HEADS, tq), lambda b, i: (b, 0, i)),
            resident((1, kv_w, seq), lambda b, i: (b, 0, 0)),
            resident((1, N_KV_HEADS, seq, LANE), lambda b, i: (b, 0, 0, 0)),
            resident((seq, IDX_DIM), lambda b, i: (b, 0)),
        ],
        out_specs=pl.BlockSpec((tq, nh * HEAD_DIM), lambda b, i: (b * tb + i, 0)),
        out_shape=jax.ShapeDtypeStruct((nb * seq, nh * HEAD_DIM), F32),
        scratch_shapes=[
            pltpu.VMEM((seq, tq), F32),
            pltpu.VMEM((seq, tq), BF16),
            pltpu.VMEM((N_KV_HEADS, hpg, tq, LANE), F32),
            pltpu.VMEM((N_KV_HEADS, hpg, tq, LANE), F32),
        ],
        compiler_params=_params(2),
        name="prompt_attention",
    )(q_hm, iqt, cwt, kt, ve, ikr)


def _gated_group_norm(y, z, nw, d_inner):
    g = y * _silu(z)
    gw = d_inner // SSM_GROUPS
    outs = []
    for k in range(SSM_GROUPS):
        gg = g[:, k * gw:(k + 1) * gw]
        outs.append(gg * lax.rsqrt(jnp.mean(gg * gg, axis=-1, keepdims=True) + EPS))
    return jnp.concatenate(outs, axis=-1) * nw


def _ssd_kernel(xbc_ref, z_ref, misc_ref, cw_ref, cb_ref, dtb_ref, alog_ref, e_ref, dsk_ref, nw_ref, y_ref, st_ref,
                ext_scr, st_scr, y_scr, *, d_inner, n_heads):
    c = pl.program_id(1)
    q = xbc_ref.shape[0]
    gn = SSM_GROUPS * D_STATE
    hpg = n_heads // SSM_GROUPS
    gw = hpg * SSM_HEAD_DIM

    @pl.when(c == 0)
    def _():
        ext_scr[0:SUBLANE, :] = jnp.zeros((SUBLANE, ext_scr.shape[1]), F32)
        st_scr[...] = jnp.zeros(st_scr.shape, F32)

    u = xbc_ref[...]
    ext_scr[SUBLANE:, :] = u
    conv = cb_ref[...] + cw_ref[CONV_W - 1:CONV_W, :] * u
    for w in range(CONV_W - 1):
        lo = SUBLANE - (CONV_W - 1) + w
        conv = conv + cw_ref[w:w + 1, :] * ext_scr[lo:lo + q, :]
    ext_scr[0:SUBLANE, :] = u[q - SUBLANE:, :]
    act = _silu(conv)
    xs = act[:, :d_inner]
    bm = act[:, d_inner:d_inner + gn]
    cm = act[:, d_inner + gn:]

    dt_t = _softplus(misc_ref[...].T + dtb_ref[...])
    da_t = dt_t * (-jnp.exp(alog_ref[...]))
    ri = lax.broadcasted_iota(I32, (q, q), 0)
    ci = lax.broadcasted_iota(I32, (q, q), 1)
    cs_t = _split3_dot(da_t, jnp.where(ri <= ci, 1.0, 0.0).astype(BF16))
    cs = cs_t.T
    dt = dt_t.T
    e = e_ref[...]
    ecs_x = _split3_dot(jnp.exp(cs), e)
    wst_x = _split3_dot(dt * jnp.exp(cs[q - 1:q, :] - cs), e)
    tri = ri >= ci

    for g in range(SSM_GROUPS):
        bg = bm[:, g * D_STATE:(g + 1) * D_STATE]
        cg = cm[:, g * D_STATE:(g + 1) * D_STATE].astype(BF16)
        bg_t = bg.T.astype(BF16)
        cb = jnp.dot(cg, bg_t, preferred_element_type=F32)
        s_t = st_scr[g]
        y_off = jnp.dot(cg, s_t.astype(BF16), preferred_element_type=F32)
        xg = xs[:, g * gw:(g + 1) * gw]
        y_g = y_off * ecs_x[:, g * gw:(g + 1) * gw]
        pieces = []
        for hh in range(hpg):
            idx = DT_OFF + g * hpg + hh
            seg = cs[:, idx:idx + 1] - cs_t[idx:idx + 1, :]
            mm = cb * jnp.exp(jnp.where(tri, seg, NEG)) * dt_t[idx:idx + 1, :]
            pieces.append(_bdot(mm, xg[:, hh * SSM_HEAD_DIM:(hh + 1) * SSM_HEAD_DIM]))
        y_scr[:, g * gw:(g + 1) * gw] = y_g + jnp.concatenate(pieces, axis=-1)
        wg = (xg * wst_x[:, g * gw:(g + 1) * gw]).astype(BF16)
        st_scr[g] = s_t * ecs_x[q - 1:q, g * gw:(g + 1) * gw] + jnp.dot(bg_t, wg, preferred_element_type=F32)

    y = y_scr[...] + dsk_ref[...] * xs
    y_ref[...] = _gated_group_norm(y, z_ref[...], nw_ref[...], d_inner)

    @pl.when(c == pl.num_programs(1) - 1)
    def _():
        for g in range(SSM_GROUPS):
            st_ref[0, g * hpg:(g + 1) * hpg] = st_scr[g].T.reshape(hpg, SSM_HEAD_DIM, D_STATE)


def _prompt_ssd(xbc, z, misc, conv_w, conv_b, dtb_col, alog_col, e_mat, dsk_row, nw_row, *, n_batch, seq, d_inner,
                n_heads):
    q = SSD_CHUNK
    nc = seq // q
    cd = xbc.shape[1]
    hpg = n_heads // SSM_GROUPS
    return pl.pallas_call(
        functools.partial(_ssd_kernel, d_inner=d_inner, n_heads=n_heads),
        grid=(n_batch, nc),
        in_specs=[
            pl.BlockSpec((q, cd), lambda b, c: (b * nc + c, 0)),
            pl.BlockSpec((q, d_inner), lambda b, c: (b * nc + c, 0)),
            pl.BlockSpec((q, LANE), lambda b, c: (b * nc + c, 0)),
            _const_spec(conv_w.shape), _const_spec(conv_b.shape), _const_spec(dtb_col.shape),
            _const_spec(alog_col.shape), _const_spec(e_mat.shape), _const_spec(dsk_row.shape),
            _const_spec(nw_row.shape),
        ],
        out_specs=[
            pl.BlockSpec((q, d_inner), lambda b, c: (b * nc + c, 0)),
            pl.BlockSpec((1, n_heads, SSM_HEAD_DIM, D_STATE), lambda b, c: (b, 0, 0, 0)),
        ],
        out_shape=[
            jax.ShapeDtypeStruct((n_batch * seq, d_inner), F32),
            jax.ShapeDtypeStruct((n_batch, n_heads, SSM_HEAD_DIM, D_STATE), F32),
        ],
        scratch_shapes=[
            pltpu.VMEM((q + SUBLANE, cd), F32),
            pltpu.VMEM((SSM_GROUPS, D_STATE, hpg * SSM_HEAD_DIM), F32),
            pltpu.VMEM((q, d_inner), F32),
        ],
        compiler_params=_params(2),
        name="prompt_ssd",
    )(xbc, z, misc, conv_w, conv_b, dtb_col, alog_col, e_mat, dsk_row, nw_row)


def _mlp_kernel(x_ref, att_ref, y_ref, ga_ref, gs_ref, gt1_ref, sc2_ref, sh2_ref, gt2_ref, n2_ref, fn_ref, wo_ref,
                wu_ref, wd_ref, o_ref, *, ff_chunk):
    merged = _sigmoid(ga_ref[...]) * att_ref[...] + _sigmoid(gs_ref[...]) * y_ref[...]
    x1 = x_ref[...] + gt1_ref[0] * jnp.dot(merged.astype(BF16), wo_ref[...], preferred_element_type=F32)
    h2 = x1 * lax.rsqrt(jnp.mean(x1 * x1, axis=-1, keepdims=True) + EPS) * n2_ref[...]
    hb = (h2 * (1.0 + sc2_ref[0]) + sh2_ref[0]).astype(BF16)
    acc = jnp.zeros(x1.shape, F32)
    for c in range(wu_ref.shape[1] // ff_chunk):
        u = jnp.maximum(jnp.dot(hb, wu_ref[:, c * ff_chunk:(c + 1) * ff_chunk], preferred_element_type=F32), 0.0)
        acc = acc + jnp.dot((u * u).astype(BF16), wd_ref[c * ff_chunk:(c + 1) * ff_chunk, :],
                            preferred_element_type=F32)
    x2 = x1 + gt2_ref[0] * acc
    o_ref[...] = x2 * lax.rsqrt(jnp.mean(x2 * x2, axis=-1, keepdims=True) + EPS) * fn_ref[...]


def _mlp(x2d, att, y, ga, gs, gt1, sc2, sh2, gt2, n2, fn, wo, wu, wd, *, seq, tm, ff_chunk=1024):
    rows, d = x2d.shape
    tb = seq // tm
    r_mod = gt1.shape[1]
    row = pl.BlockSpec((tm, d), lambda i: (i, 0))
    mod = pl.BlockSpec((1, r_mod, d), lambda i: (i // tb, 0, 0))
    return pl.pallas_call(
        functools.partial(_mlp_kernel, ff_chunk=ff_chunk),
        grid=(rows // tm,),
        in_specs=[row, row, row, row, row, mod, mod, mod, mod, _const_spec((1, d)), _const_spec((1, d)),
                  _const_spec(wo.shape), _const_spec(wu.shape), _const_spec(wd.shape)],
        out_specs=row,
        out_shape=jax.ShapeDtypeStruct((rows, d), F32),
        compiler_params=_params(1),
        name="merge_mlp",
    )(x2d, att, y, ga, gs, gt1, sc2, sh2, gt2, n2, fn, wo, wu, wd)


def _page_copy(cache_ref, page_id, buf, slot, m, sem):
    return pltpu.make_async_copy(cache_ref.at[0, page_id], buf.at[slot, m], sem.at[slot])


def _sidx_kernel(pt_ref, iq_ref, cw_ref, iknew_ref, ic_ref, o_ref, buf, sem, ik_scr, *, n_pages, tail):
    b = pl.program_id(0)
    slot = b % 2
    page = buf.shape[3]
    past = n_pages * page
    iq = iq_ref[0]
    cw = cw_ref[0]

    def fetch(sample, sl):
        for m in range(n_pages):
            _page_copy(ic_ref, pt_ref[sample, m], buf, sl, m, sem).start()

    @pl.when(b == 0)
    def _():
        fetch(0, 0)

    @pl.when(b + 1 < pl.num_programs(0))
    def _():
        fetch(b + 1, 1 - slot)

    for m in range(n_pages):
        _page_copy(ic_ref, 0, buf, slot, m, sem).wait()

    def score(ik_t):
        x = _bdot(iq, ik_t)
        return jnp.sum(jnp.maximum(x, 0.0) * cw, axis=0, keepdims=True)

    for m in range(n_pages):
        ik_scr[:, m * LANE:(m + 1) * LANE] = buf[slot, m].astype(BF16)
    o_ref[0, :, 0:past] = score(ik_scr[...])
    s_new = score(iknew_ref[0])[:, 0:1]
    lane = lax.broadcasted_iota(I32, (1, tail), 1)
    o_ref[0, :, past:past + tail] = jnp.where(lane == 0, jnp.broadcast_to(s_new, lane.shape), NEG_INF)


def _sample_index_scores(page_table, iq3, cw3, iknew_t, idx_cache_t, *, tail):
    db, n_pages = page_table.shape
    page = idx_cache_t.shape[3]
    past = n_pages * page
    assert page == LANE

    def per_sample(shape):
        return pl.BlockSpec((1,) + tuple(shape[1:]), lambda b, pt: (b, 0, 0))

    grid_spec = pltpu.PrefetchScalarGridSpec(
        num_scalar_prefetch=1,
        grid=(db,),
        in_specs=[per_sample(iq3.shape), per_sample(cw3.shape), per_sample(iknew_t.shape),
                  pl.BlockSpec(memory_space=pl.ANY)],
        out_specs=per_sample((db, 1, past + tail)),
        scratch_shapes=[pltpu.VMEM((2, n_pages, IDX_DIM, page), F32), pltpu.SemaphoreType.DMA((2,)),
                        pltpu.VMEM((IDX_DIM, past), BF16)],
    )
    return pl.pallas_call(
        functools.partial(_sidx_kernel, n_pages=n_pages, tail=tail),
        grid_spec=grid_spec,
        out_shape=jax.ShapeDtypeStruct((db, 1, past + tail), F32),
        compiler_params=_params(1),
        name="sample_index_scores",
    )(page_table, iq3, cw3, iknew_t, idx_cache_t)


def _ssel_kernel(sc_ref, bias_ref, *, topk):
    sc = sc_ref[...]
    db, lk = sc.shape

    def count_ge(thr):
        return jnp.sum(jnp.where(sc >= thr, 1.0, 0.0), axis=1, keepdims=True)

    tau, _ = _kth_largest([(32, count_ge, True)], (db, 1), topk, jnp.float32(lk))
    need = float(topk) - jnp.sum(jnp.where(sc > tau, 1.0, 0.0), axis=1, keepdims=True)
    ri = lax.broadcasted_iota(I32, (LANE, LANE), 0)
    ci = lax.broadcasted_iota(I32, (LANE, LANE), 1)
    upper = jnp.where(ri <= ci, 1.0, 0.0).astype(BF16)

    def body(j, seen):
        off = pl.multiple_of(j * LANE, LANE)
        sb = sc_ref[:, pl.ds(off, LANE)]
        eq = sb == tau
        prefix = jnp.dot(jnp.where(eq, 1.0, 0.0).astype(BF16), upper, preferred_element_type=F32)
        keep_eq = jnp.where(seen + prefix <= need, 0.0, NEG)
        sel = jnp.where(sb > tau, 0.0, jnp.where(eq, keep_eq, NEG))
        bias_ref[:, pl.ds(off, LANE)] = jnp.where(sb == NEG_INF, NEG, sel)
        return seen + prefix[:, LANE - 1:LANE]

    lax.fori_loop(0, lk // LANE, body, jnp.zeros((db, 1), F32))


def _sample_select(keys2d, *, topk):
    return pl.pallas_call(
        functools.partial(_ssel_kernel, topk=topk),
        out_shape=jax.ShapeDtypeStruct(keys2d.shape, F32),
        compiler_params=pltpu.CompilerParams(vmem_limit_bytes=VMEM_LIMIT),
        name="sample_select",
    )(keys2d)


def _sattn_kernel(pt_ref, q_ref, bias_ref, knew_ref, vnew_ref, kc_ref, vc_ref, o_ref, kbuf, vbuf, ksem, vsem, k_scr,
                  v_scr, m_scr, l_scr, acc_scr, *, pages_per_step, n_steps):
    b = pl.program_id(0)
    width = pages_per_step * kbuf.shape[3]
    q = q_ref[0]
    bias = bias_ref[0]

    def fetch(sample, step, slot):
        for m in range(pages_per_step):
            page_id = pt_ref[sample, step * pages_per_step + m]
            _page_copy(kc_ref, page_id, kbuf, slot, m, ksem).start()
            _page_copy(vc_ref, page_id, vbuf, slot, m, vsem).start()

    def wait(slot):
        for m in range(pages_per_step):
            _page_copy(kc_ref, 0, kbuf, slot, m, ksem).wait()
            _page_copy(vc_ref, 0, vbuf, slot, m, vsem).wait()

    n_slots = kbuf.shape[0]
    ahead = n_slots - 1

    def fetch_ahead(step, lookahead):
        g = b * n_steps + step + lookahead
        sample_off, step_t = divmod(step + lookahead, n_steps)

        @pl.when(b + sample_off < pl.num_programs(0))
        def _():
            fetch(b + sample_off, step_t, lax.rem(g, n_slots))

    @pl.when(b == 0)
    def _():
        for lookahead in range(ahead):
            fetch_ahead(0, lookahead)

    m_scr[...] = jnp.full(m_scr.shape, NEG, F32)
    l_scr[...] = jnp.zeros(l_scr.shape, F32)
    acc_scr[...] = jnp.zeros(acc_scr.shape, F32)

    def update(s, pv_fn):
        m_prev = m_scr[...]
        m_new = jnp.maximum(m_prev, jnp.max(s, axis=1, keepdims=True))
        alpha = jnp.exp(m_prev - m_new)
        p = jnp.exp(s - m_new[:, 0:1])
        m_scr[...] = m_new
        l_scr[...] = alpha * l_scr[...] + jnp.sum(p, axis=1, keepdims=True)
        acc_scr[...] = alpha[:, 0:1] * acc_scr[...] + pv_fn(p)

    for step in range(n_steps):
        slot = lax.rem(b * n_steps + step, n_slots)
        fetch_ahead(step, ahead)
        wait(slot)
        for m in range(pages_per_step):
            k_scr[:, m * LANE:(m + 1) * LANE] = kbuf[slot, m].astype(BF16)
            v_scr[:, m * LANE:(m + 1) * LANE] = vbuf[slot, m].astype(BF16)
        s = jnp.dot(q, k_scr[...], preferred_element_type=F32) + bias[:, step * width:(step + 1) * width]
        update(s, lambda p: _bdot_nt(p, v_scr[...]))

    past = n_steps * width
    s = _bdot(q, knew_ref[0]) + bias[:, past:past + LANE]
    update(s, lambda p: _bdot_nt(p, vnew_ref[0]))
    o_ref[0] = acc_scr[...] / l_scr[:, 0:1]


def _sample_attention(page_table, qmat, bias3, knew_t, vnew_t, cache_kt, cache_vt, *, pages_per_step):
    db, n_pages = page_table.shape
    kv_w, page = cache_kt.shape[2], cache_kt.shape[3]
    nh = qmat.shape[1]
    n_steps = n_pages // pages_per_step
    width = pages_per_step * page
    assert bias3.shape[2] >= n_steps * width + LANE
    slots = KV_FETCH_SLOTS

    def per_sample(shape):
        return pl.BlockSpec((1,) + tuple(shape[1:]), lambda b, pt: (b, 0, 0))

    grid_spec = pltpu.PrefetchScalarGridSpec(
        num_scalar_prefetch=1,
        grid=(db,),
        in_specs=[per_sample(qmat.shape), per_sample(bias3.shape), per_sample(knew_t.shape), per_sample(vnew_t.shape),
                  pl.BlockSpec(memory_space=pl.ANY), pl.BlockSpec(memory_space=pl.ANY)],
        out_specs=per_sample((db, nh, kv_w)),
        scratch_shapes=[
            pltpu.VMEM((slots, pages_per_step, kv_w, page), F32), pltpu.VMEM((slots, pages_per_step, kv_w, page), F32),
            pltpu.SemaphoreType.DMA((slots,)), pltpu.SemaphoreType.DMA((slots,)),
            pltpu.VMEM((kv_w, width), BF16), pltpu.VMEM((kv_w, width), BF16),
            pltpu.VMEM((nh, LANE), F32), pltpu.VMEM((nh, LANE), F32), pltpu.VMEM((nh, kv_w), F32),
        ],
    )
    return pl.pallas_call(
        functools.partial(_sattn_kernel, pages_per_step=pages_per_step, n_steps=n_steps),
        grid_spec=grid_spec,
        out_shape=jax.ShapeDtypeStruct((db, nh, kv_w), F32),
        compiler_params=_params(1),
        name="sample_attention",
    )(page_table, qmat, bias3, knew_t, vnew_t, cache_kt, cache_vt)


def _sssd_kernel(xbc_ref, prev_ref, misc_ref, z_ref, st_ref, cw_ref, cb_ref, dtb_ref, alog_ref, e_ref, dsk_ref,
                 nw_ref, y_ref, sto_ref, *, d_inner, n_heads):
    gn = SSM_GROUPS * D_STATE
    hpg = n_heads // SSM_GROUPS
    gw = hpg * SSM_HEAD_DIM
    u = xbc_ref[0]
    prev = prev_ref[0]
    conv = cb_ref[...] + cw_ref[CONV_W - 1:CONV_W, :] * u
    for w in range(CONV_W - 1):
        conv = conv + cw_ref[w:w + 1, :] * prev[w:w + 1, :]
    act = _silu(conv)
    xs = act[:, :d_inner]
    bm = act[:, d_inner:d_inner + gn]
    cm = act[:, d_inner + gn:]
    dt = _softplus(misc_ref[0] + dtb_ref[...])
    dec = jnp.exp(dt * (-jnp.exp(alog_ref[...])))
    pad6 = jnp.zeros((SUBLANE - 2, LANE), F32)
    ex = _split3_dot(jnp.concatenate([dt, dec, pad6], axis=0), e_ref[...])
    rows = jnp.concatenate([xs * ex[0:1, :], ex[1:2, :], jnp.zeros((LANE - 2, d_inner), F32)], axis=0)
    cols = rows.T
    pad15 = jnp.zeros((2 * SUBLANE - 1, D_STATE), F32)
    ys = []
    for g in range(SSM_GROUPS):
        s0 = st_ref[0, g * hpg:(g + 1) * hpg].reshape(gw, D_STATE)
        cg = cols[g * gw:(g + 1) * gw, :]
        s1 = s0 * cg[:, 1:2] + cg[:, 0:1] * bm[:, g * D_STATE:(g + 1) * D_STATE]
        sto_ref[0, g * hpg:(g + 1) * hpg] = s1.reshape(hpg, SSM_HEAD_DIM, D_STATE)
        c16 = jnp.concatenate([cm[:, g * D_STATE:(g + 1) * D_STATE], pad15], axis=0)
        ys.append(_bdot_nt(c16, s1)[0:1, :])
    y = jnp.concatenate(ys, axis=-1) + dsk_ref[...] * xs
    y_ref[0] = _gated_group_norm(y, z_ref[0], nw_ref[...], d_inner)


def _sample_ssd(xbc3, prev, misc3, z3, state, conv_w, conv_b, dtb_row, alog_row, e_mat, dsk_row, nw_row, *,
                d_inner, n_heads):
    db = xbc3.shape[0]
    cd = xbc3.shape[2]

    def per(shape):
        n = len(shape)
        return pl.BlockSpec((1,) + tuple(shape[1:]), lambda b: (b,) + (0,) * (n - 1))

    return pl.pallas_call(
        functools.partial(_sssd_kernel, d_inner=d_inner, n_heads=n_heads),
        grid=(db,),
        in_specs=[per(xbc3.shape), per(prev.shape), per(misc3.shape), per(z3.shape), per(state.shape),
                  _const_spec(conv_w.shape), _const_spec(conv_b.shape), _const_spec(dtb_row.shape),
                  _const_spec(alog_row.shape), _const_spec(e_mat.shape), _const_spec(dsk_row.shape),
                  _const_spec(nw_row.shape)],
        out_specs=[per((db, 1, d_inner)), per(state.shape)],
        out_shape=[jax.ShapeDtypeStruct((db, 1, d_inner), F32), jax.ShapeDtypeStruct(state.shape, F32)],
        compiler_params=_params(1),
        name="sample_ssd",
    )(xbc3, prev, misc3, z3, state, conv_w, conv_b, dtb_row, alog_row, e_mat, dsk_row, nw_row)


def _rope_tables(pos):
    inv = ROPE_THETA ** (-jnp.arange(ROT_HALF, dtype=F32) * 2.0 / (2 * ROT_HALF))
    ang = pos.astype(F32)[:, None] * inv[None, :]
    cos, sin = jnp.cos(ang), jnp.sin(ang)
    n = pos.shape[0]
    one = jnp.ones((n, HEAD_DIM - 2 * ROT_HALF), F32)
    zero = jnp.zeros((n, HEAD_DIM - 2 * ROT_HALF), F32)
    z8 = jnp.zeros((n, ROT_HALF), F32)
    c_head = jnp.concatenate([cos, cos, one], axis=1)
    s1_head = jnp.concatenate([-sin, z8, zero], axis=1)
    s2_head = jnp.concatenate([z8, sin, zero], axis=1)
    rep = LANE // HEAD_DIM
    return jnp.tile(c_head, (1, rep)), jnp.tile(s1_head, (1, rep)), jnp.tile(s2_head, (1, rep))


def kernel(x_prompt, x_sample, cache_k, cache_v, cache_idx_k, state_conv, state_ssm, page_table, c_prompt, c_sample,
           w_ada, b_ada, norm1_w, w_in, conv_w, conv_b, dt_bias, a_log, d_skip, ssm_norm_w, w_out, norm2_w, w_up,
           w_down, final_norm_w):
    nb, seq, d = x_prompt.shape
    db, dec_seq, _ = x_sample.shape
    depth = w_in.shape[0]
    assert depth == 1 and dec_seq == 1 and seq % SSD_CHUNK == 0
    n_heads = d // HEAD_DIM
    n_ssm_heads = a_log.shape[1]
    d_inner = n_ssm_heads * SSM_HEAD_DIM
    conv_dim = conv_w.shape[2]
    kv_w = N_KV_HEADS * HEAD_DIM
    iq_w = IDX_HEADS * IDX_DIM
    n_pages, page = page_table.shape[1], cache_k.shape[2]
    past = n_pages * page

    wi = w_in[0]
    offs = np.cumsum([0, d, kv_w, kv_w, iq_w, IDX_DIM, IDX_HEADS, d_inner, conv_dim, n_ssm_heads, d, d])
    misc_w = jnp.concatenate(
        [wi[:, offs[4]:offs[6]], wi[:, offs[8]:offs[9]],
         jnp.zeros((d, LANE - IDX_DIM - IDX_HEADS - n_ssm_heads), F32)], axis=1)
    w_parts = tuple(w.astype(BF16) for w in (wi[:, :offs[4]], misc_w, wi[:, offs[6]:offs[8]], wi[:, offs[9]:]))
    wo, wu, wd = w_out[0].astype(BF16), w_up[0].astype(BF16), w_down[0].astype(BF16)
    nw1, nw2, fnw = norm1_w[0][None, :], norm2_w[0][None, :], final_norm_w[None, :]
    head_lane = jnp.zeros((LANE,), F32)
    dtb_row = head_lane.at[DT_OFF:DT_OFF + n_ssm_heads].set(dt_bias[0])[None, :]
    alog_row = head_lane.at[DT_OFF:DT_OFF + n_ssm_heads].set(a_log[0])[None, :]
    e_mat = jnp.zeros((LANE, d_inner), F32).at[DT_OFF:DT_OFF + n_ssm_heads].set(
        jnp.repeat(jnp.eye(n_ssm_heads, dtype=F32), SSM_HEAD_DIM, axis=1)).astype(BF16)
    dsk_row = jnp.repeat(d_skip[0], SSM_HEAD_DIM)[None, :]
    snw_row = ssm_norm_w[0][None, :]
    cw2, cb2 = conv_w[0], conv_b[0][None, :]

    n_mod = nb + db
    r_mod = -(-n_mod // SUBLANE) * SUBLANE
    c_all = jnp.concatenate([c_prompt, c_sample, jnp.zeros((r_mod - n_mod, d), F32)], axis=0)
    mod = _ada(c_all, w_ada[0], b_ada[0][None, :])
    sh1, sc1, gt1, sh2, sc2, gt2 = [mod[:, k * d:(k + 1) * d] for k in range(6)]

    def pmod(a):
        return a[:nb].reshape(nb, 1, d)

    def smod(a):
        return a[nb:nb + db].reshape(1, db, d)

    xp = x_prompt.reshape(nb * seq, d)
    cos_p, s1_p, s2_p = _rope_tables(jnp.arange(seq))
    (q_hm, kt32, kt, vt32, ve, iqt, ikt32, ikr, cwt, misc_p, z_p, xbc_p, ga_p, gs_p) = _inproj(
        xp, pmod(sc1), pmod(sh1), nw1, w_parts, cos_p, s1_p, s2_p, n_batch=nb, seq=seq, tm=256, prompt=True,
        d_inner=d_inner, conv_dim=conv_dim)
    topk_p = min(TOPK_MAX, seq // 4)
    att_p = _prompt_attention(q_hm, iqt, cwt, kt, ve, ikr, tq=256, topk=topk_p)
    y_p, ssm_p = _prompt_ssd(xbc_p, z_p, misc_p, cw2, cb2, dtb_row.T, alog_row.T, e_mat, dsk_row, snw_row,
                             n_batch=nb, seq=seq, d_inner=d_inner, n_heads=n_ssm_heads)
    out_p = _mlp(xp, att_p, y_p, ga_p, gs_p, pmod(gt1), pmod(sc2), pmod(sh2), pmod(gt2), nw2, fnw, wo, wu, wd,
                 seq=seq, tm=256)

    xs2 = x_sample.reshape(db, d)
    cos_s, s1_s, s2_s = _rope_tables(jnp.full((1,), past, jnp.int32))
    (q_s, k_s, v_s, iq_s, misc_s, z_s, xbc_s, ga_s, gs_s) = _inproj(
        xs2, smod(sc1), smod(sh1), nw1, w_parts, cos_s, s1_s, s2_s, n_batch=1, seq=db, tm=db, prompt=False,
        d_inner=d_inner, conv_dim=conv_dim)
    pps_kv = _largest_divisor(n_pages, KV_PAGES_PER_STEP)
    cw_s = (misc_s[:, IW_OFF:IW_OFF + IDX_HEADS] * (IDX_HEADS ** -0.5) * (IDX_DIM ** -0.5)).reshape(db, IDX_HEADS, 1)

    def as_page_t(a):
        return jnp.pad(a[:, :, None], ((0, 0), (0, 0), (0, page - 1)))

    idx_cache_t = jnp.transpose(cache_idx_k, (0, 1, 3, 2))
    kv_shape_t = (depth, cache_k.shape[1], kv_w, page)
    cache_kt = jnp.transpose(cache_k, (0, 1, 3, 4, 2)).reshape(kv_shape_t)
    cache_vt = jnp.transpose(cache_v, (0, 1, 3, 4, 2)).reshape(kv_shape_t)

    lk = past + pps_kv * page
    sc_s = _sample_index_scores(page_table, iq_s.reshape(db, IDX_HEADS, IDX_DIM), cw_s,
                                as_page_t(misc_s[:, :IDX_DIM]), idx_cache_t, tail=pps_kv * page)
    topk_s = min(TOPK_MAX, (past + 1) // 4)
    bias_s = _sample_select(sc_s.reshape(db, lk), topk=topk_s).reshape(db, 1, lk)
    hpg = n_heads // N_KV_HEADS
    q5 = q_s.reshape(db, N_KV_HEADS, hpg, 1, HEAD_DIM)
    eye = jnp.eye(N_KV_HEADS, dtype=F32)[None, :, None, :, None]
    qmat = (q5 * eye).reshape(db, n_heads, kv_w).astype(BF16)
    acc_s = _sample_attention(page_table, qmat, bias_s, as_page_t(k_s), as_page_t(v_s), cache_kt, cache_vt,
                              pages_per_step=pps_kv)
    a5 = acc_s.reshape(db, N_KV_HEADS, hpg, N_KV_HEADS, HEAD_DIM)
    att_s = jnp.einsum('bghgd->bghd', a5).reshape(db, d)
    y_s, ssm_s = _sample_ssd(xbc_s.reshape(db, 1, conv_dim), state_conv[0], misc_s.reshape(db, 1, LANE),
                             z_s.reshape(db, 1, d_inner), state_ssm[0], cw2, cb2, dtb_row, alog_row, e_mat, dsk_row,
                             snw_row, d_inner=d_inner, n_heads=n_ssm_heads)
    out_s = _mlp(xs2, att_s, y_s.reshape(db, d_inner), ga_s, gs_s, smod(gt1), smod(sc2), smod(sh2), smod(gt2), nw2,
                 fnw, wo, wu, wd, seq=db, tm=db)

    conv_p = xbc_p.reshape(nb, seq, conv_dim)[:, seq - (CONV_W - 1):, :]
    conv_s = jnp.concatenate([state_conv[0][:, 1:, :], xbc_s[:, None, :]], axis=1)
    def kv_out(a_t):
        return jnp.transpose(a_t.reshape(nb, N_KV_HEADS, HEAD_DIM, seq), (0, 3, 1, 2))[None]

    return (
        out_p.reshape(nb, seq, d),
        out_s.reshape(db, 1, d),
        kv_out(kt32),
        kv_out(vt32),
        jnp.transpose(ikt32, (0, 2, 1))[None],
        conv_p[None],
        ssm_p[None],
        k_s.reshape(1, db, 1, N_KV_HEADS, HEAD_DIM),
        v_s.reshape(1, db, 1, N_KV_HEADS, HEAD_DIM),
        misc_s[:, :IDX_DIM].reshape(1, db, 1, IDX_DIM),
        conv_s[None],
        ssm_s[None],
    )
```

```python
import functools

import jax
import jax.numpy as jnp
import numpy as np
from jax import lax
from jax.experimental import pallas as pl
from jax.experimental.pallas import tpu as pltpu

F32 = jnp.float32
BF16 = jnp.bfloat16
I32 = jnp.int32

HEAD_DIM = 64
N_KV_HEADS = 4
ROT_HALF = 8
ROPE_THETA = 500000.0
IDX_HEADS = 8
IDX_DIM = 64
TOPK_MAX = 256
SSM_HEAD_DIM = 64
SSM_GROUPS = 4
D_STATE = 128
CONV_W = 4
SSD_CHUNK = 128
EPS = 1e-6

LANE = 128
SUBLANE = 8
VMEM_LIMIT = 56 * 1024 * 1024
KV_PAGES_PER_STEP = 16
KV_FETCH_SLOTS = 3
COUNT_STRIP = 64
SOFTMAX_KEY_BLOCKS = 2
EARLY_EXIT_FROM_BIT = 24

IW_OFF = IDX_DIM
DT_OFF = IDX_DIM + IDX_HEADS
INT_MIN = -(2 ** 31)
NEG = -1e30
NEG_INF = float("-inf")
F32_LOWEST = float(np.finfo(np.float32).min)
LOG2E = 1.4426950408889634
BF16_STEP_DOWN = 1.25 * 2.0 ** -8


def _sigmoid(x):
    return 1.0 / (1.0 + jnp.exp(-x))


def _silu(x):
    return x * _sigmoid(x)


def _softplus(x):
    return jnp.maximum(x, 0.0) + jnp.log1p(jnp.exp(-jnp.abs(x)))


def _bdot(a, b):
    return jnp.dot(a.astype(BF16), b.astype(BF16), preferred_element_type=F32)


def _bdot_nt(a, b):
    return lax.dot_general(a.astype(BF16), b.astype(BF16), (((1,), (1,)), ((), ())), preferred_element_type=F32)


def _split3_dot(v, e):
    hi = v.astype(BF16)
    r1 = v - hi.astype(F32)
    mid = r1.astype(BF16)
    lo = (r1 - mid.astype(F32)).astype(BF16)
    d = functools.partial(jnp.dot, preferred_element_type=F32)
    return d(hi, e) + d(mid, e) + d(lo, e)


def _f32_at_rank(u):
    key = u ^ jnp.int32(INT_MIN)
    bits = jnp.where(key < 0, jnp.int32(INT_MIN) - key, key)
    return lax.bitcast_convert_type(bits, F32)


def _floor_to_bf16(x):
    r = x.astype(BF16).astype(F32)
    below = r - jnp.abs(r) * BF16_STEP_DOWN
    return jnp.where(r > x, below, r).astype(BF16)


def _kth_largest(stages, shape, topk, n_total):
    def step(t, tau_u, cnt_tau, count_ge):
        cand_u = tau_u | jnp.left_shift(jnp.int32(1), 31 - t)
        cnt = count_ge(_f32_at_rank(cand_u))
        take = cnt >= float(topk)
        return jnp.where(take, cand_u, tau_u), jnp.where(take, cnt, cnt_tau)

    t0 = 0
    tau_u, cnt_tau = jnp.zeros(shape, I32), jnp.zeros(shape, F32) + n_total
    for end_bit, count_ge, early_exit in stages:
        if early_exit:
            def cond(c, end_bit=end_bit):
                return jnp.logical_and(c[0] < end_bit, c[3] > 0)

            def body(c, count_ge=count_ge):
                tau_n, cnt_n = step(c[0], c[1], c[2], count_ge)
                return c[0] + 1, tau_n, cnt_n, (jnp.max(cnt_n) > float(topk)).astype(I32)

            init = (jnp.int32(t0), tau_u, cnt_tau, (jnp.max(cnt_tau) > float(topk)).astype(I32))
            _, tau_u, cnt_tau, _ = lax.while_loop(cond, body, init)
        else:
            tau_u, cnt_tau = lax.fori_loop(
                t0, end_bit, lambda t, c, count_ge=count_ge: step(t, c[0], c[1], count_ge), (tau_u, cnt_tau))
        t0 = end_bit
    return jnp.where(tau_u == 0, NEG_INF, _f32_at_rank(tau_u)), cnt_tau


def _largest_divisor(n, cap):
    return max(k for k in range(1, cap + 1) if n % k == 0)


def _const_spec(shape):
    n = len(shape)
    return pl.BlockSpec(shape, lambda *a: (0,) * n, pipeline_mode=pl.Buffered(1))


def _params(n_axes):
    return pltpu.CompilerParams(dimension_semantics=("arbitrary",) * n_axes, vmem_limit_bytes=VMEM_LIMIT)


def _ada_kernel(c_ref, w_ref, b_ref, o_ref):
    s = _silu(c_ref[...])
    s_hi = s.astype(BF16)
    s_lo = (s - s_hi.astype(F32)).astype(BF16)
    w = w_ref[...]
    w_hi = w.astype(BF16)
    w_lo = (w - w_hi.astype(F32)).astype(BF16)
    d = functools.partial(jnp.dot, preferred_element_type=F32)
    o_ref[...] = d(s_hi, w_hi) + d(s_lo, w_hi) + d(s_hi, w_lo) + b_ref[...]


def _ada(c_all, w_ada, b_ada, tn=1024):
    r, d = c_all.shape
    n = w_ada.shape[1]
    return pl.pallas_call(
        _ada_kernel,
        grid=(n // tn,),
        in_specs=[
            pl.BlockSpec((r, d), lambda j: (0, 0)),
            pl.BlockSpec((d, tn), lambda j: (0, j)),
            pl.BlockSpec((1, tn), lambda j: (0, j)),
        ],
        out_specs=pl.BlockSpec((r, tn), lambda j: (0, j)),
        out_shape=jax.ShapeDtypeStruct((r, n), F32),
        compiler_params=_params(1),
        name="ada",
    )(c_all, w_ada, b_ada)


def _rope_chunk(c, cos, s1, s2):
    return c * cos + pltpu.roll(c, LANE - ROT_HALF, 1) * s1 + pltpu.roll(c, ROT_HALF, 1) * s2


def _inproj_kernel(x_ref, sc_ref, sh_ref, nw_ref, wa_ref, wm_ref, wb_ref, wc_ref, cos_ref, s1_ref, s2_ref, *outs,
                   d_model, d_inner, conv_dim, prompt):
    x = x_ref[...]
    h = x * lax.rsqrt(jnp.mean(x * x, axis=-1, keepdims=True) + EPS) * nw_ref[...]
    hb = (h * (1.0 + sc_ref[0]) + sh_ref[0]).astype(BF16)
    cos, s1, s2 = cos_ref[...], s1_ref[...], s2_ref[...]
    kv_w = N_KV_HEADS * HEAD_DIM
    iq_w = IDX_HEADS * IDX_DIM
    q0 = (wa_ref, 0)
    k0 = (wa_ref, d_model)
    v0 = (wa_ref, d_model + kv_w)
    iq0 = (wa_ref, d_model + 2 * kv_w)
    m0 = (wm_ref, 0)
    z0 = (wb_ref, 0)
    x0 = (wb_ref, d_inner)
    ga0 = (wc_ref, 0)
    gs0 = (wc_ref, d_model)

    def proj(group, width):
        w_ref, lo = group
        return jnp.dot(hb, w_ref[:, lo:lo + width], preferred_element_type=F32)

    if prompt:
        (q_ref, kt32_ref, kt_ref, vt32_ref, ve_ref, iqt_ref, ikt32_ref, ikr_ref, cwt_ref, misc_ref, z_ref, xbc_ref,
         ga_ref, gs_ref) = outs
    else:
        q_ref, k_ref, v_ref, iq_ref, misc_ref, z_ref, xbc_ref, ga_ref, gs_ref = outs

    q = proj(q0, d_model)
    q_scale = HEAD_DIM ** -0.5 * (LOG2E if prompt else 1.0)
    for c in range(d_model // LANE):
        qc = _rope_chunk(q[:, c * LANE:(c + 1) * LANE], cos, s1, s2) * q_scale
        if prompt:
            q_ref[0, 2 * c] = qc[:, :HEAD_DIM].astype(BF16)
            q_ref[0, 2 * c + 1] = qc[:, HEAD_DIM:].astype(BF16)
        else:
            q_ref[:, c * LANE:(c + 1) * LANE] = qc

    kk = proj(k0, kv_w)
    for c in range(kv_w // LANE):
        kc = _rope_chunk(kk[:, c * LANE:(c + 1) * LANE], cos, s1, s2)
        if prompt:
            kc_t = kc.T
            kt32_ref[0, c * LANE:(c + 1) * LANE, :] = kc_t
            kt_ref[0, c * LANE:(c + 1) * LANE, :] = kc_t.astype(BF16)
        else:
            k_ref[:, c * LANE:(c + 1) * LANE] = kc

    vv = proj(v0, kv_w)
    if not prompt:
        v_ref[...] = vv
    else:
        for c in range(kv_w // LANE):
            vt32_ref[0, c * LANE:(c + 1) * LANE, :] = vv[:, c * LANE:(c + 1) * LANE].T
        lane = lax.broadcasted_iota(I32, (vv.shape[0], LANE), 1)
        for g in range(N_KV_HEADS):
            vc = vv[:, (g // 2) * LANE:(g // 2 + 1) * LANE]
            if g % 2 == 1:
                vc = pltpu.roll(vc, HEAD_DIM, 1)
            ve = jnp.where(lane < HEAD_DIM, vc, jnp.where(lane == HEAD_DIM, 1.0, 0.0))
            ve_ref[0, g] = ve.astype(BF16)

    iq = proj(iq0, iq_w)
    for c in range(iq_w // LANE):
        ic = _rope_chunk(iq[:, c * LANE:(c + 1) * LANE], cos, s1, s2)
        if prompt:
            iqt_ref[0, c * LANE:(c + 1) * LANE, :] = ic.T.astype(BF16)
        else:
            iq_ref[:, c * LANE:(c + 1) * LANE] = ic.astype(BF16)

    mm = proj(m0, LANE)
    lane = lax.broadcasted_iota(I32, mm.shape, 1)
    mm = jnp.where(lane < IDX_DIM, _rope_chunk(mm, cos, s1, s2), mm)
    misc_ref[...] = mm
    if prompt:
        mm_t = mm.T
        ikt32_ref[0] = mm_t[:IDX_DIM, :]
        ikr_ref[...] = mm[:, :IDX_DIM].astype(BF16)
        cwt_ref[0] = mm_t[IW_OFF:IW_OFF + IDX_HEADS, :] * (IDX_HEADS ** -0.5) * (IDX_DIM ** -0.5)

    z_ref[...] = proj(z0, d_inner)
    xbc_ref[...] = proj(x0, conv_dim)
    ga_ref[...] = proj(ga0, d_model)
    gs_ref[...] = proj(gs0, d_model)


def _inproj(x2d, sc, sh, nw, weights, cos, s1, s2, *, n_batch, seq, tm, prompt, d_inner, conv_dim):
    rows, d = x2d.shape
    tb = seq // tm
    r_mod = sc.shape[1]
    kv_w = N_KV_HEADS * HEAD_DIM
    iq_w = IDX_HEADS * IDX_DIM
    n_heads = d // HEAD_DIM
    if cos.shape[0] == 1:
        tab = pl.BlockSpec((1, LANE), lambda i: (0, 0))
    else:
        tab = pl.BlockSpec((tm, LANE), lambda i: (i % tb, 0))
    mod = pl.BlockSpec((1, r_mod, d), lambda i: (i // tb, 0, 0))
    in_specs = [pl.BlockSpec((tm, d), lambda i: (i, 0)), mod, mod, _const_spec((1, d))]
    in_specs += [_const_spec(w.shape) for w in weights] + [tab, tab, tab]

    def rowspec(w):
        return pl.BlockSpec((tm, w), lambda i: (i, 0))

    def sds(shape, dt):
        return jax.ShapeDtypeStruct(shape, dt)

    if prompt:
        def tspec(w):
            return pl.BlockSpec((1, w, tm), lambda i: (i // tb, 0, i % tb))

        out_specs = [
            pl.BlockSpec((1, n_heads, tm, HEAD_DIM), lambda i: (i // tb, 0, i % tb, 0)),
            tspec(kv_w), tspec(kv_w), tspec(kv_w),
            pl.BlockSpec((1, N_KV_HEADS, tm, LANE), lambda i: (i // tb, 0, i % tb, 0)),
            tspec(iq_w), tspec(IDX_DIM), rowspec(IDX_DIM), tspec(IDX_HEADS),
            rowspec(LANE), rowspec(d_inner), rowspec(conv_dim), rowspec(d), rowspec(d),
        ]
        out_shape = [
            sds((n_batch, n_heads, seq, HEAD_DIM), BF16), sds((n_batch, kv_w, seq), F32),
            sds((n_batch, kv_w, seq), BF16), sds((n_batch, kv_w, seq), F32),
            sds((n_batch, N_KV_HEADS, seq, LANE), BF16), sds((n_batch, iq_w, seq), BF16),
            sds((n_batch, IDX_DIM, seq), F32), sds((rows, IDX_DIM), BF16), sds((n_batch, IDX_HEADS, seq), F32),
            sds((rows, LANE), F32), sds((rows, d_inner), F32), sds((rows, conv_dim), F32), sds((rows, d), F32),
            sds((rows, d), F32),
        ]
    else:
        out_specs = [rowspec(d), rowspec(kv_w), rowspec(kv_w), rowspec(iq_w), rowspec(LANE), rowspec(d_inner),
                     rowspec(conv_dim), rowspec(d), rowspec(d)]
        out_shape = [sds((rows, d), F32), sds((rows, kv_w), F32), sds((rows, kv_w), F32), sds((rows, iq_w), BF16),
                     sds((rows, LANE), F32), sds((rows, d_inner), F32), sds((rows, conv_dim), F32),
                     sds((rows, d), F32), sds((rows, d), F32)]
    return pl.pallas_call(
        functools.partial(_inproj_kernel, d_model=d, d_inner=d_inner, conv_dim=conv_dim, prompt=prompt),
        grid=(rows // tm,),
        in_specs=in_specs,
        out_specs=out_specs,
        out_shape=out_shape,
        compiler_params=_params(1),
        name="inproj_prompt" if prompt else "inproj_sample",
    )(x2d, sc, sh, nw, *weights, cos, s1, s2)


def _attn_kernel(q_ref, iqt_ref, cwt_ref, kt_ref, ve_ref, ikr_ref, o_ref, sc_scr, sc16_scr, m_scr, acc_scr, *, tq,
                 topk):
    i = pl.program_id(1)
    nblk = i + 1
    tk = tq
    tkc = SOFTMAX_KEY_BLOCKS * tk
    hpg = q_ref.shape[1] // N_KV_HEADS
    cw = cwt_ref[0]
    kidx = lax.broadcasted_iota(I32, (tk, tq), 0)
    qidx = lax.broadcasted_iota(I32, (tk, tq), 1)

    def store_scores(j, diagonal):
        off = pl.multiple_of(j * tk, tk)
        ikb = ikr_ref[pl.ds(off, tk), :]
        acc = jnp.zeros((tk, tq), F32)
        for h in range(IDX_HEADS):
            x = jnp.dot(ikb, iqt_ref[0, h * IDX_DIM:(h + 1) * IDX_DIM, :], preferred_element_type=F32)
            acc = acc + jnp.maximum(x, 0.0) * cw[h:h + 1, :]
        sc = jnp.where(kidx <= qidx, acc, NEG_INF) if diagonal else acc
        sc_scr[pl.ds(off, tk), :] = sc
        sc16_scr[pl.ds(off, tk), :] = _floor_to_bf16(sc)

    def full_block(j, carry):
        store_scores(j, False)
        return carry

    lax.fori_loop(0, i, full_block, 0)
    store_scores(i, True)

    def count(cmp, thr):
        def body(j, acc):
            off = pl.multiple_of(j * tk, tk)
            for r in range(tk // COUNT_STRIP):
                sb = sc_scr[pl.ds(off + r * COUNT_STRIP, COUNT_STRIP), :]
                acc = acc + jnp.where(cmp(sb, thr), 1.0, 0.0)
            return acc

        acc = lax.fori_loop(0, nblk, body, jnp.zeros((COUNT_STRIP, tq), F32))
        return jnp.sum(acc, axis=0, keepdims=True)

    def count16(thr):
        thr16 = thr.astype(BF16)
        one, zero = jnp.ones((), BF16), jnp.zeros((), BF16)

        def body(j, acc):
            off = pl.multiple_of(j * tk, tk)
            for r in range(tk // COUNT_STRIP):
                sb = sc16_scr[pl.ds(off + r * COUNT_STRIP, COUNT_STRIP), :]
                acc = acc + jnp.where(sb >= thr16, one, zero)
            return acc

        acc = lax.fori_loop(0, nblk, body, jnp.zeros((COUNT_STRIP, tq), BF16))
        return jnp.sum(acc.astype(F32), axis=0, keepdims=True)

    count_ge = functools.partial(count, lax.ge)
    tau, cnt_tau = _kth_largest(
        [(16, count16, False), (EARLY_EXIT_FROM_BIT, count_ge, False), (32, count_ge, True)], (1, tq), topk,
        (nblk * tk).astype(F32))

    any_tie = jnp.max(jnp.where(cnt_tau > float(topk), 1.0, 0.0)) > 0.5

    @pl.when(any_tie)
    def _():
        need = float(topk) - count(lax.gt, tau)
        lower = jnp.where(qidx <= kidx, 1.0, 0.0).astype(BF16)

        def body(j, seen):
            off = pl.multiple_of(j * tk, tk)
            sb = sc_scr[pl.ds(off, tk), :]
            eq = sb == tau
            prefix = jnp.dot(lower, jnp.where(eq, 1.0, 0.0).astype(BF16), preferred_element_type=F32)
            late = jnp.where(seen + prefix > need, NEG_INF, sb)
            sc_scr[pl.ds(off, tk), :] = jnp.where(eq, late, sb)
            return seen + prefix[tk - 1:tk, :]

        lax.fori_loop(0, nblk, body, jnp.zeros((1, tq), F32))

    tau_c = jnp.maximum(tau, F32_LOWEST)
    m_scr[...] = jnp.full(m_scr.shape, NEG, F32)
    acc_scr[...] = jnp.zeros(acc_scr.shape, F32)

    def softmax_step(off, n_kb):
        w = n_kb * tk
        bias = jnp.concatenate(
            [jnp.where(sc_scr[pl.ds(off + b * tk, tk), :] >= tau_c, 0.0, NEG).T for b in range(n_kb)], axis=1)
        for g in range(N_KV_HEADS):
            qg = q_ref[0, g * hpg:(g + 1) * hpg].reshape(hpg * tq, HEAD_DIM)
            s = jnp.dot(qg, kt_ref[0, g * HEAD_DIM:(g + 1) * HEAD_DIM, pl.ds(off, w)],
                        preferred_element_type=F32)
            s = s.reshape(hpg, tq, w) + bias[None]
            m_prev = m_scr[g]
            m_new = jnp.maximum(m_prev, jnp.max(s, axis=-1, keepdims=True))
            alpha = jnp.exp2(m_prev - m_new)
            p = jnp.exp2(s - jnp.concatenate([m_new] * (w // LANE), axis=-1))
            m_scr[g] = m_new
            pv = jnp.dot(p.reshape(hpg * tq, w).astype(BF16), ve_ref[0, g, pl.ds(off, w), :],
                         preferred_element_type=F32)
            acc_scr[g] = alpha * acc_scr[g] + pv.reshape(hpg, tq, LANE)

    def full_step(jc, carry):
        softmax_step(pl.multiple_of(jc * tkc, tkc), SOFTMAX_KEY_BLOCKS)
        return carry

    def tail_step(j, carry):
        softmax_step(pl.multiple_of(j * tk, tk), 1)
        return carry

    n_full = nblk // SOFTMAX_KEY_BLOCKS
    lax.fori_loop(0, n_full, full_step, 0)
    lax.fori_loop(n_full * SOFTMAX_KEY_BLOCKS, nblk, tail_step, 0)

    for g in range(N_KV_HEADS):
        for hh in range(hpg):
            a = acc_scr[g, hh]
            hq = g * hpg + hh
            o_ref[:, hq * HEAD_DIM:(hq + 1) * HEAD_DIM] = a[:, :HEAD_DIM] / a[:, HEAD_DIM:HEAD_DIM + 1]


def _prompt_attention(q_hm, iqt, cwt, kt, ve, ikr, *, tq, topk):
    nb, nh, seq, _ = q_hm.shape
    hpg = nh // N_KV_HEADS
    tb = seq // tq
    kv_w = N_KV_HEADS * HEAD_DIM

    def resident(shape, imap):
        return pl.BlockSpec(shape, imap, pipeline_mode=pl.Buffered(1))

    return pl.pallas_call(
        functools.partial(_attn_kernel, tq=tq, topk=topk),
        grid=(nb, tb),
        in_specs=[
            pl.BlockSpec((1, nh, tq, HEAD_DIM), lambda b, i: (b, 0, i, 0)),
            pl.BlockSpec((1, IDX_HEADS * IDX_DIM, tq), lambda b, i: (b, 0, i)),
            pl.BlockSpec((1, IDX_HEADS, tq), lambda b, i: (b, 0, i)),
            resident((1, kv_w, seq), lambda b, i: (b, 0, 0)),
            resident((1, N_KV_HEADS, seq, LANE), lambda b, i: (b, 0, 0, 0)),
            resident((seq, IDX_DIM), lambda b, i: (b, 0)),
        ],
        out_specs=pl.BlockSpec((tq, nh * HEAD_DIM), lambda b, i: (b * tb + i, 0)),
        out_shape=jax.ShapeDtypeStruct((nb * seq, nh * HEAD_DIM), F32),
        scratch_shapes=[
            pltpu.VMEM((seq, tq), F32),
            pltpu.VMEM((seq, tq), BF16),
            pltpu.VMEM((N_KV_HEADS, hpg, tq, LANE), F32),
            pltpu.VMEM((N_KV_HEADS, hpg, tq, LANE), F32),
        ],
        compiler_params=_params(2),
        name="prompt_attention",
    )(q_hm, iqt, cwt, kt, ve, ikr)


def _gated_group_norm(y, z, nw, d_inner):
    g = y * _silu(z)
    gw = d_inner // SSM_GROUPS
    outs = []
    for k in range(SSM_GROUPS):
        gg = g[:, k * gw:(k + 1) * gw]
        outs.append(gg * lax.rsqrt(jnp.mean(gg * gg, axis=-1, keepdims=True) + EPS))
    return jnp.concatenate(outs, axis=-1) * nw


def _ssd_kernel(xbc_ref, z_ref, misc_ref, cw_ref, cb_ref, dtb_ref, alog_ref, e_ref, dsk_ref, nw_ref, y_ref, st_ref,
                ext_scr, st_scr, y_scr, *, d_inner, n_heads):
    c = pl.program_id(1)
    q = xbc_ref.shape[0]
    gn = SSM_GROUPS * D_STATE
    hpg = n_heads // SSM_GROUPS
    gw = hpg * SSM_HEAD_DIM

    @pl.when(c == 0)
    def _():
        ext_scr[0:SUBLANE, :] = jnp.zeros((SUBLANE, ext_scr.shape[1]), F32)
        st_scr[...] = jnp.zeros(st_scr.shape, F32)

    u = xbc_ref[...]
    ext_scr[SUBLANE:, :] = u
    conv = cb_ref[...] + cw_ref[CONV_W - 1:CONV_W, :] * u
    for w in range(CONV_W - 1):
        lo = SUBLANE - (CONV_W - 1) + w
        conv = conv + cw_ref[w:w + 1, :] * ext_scr[lo:lo + q, :]
    ext_scr[0:SUBLANE, :] = u[q - SUBLANE:, :]
    act = _silu(conv)
    xs = act[:, :d_inner]
    bm = act[:, d_inner:d_inner + gn]
    cm = act[:, d_inner + gn:]

    dt_t = _softplus(misc_ref[...].T + dtb_ref[...])
    da_t = dt_t * (-jnp.exp(alog_ref[...]))
    ri = lax.broadcasted_iota(I32, (q, q), 0)
    ci = lax.broadcasted_iota(I32, (q, q), 1)
    cs_t = _split3_dot(da_t, jnp.where(ri <= ci, 1.0, 0.0).astype(BF16))
    cs = cs_t.T
    dt = dt_t.T
    e = e_ref[...]
    ecs_x = _split3_dot(jnp.exp(cs), e)
    wst_x = _split3_dot(dt * jnp.exp(cs[q - 1:q, :] - cs), e)
    tri = ri >= ci

    for g in range(SSM_GROUPS):
        bg = bm[:, g * D_STATE:(g + 1) * D_STATE]
        cg = cm[:, g * D_STATE:(g + 1) * D_STATE].astype(BF16)
        bg_t = bg.T.astype(BF16)
        cb = jnp.dot(cg, bg_t, preferred_element_type=F32)
        s_t = st_scr[g]
        y_off = jnp.dot(cg, s_t.astype(BF16), preferred_element_type=F32)
        xg = xs[:, g * gw:(g + 1) * gw]
        y_g = y_off * ecs_x[:, g * gw:(g + 1) * gw]
        pieces = []
        for hh in range(hpg):
            idx = DT_OFF + g * hpg + hh
            seg = cs[:, idx:idx + 1] - cs_t[idx:idx + 1, :]
            mm = cb * jnp.exp(jnp.where(tri, seg, NEG)) * dt_t[idx:idx + 1, :]
            pieces.append(_bdot(mm, xg[:, hh * SSM_HEAD_DIM:(hh + 1) * SSM_HEAD_DIM]))
        y_scr[:, g * gw:(g + 1) * gw] = y_g + jnp.concatenate(pieces, axis=-1)
        wg = (xg * wst_x[:, g * gw:(g + 1) * gw]).astype(BF16)
        st_scr[g] = s_t * ecs_x[q - 1:q, g * gw:(g + 1) * gw] + jnp.dot(bg_t, wg, preferred_element_type=F32)

    y = y_scr[...] + dsk_ref[...] * xs
    y_ref[...] = _gated_group_norm(y, z_ref[...], nw_ref[...], d_inner)

    @pl.when(c == pl.num_programs(1) - 1)
    def _():
        for g in range(SSM_GROUPS):
            st_ref[0, g * hpg:(g + 1) * hpg] = st_scr[g].T.reshape(hpg, SSM_HEAD_DIM, D_STATE)


def _prompt_ssd(xbc, z, misc, conv_w, conv_b, dtb_col, alog_col, e_mat, dsk_row, nw_row, *, n_batch, seq, d_inner,
                n_heads):
    q = SSD_CHUNK
    nc = seq // q
    cd = xbc.shape[1]
    hpg = n_heads // SSM_GROUPS
    return pl.pallas_call(
        functools.partial(_ssd_kernel, d_inner=d_inner, n_heads=n_heads),
        grid=(n_batch, nc),
        in_specs=[
            pl.BlockSpec((q, cd), lambda b, c: (b * nc + c, 0)),
            pl.BlockSpec((q, d_inner), lambda b, c: (b * nc + c, 0)),
            pl.BlockSpec((q, LANE), lambda b, c: (b * nc + c, 0)),
            _const_spec(conv_w.shape), _const_spec(conv_b.shape), _const_spec(dtb_col.shape),
            _const_spec(alog_col.shape), _const_spec(e_mat.shape), _const_spec(dsk_row.shape),
            _const_spec(nw_row.shape),
        ],
        out_specs=[
            pl.BlockSpec((q, d_inner), lambda b, c: (b * nc + c, 0)),
            pl.BlockSpec((1, n_heads, SSM_HEAD_DIM, D_STATE), lambda b, c: (b, 0, 0, 0)),
        ],
        out_shape=[
            jax.ShapeDtypeStruct((n_batch * seq, d_inner), F32),
            jax.ShapeDtypeStruct((n_batch, n_heads, SSM_HEAD_DIM, D_STATE), F32),
        ],
        scratch_shapes=[
            pltpu.VMEM((q + SUBLANE, cd), F32),
            pltpu.VMEM((SSM_GROUPS, D_STATE, hpg * SSM_HEAD_DIM), F32),
            pltpu.VMEM((q, d_inner), F32),
        ],
        compiler_params=_params(2),
        name="prompt_ssd",
    )(xbc, z, misc, conv_w, conv_b, dtb_col, alog_col, e_mat, dsk_row, nw_row)


def _mlp_kernel(x_ref, att_ref, y_ref, ga_ref, gs_ref, gt1_ref, sc2_ref, sh2_ref, gt2_ref, n2_ref, fn_ref, wo_ref,
                wu_ref, wd_ref, o_ref, *, ff_chunk):
    merged = _sigmoid(ga_ref[...]) * att_ref[...] + _sigmoid(gs_ref[...]) * y_ref[...]
    x1 = x_ref[...] + gt1_ref[0] * jnp.dot(merged.astype(BF16), wo_ref[...], preferred_element_type=F32)
    h2 = x1 * lax.rsqrt(jnp.mean(x1 * x1, axis=-1, keepdims=True) + EPS) * n2_ref[...]
    hb = (h2 * (1.0 + sc2_ref[0]) + sh2_ref[0]).astype(BF16)
    acc = jnp.zeros(x1.shape, F32)
    for c in range(wu_ref.shape[1] // ff_chunk):
        u = jnp.maximum(jnp.dot(hb, wu_ref[:, c * ff_chunk:(c + 1) * ff_chunk], preferred_element_type=F32), 0.0)
        acc = acc + jnp.dot((u * u).astype(BF16), wd_ref[c * ff_chunk:(c + 1) * ff_chunk, :],
                            preferred_element_type=F32)
    x2 = x1 + gt2_ref[0] * acc
    o_ref[...] = x2 * lax.rsqrt(jnp.mean(x2 * x2, axis=-1, keepdims=True) + EPS) * fn_ref[...]


def _mlp(x2d, att, y, ga, gs, gt1, sc2, sh2, gt2, n2, fn, wo, wu, wd, *, seq, tm, ff_chunk=1024):
    rows, d = x2d.shape
    tb = seq // tm
    r_mod = gt1.shape[1]
    row = pl.BlockSpec((tm, d), lambda i: (i, 0))
    mod = pl.BlockSpec((1, r_mod, d), lambda i: (i // tb, 0, 0))
    return pl.pallas_call(
        functools.partial(_mlp_kernel, ff_chunk=ff_chunk),
        grid=(rows // tm,),
        in_specs=[row, row, row, row, row, mod, mod, mod, mod, _const_spec((1, d)), _const_spec((1, d)),
                  _const_spec(wo.shape), _const_spec(wu.shape), _const_spec(wd.shape)],
        out_specs=row,
        out_shape=jax.ShapeDtypeStruct((rows, d), F32),
        compiler_params=_params(1),
        name="merge_mlp",
    )(x2d, att, y, ga, gs, gt1, sc2, sh2, gt2, n2, fn, wo, wu, wd)


def _page_copy(cache_ref, page_id, buf, slot, m, sem):
    return pltpu.make_async_copy(cache_ref.at[0, page_id], buf.at[slot, m], sem.at[slot])


def _sidx_kernel(pt_ref, iq_ref, cw_ref, iknew_ref, ic_ref, o_ref, buf, sem, ik_scr, *, n_pages, tail):
    b = pl.program_id(0)
    slot = b % 2
    page = buf.shape[3]
    past = n_pages * page
    iq = iq_ref[0]
    cw = cw_ref[0]

    def fetch(sample, sl):
        for m in range(n_pages):
            _page_copy(ic_ref, pt_ref[sample, m], buf, sl, m, sem).start(priority=m % 2)

    @pl.when(b == 0)
    def _():
        fetch(0, 0)

    @pl.when(b + 1 < pl.num_programs(0))
    def _():
        fetch(b + 1, 1 - slot)

    for m in range(n_pages):
        _page_copy(ic_ref, 0, buf, slot, m, sem).wait()

    def score(ik_t):
        x = _bdot(iq, ik_t)
        return jnp.sum(jnp.maximum(x, 0.0) * cw, axis=0, keepdims=True)

    for m in range(n_pages):
        ik_scr[:, m * LANE:(m + 1) * LANE] = buf[slot, m].astype(BF16)
    o_ref[0, :, 0:past] = score(ik_scr[...])
    s_new = score(iknew_ref[0])[:, 0:1]
    lane = lax.broadcasted_iota(I32, (1, tail), 1)
    o_ref[0, :, past:past + tail] = jnp.where(lane == 0, jnp.broadcast_to(s_new, lane.shape), NEG_INF)


def _sample_index_scores(page_table, iq3, cw3, iknew_t, idx_cache_t, *, tail):
    db, n_pages = page_table.shape
    page = idx_cache_t.shape[3]
    past = n_pages * page
    assert page == LANE

    def per_sample(shape):
        return pl.BlockSpec((1,) + tuple(shape[1:]), lambda b, pt: (b, 0, 0))

    grid_spec = pltpu.PrefetchScalarGridSpec(
        num_scalar_prefetch=1,
        grid=(db,),
        in_specs=[per_sample(iq3.shape), per_sample(cw3.shape), per_sample(iknew_t.shape),
                  pl.BlockSpec(memory_space=pl.ANY)],
        out_specs=per_sample((db, 1, past + tail)),
        scratch_shapes=[pltpu.VMEM((2, n_pages, IDX_DIM, page), F32), pltpu.SemaphoreType.DMA((2,)),
                        pltpu.VMEM((IDX_DIM, past), BF16)],
    )
    return pl.pallas_call(
        functools.partial(_sidx_kernel, n_pages=n_pages, tail=tail),
        grid_spec=grid_spec,
        out_shape=jax.ShapeDtypeStruct((db, 1, past + tail), F32),
        compiler_params=_params(1),
        name="sample_index_scores",
    )(page_table, iq3, cw3, iknew_t, idx_cache_t)


def _ssel_kernel(sc_ref, bias_ref, *, topk):
    sc = sc_ref[...]
    db, lk = sc.shape

    def count_ge(thr):
        return jnp.sum(jnp.where(sc >= thr, 1.0, 0.0), axis=1, keepdims=True)

    tau, _ = _kth_largest([(32, count_ge, True)], (db, 1), topk, jnp.float32(lk))
    need = float(topk) - jnp.sum(jnp.where(sc > tau, 1.0, 0.0), axis=1, keepdims=True)
    ri = lax.broadcasted_iota(I32, (LANE, LANE), 0)
    ci = lax.broadcasted_iota(I32, (LANE, LANE), 1)
    upper = jnp.where(ri <= ci, 1.0, 0.0).astype(BF16)

    def body(j, seen):
        off = pl.multiple_of(j * LANE, LANE)
        sb = sc_ref[:, pl.ds(off, LANE)]
        eq = sb == tau
        prefix = jnp.dot(jnp.where(eq, 1.0, 0.0).astype(BF16), upper, preferred_element_type=F32)
        keep_eq = jnp.where(seen + prefix <= need, 0.0, NEG)
        sel = jnp.where(sb > tau, 0.0, jnp.where(eq, keep_eq, NEG))
        bias_ref[:, pl.ds(off, LANE)] = jnp.where(sb == NEG_INF, NEG, sel)
        return seen + prefix[:, LANE - 1:LANE]

    lax.fori_loop(0, lk // LANE, body, jnp.zeros((db, 1), F32))


def _sample_select(keys2d, *, topk):
    return pl.pallas_call(
        functools.partial(_ssel_kernel, topk=topk),
        out_shape=jax.ShapeDtypeStruct(keys2d.shape, F32),
        compiler_params=pltpu.CompilerParams(vmem_limit_bytes=VMEM_LIMIT),
        name="sample_select",
    )(keys2d)


def _sattn_kernel(pt_ref, q_ref, bias_ref, knew_ref, vnew_ref, kc_ref, vc_ref, o_ref, kbuf, vbuf, ksem, vsem, k_scr,
                  v_scr, m_scr, l_scr, acc_scr, *, pages_per_step, n_steps):
    b = pl.program_id(0)
    width = pages_per_step * kbuf.shape[3]
    q = q_ref[0]
    bias = bias_ref[0]

    def fetch(sample, step, slot):
        for m in range(pages_per_step):
            page_id = pt_ref[sample, step * pages_per_step + m]
            _page_copy(kc_ref, page_id, kbuf, slot, m, ksem).start()
            _page_copy(vc_ref, page_id, vbuf, slot, m, vsem).start(priority=1)

    def wait(slot):
        for m in range(pages_per_step):
            _page_copy(kc_ref, 0, kbuf, slot, m, ksem).wait()
            _page_copy(vc_ref, 0, vbuf, slot, m, vsem).wait()

    n_slots = kbuf.shape[0]
    ahead = n_slots - 1

    def fetch_ahead(step, lookahead):
        g = b * n_steps + step + lookahead
        sample_off, step_t = divmod(step + lookahead, n_steps)

        @pl.when(b + sample_off < pl.num_programs(0))
        def _():
            fetch(b + sample_off, step_t, lax.rem(g, n_slots))

    @pl.when(b == 0)
    def _():
        for lookahead in range(ahead):
            fetch_ahead(0, lookahead)

    m_scr[...] = jnp.full(m_scr.shape, NEG, F32)
    l_scr[...] = jnp.zeros(l_scr.shape, F32)
    acc_scr[...] = jnp.zeros(acc_scr.shape, F32)

    def update(s, pv_fn):
        m_prev = m_scr[...]
        m_new = jnp.maximum(m_prev, jnp.max(s, axis=1, keepdims=True))
        alpha = jnp.exp(m_prev - m_new)
        p = jnp.exp(s - m_new[:, 0:1])
        m_scr[...] = m_new
        l_scr[...] = alpha * l_scr[...] + jnp.sum(p, axis=1, keepdims=True)
        acc_scr[...] = alpha[:, 0:1] * acc_scr[...] + pv_fn(p)

    for step in range(n_steps):
        slot = lax.rem(b * n_steps + step, n_slots)
        fetch_ahead(step, ahead)
        wait(slot)
        for m in range(pages_per_step):
            k_scr[:, m * LANE:(m + 1) * LANE] = kbuf[slot, m].astype(BF16)
            v_scr[:, m * LANE:(m + 1) * LANE] = vbuf[slot, m].astype(BF16)
        s = jnp.dot(q, k_scr[...], preferred_element_type=F32) + bias[:, step * width:(step + 1) * width]
        update(s, lambda p: _bdot_nt(p, v_scr[...]))

    past = n_steps * width
    s = _bdot(q, knew_ref[0]) + bias[:, past:past + LANE]
    update(s, lambda p: _bdot_nt(p, vnew_ref[0]))
    o_ref[0] = acc_scr[...] / l_scr[:, 0:1]


def _sample_attention(page_table, qmat, bias3, knew_t, vnew_t, cache_kt, cache_vt, *, pages_per_step):
    db, n_pages = page_table.shape
    kv_w, page = cache_kt.shape[2], cache_kt.shape[3]
    nh = qmat.shape[1]
    n_steps = n_pages // pages_per_step
    width = pages_per_step * page
    assert bias3.shape[2] >= n_steps * width + LANE
    slots = KV_FETCH_SLOTS

    def per_sample(shape):
        return pl.BlockSpec((1,) + tuple(shape[1:]), lambda b, pt: (b, 0, 0))

    grid_spec = pltpu.PrefetchScalarGridSpec(
        num_scalar_prefetch=1,
        grid=(db,),
        in_specs=[per_sample(qmat.shape), per_sample(bias3.shape), per_sample(knew_t.shape), per_sample(vnew_t.shape),
                  pl.BlockSpec(memory_space=pl.ANY), pl.BlockSpec(memory_space=pl.ANY)],
        out_specs=per_sample((db, nh, kv_w)),
        scratch_shapes=[
            pltpu.VMEM((slots, pages_per_step, kv_w, page), F32), pltpu.VMEM((slots, pages_per_step, kv_w, page), F32),
            pltpu.SemaphoreType.DMA((slots,)), pltpu.SemaphoreType.DMA((slots,)),
            pltpu.VMEM((kv_w, width), BF16), pltpu.VMEM((kv_w, width), BF16),
            pltpu.VMEM((nh, LANE), F32), pltpu.VMEM((nh, LANE), F32), pltpu.VMEM((nh, kv_w), F32),
        ],
    )
    return pl.pallas_call(
        functools.partial(_sattn_kernel, pages_per_step=pages_per_step, n_steps=n_steps),
        grid_spec=grid_spec,
        out_shape=jax.ShapeDtypeStruct((db, nh, kv_w), F32),
        compiler_params=_params(1),
        name="sample_attention",
    )(page_table, qmat, bias3, knew_t, vnew_t, cache_kt, cache_vt)


def _sssd_kernel(xbc_ref, prev_ref, misc_ref, z_ref, st_ref, cw_ref, cb_ref, dtb_ref, alog_ref, e_ref, dsk_ref,
                 nw_ref, y_ref, sto_ref, *, d_inner, n_heads):
    gn = SSM_GROUPS * D_STATE
    hpg = n_heads // SSM_GROUPS
    gw = hpg * SSM_HEAD_DIM
    u = xbc_ref[0]
    prev = prev_ref[0]
    conv = cb_ref[...] + cw_ref[CONV_W - 1:CONV_W, :] * u
    for w in range(CONV_W - 1):
        conv = conv + cw_ref[w:w + 1, :] * prev[w:w + 1, :]
    act = _silu(conv)
    xs = act[:, :d_inner]
    bm = act[:, d_inner:d_inner + gn]
    cm = act[:, d_inner + gn:]
    dt = _softplus(misc_ref[0] + dtb_ref[...])
    dec = jnp.exp(dt * (-jnp.exp(alog_ref[...])))
    pad6 = jnp.zeros((SUBLANE - 2, LANE), F32)
    ex = _split3_dot(jnp.concatenate([dt, dec, pad6], axis=0), e_ref[...])
    rows = jnp.concatenate([xs * ex[0:1, :], ex[1:2, :], jnp.zeros((LANE - 2, d_inner), F32)], axis=0)
    cols = rows.T
    pad15 = jnp.zeros((2 * SUBLANE - 1, D_STATE), F32)
    ys = []
    for g in range(SSM_GROUPS):
        s0 = st_ref[0, g * hpg:(g + 1) * hpg].reshape(gw, D_STATE)
        cg = cols[g * gw:(g + 1) * gw, :]
        s1 = s0 * cg[:, 1:2] + cg[:, 0:1] * bm[:, g * D_STATE:(g + 1) * D_STATE]
        sto_ref[0, g * hpg:(g + 1) * hpg] = s1.reshape(hpg, SSM_HEAD_DIM, D_STATE)
        c16 = jnp.concatenate([cm[:, g * D_STATE:(g + 1) * D_STATE], pad15], axis=0)
        ys.append(_bdot_nt(c16, s1)[0:1, :])
    y = jnp.concatenate(ys, axis=-1) + dsk_ref[...] * xs
    y_ref[0] = _gated_group_norm(y, z_ref[0], nw_ref[...], d_inner)


def _sample_ssd(xbc3, prev, misc3, z3, state, conv_w, conv_b, dtb_row, alog_row, e_mat, dsk_row, nw_row, *,
                d_inner, n_heads):
    db = xbc3.shape[0]
    cd = xbc3.shape[2]

    def per(shape):
        n = len(shape)
        return pl.BlockSpec((1,) + tuple(shape[1:]), lambda b: (b,) + (0,) * (n - 1))

    return pl.pallas_call(
        functools.partial(_sssd_kernel, d_inner=d_inner, n_heads=n_heads),
        grid=(db,),
        in_specs=[per(xbc3.shape), per(prev.shape), per(misc3.shape), per(z3.shape), per(state.shape),
                  _const_spec(conv_w.shape), _const_spec(conv_b.shape), _const_spec(dtb_row.shape),
                  _const_spec(alog_row.shape), _const_spec(e_mat.shape), _const_spec(dsk_row.shape),
                  _const_spec(nw_row.shape)],
        out_specs=[per((db, 1, d_inner)), per(state.shape)],
        out_shape=[jax.ShapeDtypeStruct((db, 1, d_inner), F32), jax.ShapeDtypeStruct(state.shape, F32)],
        compiler_params=_params(1),
        name="sample_ssd",
    )(xbc3, prev, misc3, z3, state, conv_w, conv_b, dtb_row, alog_row, e_mat, dsk_row, nw_row)


def _rope_tables(pos):
    inv = ROPE_THETA ** (-jnp.arange(ROT_HALF, dtype=F32) * 2.0 / (2 * ROT_HALF))
    ang = pos.astype(F32)[:, None] * inv[None, :]
    cos, sin = jnp.cos(ang), jnp.sin(ang)
    n = pos.shape[0]
    one = jnp.ones((n, HEAD_DIM - 2 * ROT_HALF), F32)
    zero = jnp.zeros((n, HEAD_DIM - 2 * ROT_HALF), F32)
    z8 = jnp.zeros((n, ROT_HALF), F32)
    c_head = jnp.concatenate([cos, cos, one], axis=1)
    s1_head = jnp.concatenate([-sin, z8, zero], axis=1)
    s2_head = jnp.concatenate([z8, sin, zero], axis=1)
    rep = LANE // HEAD_DIM
    return jnp.tile(c_head, (1, rep)), jnp.tile(s1_head, (1, rep)), jnp.tile(s2_head, (1, rep))


def kernel(x_prompt, x_sample, cache_k, cache_v, cache_idx_k, state_conv, state_ssm, page_table, c_prompt, c_sample,
           w_ada, b_ada, norm1_w, w_in, conv_w, conv_b, dt_bias, a_log, d_skip, ssm_norm_w, w_out, norm2_w, w_up,
           w_down, final_norm_w):
    nb, seq, d = x_prompt.shape
    db, dec_seq, _ = x_sample.shape
    depth = w_in.shape[0]
    assert depth == 1 and dec_seq == 1 and seq % SSD_CHUNK == 0
    n_heads = d // HEAD_DIM
    n_ssm_heads = a_log.shape[1]
    d_inner = n_ssm_heads * SSM_HEAD_DIM
    conv_dim = conv_w.shape[2]
    kv_w = N_KV_HEADS * HEAD_DIM
    iq_w = IDX_HEADS * IDX_DIM
    n_pages, page = page_table.shape[1], cache_k.shape[2]
    past = n_pages * page

    wi = w_in[0]
    offs = np.cumsum([0, d, kv_w, kv_w, iq_w, IDX_DIM, IDX_HEADS, d_inner, conv_dim, n_ssm_heads, d, d])
    misc_w = jnp.concatenate(
        [wi[:, offs[4]:offs[6]], wi[:, offs[8]:offs[9]],
         jnp.zeros((d, LANE - IDX_DIM - IDX_HEADS - n_ssm_heads), F32)], axis=1)
    w_parts = tuple(w.astype(BF16) for w in (wi[:, :offs[4]], misc_w, wi[:, offs[6]:offs[8]], wi[:, offs[9]:]))
    wo, wu, wd = w_out[0].astype(BF16), w_up[0].astype(BF16), w_down[0].astype(BF16)
    nw1, nw2, fnw = norm1_w[0][None, :], norm2_w[0][None, :], final_norm_w[None, :]
    head_lane = jnp.zeros((LANE,), F32)
    dtb_row = head_lane.at[DT_OFF:DT_OFF + n_ssm_heads].set(dt_bias[0])[None, :]
    alog_row = head_lane.at[DT_OFF:DT_OFF + n_ssm_heads].set(a_log[0])[None, :]
    e_mat = jnp.zeros((LANE, d_inner), F32).at[DT_OFF:DT_OFF + n_ssm_heads].set(
        jnp.repeat(jnp.eye(n_ssm_heads, dtype=F32), SSM_HEAD_DIM, axis=1)).astype(BF16)
    dsk_row = jnp.repeat(d_skip[0], SSM_HEAD_DIM)[None, :]
    snw_row = ssm_norm_w[0][None, :]
    cw2, cb2 = conv_w[0], conv_b[0][None, :]

    n_mod = nb + db
    r_mod = -(-n_mod // SUBLANE) * SUBLANE
    c_all = jnp.concatenate([c_prompt, c_sample, jnp.zeros((r_mod - n_mod, d), F32)], axis=0)
    mod = _ada(c_all, w_ada[0], b_ada[0][None, :])
    sh1, sc1, gt1, sh2, sc2, gt2 = [mod[:, k * d:(k + 1) * d] for k in range(6)]

    def pmod(a):
        return a[:nb].reshape(nb, 1, d)

    def smod(a):
        return a[nb:nb + db].reshape(1, db, d)

    xp = x_prompt.reshape(nb * seq, d)
    cos_p, s1_p, s2_p = _rope_tables(jnp.arange(seq))
    (q_hm, kt32, kt, vt32, ve, iqt, ikt32, ikr, cwt, misc_p, z_p, xbc_p, ga_p, gs_p) = _inproj(
        xp, pmod(sc1), pmod(sh1), nw1, w_parts, cos_p, s1_p, s2_p, n_batch=nb, seq=seq, tm=256, prompt=True,
        d_inner=d_inner, conv_dim=conv_dim)
    topk_p = min(TOPK_MAX, seq // 4)
    att_p = _prompt_attention(q_hm, iqt, cwt, kt, ve, ikr, tq=256, topk=topk_p)
    y_p, ssm_p = _prompt_ssd(xbc_p, z_p, misc_p, cw2, cb2, dtb_row.T, alog_row.T, e_mat, dsk_row, snw_row,
                             n_batch=nb, seq=seq, d_inner=d_inner, n_heads=n_ssm_heads)
    out_p = _mlp(xp, att_p, y_p, ga_p, gs_p, pmod(gt1), pmod(sc2), pmod(sh2), pmod(gt2), nw2, fnw, wo, wu, wd,
                 seq=seq, tm=256)

    xs2 = x_sample.reshape(db, d)
    cos_s, s1_s, s2_s = _rope_tables(jnp.full((1,), past, jnp.int32))
    (q_s, k_s, v_s, iq_s, misc_s, z_s, xbc_s, ga_s, gs_s) = _inproj(
        xs2, smod(sc1), smod(sh1), nw1, w_parts, cos_s, s1_s, s2_s, n_batch=1, seq=db, tm=db, prompt=False,
        d_inner=d_inner, conv_dim=conv_dim)
    pps_kv = _largest_divisor(n_pages, KV_PAGES_PER_STEP)
    cw_s = (misc_s[:, IW_OFF:IW_OFF + IDX_HEADS] * (IDX_HEADS ** -0.5) * (IDX_DIM ** -0.5)).reshape(db, IDX_HEADS, 1)

    def as_page_t(a):
        return jnp.pad(a[:, :, None], ((0, 0), (0, 0), (0, page - 1)))

    idx_cache_t = jnp.transpose(cache_idx_k, (0, 1, 3, 2))
    kv_shape_t = (depth, cache_k.shape[1], kv_w, page)
    cache_kt = jnp.transpose(cache_k, (0, 1, 3, 4, 2)).reshape(kv_shape_t)
    cache_vt = jnp.transpose(cache_v, (0, 1, 3, 4, 2)).reshape(kv_shape_t)

    lk = past + pps_kv * page
    sc_s = _sample_index_scores(page_table, iq_s.reshape(db, IDX_HEADS, IDX_DIM), cw_s,
                                as_page_t(misc_s[:, :IDX_DIM]), idx_cache_t, tail=pps_kv * page)
    topk_s = min(TOPK_MAX, (past + 1) // 4)
    bias_s = _sample_select(sc_s.reshape(db, lk), topk=topk_s).reshape(db, 1, lk)
    hpg = n_heads // N_KV_HEADS
    q5 = q_s.reshape(db, N_KV_HEADS, hpg, 1, HEAD_DIM)
    eye = jnp.eye(N_KV_HEADS, dtype=F32)[None, :, None, :, None]
    qmat = (q5 * eye).reshape(db, n_heads, kv_w).astype(BF16)
    acc_s = _sample_attention(page_table, qmat, bias_s, as_page_t(k_s), as_page_t(v_s), cache_kt, cache_vt,
                              pages_per_step=pps_kv)
    a5 = acc_s.reshape(db, N_KV_HEADS, hpg, N_KV_HEADS, HEAD_DIM)
    att_s = jnp.einsum('bghgd->bghd', a5).reshape(db, d)
    y_s, ssm_s = _sample_ssd(xbc_s.reshape(db, 1, conv_dim), state_conv[0], misc_s.reshape(db, 1, LANE),
                             z_s.reshape(db, 1, d_inner), state_ssm[0], cw2, cb2, dtb_row, alog_row, e_mat, dsk_row,
                             snw_row, d_inner=d_inner, n_heads=n_ssm_heads)
    out_s = _mlp(xs2, att_s, y_s.reshape(db, d_inner), ga_s, gs_s, smod(gt1), smod(sc2), smod(sh2), smod(gt2), nw2,
                 fnw, wo, wu, wd, seq=db, tm=db)

    conv_p = xbc_p.reshape(nb, seq, conv_dim)[:, seq - (CONV_W - 1):, :]
    conv_s = jnp.concatenate([state_conv[0][:, 1:, :], xbc_s[:, None, :]], axis=1)
    def kv_out(a_t):
        return jnp.transpose(a_t.reshape(nb, N_KV_HEADS, HEAD_DIM, seq), (0, 3, 1, 2))[None]

    return (
        out_p.reshape(nb, seq, d),
        out_s.reshape(db, 1, d),
        kv_out(kt32),
        kv_out(vt32),
        jnp.transpose(ikt32, (0, 2, 1))[None],
        conv_p[None],
        ssm_p[None],
        k_s.reshape(1, db, 1, N_KV_HEADS, HEAD_DIM),
        v_s.reshape(1, db, 1, N_KV_HEADS, HEAD_DIM),
        misc_s[:, :IDX_DIM].reshape(1, db, 1, IDX_DIM),
        conv_s[None],
        ssm_s[None],
    )
```
